```python
import math
import jax, jax.numpy as jnp
from jax import lax
import numpy as np

D_MODEL = 2048
BATCH = 4
SEQ = 4096
DEPTH = 2

HEAD_DIM = 128
N_HEADS = 8
N_KV_HEADS = 2
IDX_HEADS = 16
IDX_DIM = 64
MAX_TOPK = 256
Q_BLOCK = 128
CONV_CH = 512
CONV_WIDTH = 31
MEM_LEN = 256
MEM_HEADS = 4
FFN_DIM = 4 * D_MODEL
ROPE_THETA = 500000.0
ROT_FRACTION = 4
N_BRANCH = 3
EPS = 1e-6

ATTN_DIM = N_HEADS * HEAD_DIM
KV_DIM = N_KV_HEADS * HEAD_DIM
MEM_DIM = MEM_HEADS * HEAD_DIM
SPLITS = (ATTN_DIM, KV_DIM, KV_DIM, IDX_HEADS * IDX_DIM, IDX_DIM, IDX_HEADS,
          2 * CONV_CH, MEM_DIM, N_BRANCH * D_MODEL)
N_IN = sum(SPLITS)

kernel_name = "hybrid_dsa_conformer_memory_gated_block"


def rms_norm(x, g):
    xf = x.astype(jnp.float32)
    y = xf * lax.rsqrt(jnp.mean(xf * xf, axis=-1, keepdims=True) + EPS)
    return (y * g.astype(jnp.float32)).astype(x.dtype)


def layer_norm(x, g, b):
    xf = x.astype(jnp.float32)
    mu = jnp.mean(xf, axis=-1, keepdims=True)
    var = jnp.mean(jnp.square(xf - mu), axis=-1, keepdims=True)
    y = (xf - mu) * lax.rsqrt(var + EPS)
    return (y * g.astype(jnp.float32) + b.astype(jnp.float32)).astype(x.dtype)


def rope_tables(positions, rot_dim):
    half = rot_dim // 2
    inv_freq = ROPE_THETA ** (-jnp.arange(half, dtype=jnp.float32) * 2.0 / rot_dim)
    ang = positions.astype(jnp.float32)[..., None] * inv_freq
    return jnp.cos(ang)[:, :, None, :], jnp.sin(ang)[:, :, None, :]


def apply_partial_rope(t, cos, sin):
    half = cos.shape[-1]
    rot = 2 * half
    tf = t.astype(jnp.float32)
    t1, t2, tp = tf[..., :half], tf[..., half:rot], tf[..., rot:]
    out = jnp.concatenate([t1 * cos - t2 * sin, t2 * cos + t1 * sin, tp], axis=-1)
    return out.astype(t.dtype)


def dsa_sparse_attention(q, k, v, iq, ik, iw):
    B, S = q.shape[0], q.shape[1]
    topk = min(MAX_TOPK, S // 4)
    nb = S // Q_BLOCK
    group = N_HEADS // N_KV_HEADS
    scale = HEAD_DIM ** -0.5
    idx_scale = (IDX_DIM ** -0.5) * (IDX_HEADS ** -0.5)
    key_pos = jnp.arange(S, dtype=jnp.int32)
    starts = jnp.arange(nb, dtype=jnp.int32) * Q_BLOCK
    gather = jax.vmap(lambda t, i: t[i])

    def to_blocks(t):
        return t.reshape((B, nb, Q_BLOCK) + t.shape[2:]).swapaxes(0, 1)

    def block(args):
        qb, iqb, iwb, t0 = args
        s_idx = jnp.einsum('bqhd,bsd->bqhs', iqb, ik).astype(jnp.float32)
        index = jnp.einsum('bqh,bqhs->bqs', iwb.astype(jnp.float32),
                           jax.nn.relu(s_idx)) * idx_scale
        q_pos = t0 + jnp.arange(Q_BLOCK, dtype=jnp.int32)
        causal = key_pos[None, :] <= q_pos[:, None]
        index = jnp.where(causal[None], index, -jnp.inf)
        sel_val, sel_idx = lax.top_k(index, topk)
        valid = jnp.isfinite(sel_val)
        k_sel = gather(k, sel_idx)
        v_sel = gather(v, sel_idx)
        qg = qb.reshape(B, Q_BLOCK, N_KV_HEADS, group, HEAD_DIM)
        logits = jnp.einsum('bqkgd,bqnkd->bqkgn', qg, k_sel).astype(jnp.float32) * scale
        logits = jnp.where(valid[:, :, None, None, :], logits, -jnp.inf)
        p = jax.nn.softmax(logits, axis=-1).astype(v.dtype)
        o = jnp.einsum('bqkgn,bqnkd->bqkgd', p, v_sel)
        return o.reshape(B, Q_BLOCK, N_HEADS * HEAD_DIM)

    out = lax.map(block, (to_blocks(q), to_blocks(iq), to_blocks(iw), starts))
    return out.swapaxes(0, 1).reshape(B, S, N_HEADS * HEAD_DIM)


def memory_cross_attention(q, mk, mv):
    B, S = q.shape[0], q.shape[1]
    logits = jnp.einsum('bshd,bmhd->bhsm', q, mk).astype(jnp.float32) * (HEAD_DIM ** -0.5)
    p = jax.nn.softmax(logits, axis=-1).astype(mv.dtype)
    o = jnp.einsum('bhsm,bmhd->bshd', p, mv)
    return o.reshape(B, S, MEM_HEADS * HEAD_DIM)


def conformer_conv(glu_in, conv_in_b, conv_w, conv_b, ln_g, ln_b):
    a, gte = jnp.split(glu_in + conv_in_b, 2, axis=-1)
    u = a * jax.nn.sigmoid(gte)
    y = lax.conv_general_dilated(
        u, conv_w[:, None, :].astype(u.dtype), window_strides=(1,),
        padding=[(CONV_WIDTH - 1, 0)],
        dimension_numbers=('NWC', 'WIO', 'NWC'),
        feature_group_count=CONV_CH) + conv_b
    y = layer_norm(y, ln_g, ln_b)
    return jax.nn.silu(y)


def setup_inputs(seed: int = 0) -> dict:
    key = jax.random.key(seed)
    ks = jax.random.split(key, 32)

    def nrm(k, shape, fan_in, extra=1.0):
        return jax.random.normal(k, shape, jnp.float32) * (fan_in ** -0.5) * extra

    def gain(k, shape):
        return 1.0 + 0.1 * jax.random.normal(k, shape, jnp.float32)

    def bias(k, shape):
        return 0.02 * jax.random.normal(k, shape, jnp.float32)

    res_scale = (2 * DEPTH) ** -0.5
    x = jax.random.normal(ks[0], (BATCH, SEQ, D_MODEL), jnp.float32)
    mem = jax.random.normal(ks[1], (BATCH, MEM_LEN, D_MODEL), jnp.float32)
    offsets = jax.random.randint(ks[2], (BATCH, 1), 0, 1024, dtype=jnp.int32)
    positions = offsets + jnp.arange(SEQ, dtype=jnp.int32)[None, :]
    return {
        "x": x,
        "mem": mem,
        "positions": positions,
        "norm1_g": gain(ks[3], (DEPTH, D_MODEL)),
        "w_in": nrm(ks[4], (DEPTH, D_MODEL, N_IN), D_MODEL),
        "q_norm_g": gain(ks[5], (DEPTH, HEAD_DIM)),
        "k_norm_g": gain(ks[6], (DEPTH, HEAD_DIM)),
        "mem_norm_g": gain(ks[7], (DEPTH, D_MODEL)),
        "w_mem_kv": nrm(ks[8], (DEPTH, D_MODEL, 2 * MEM_DIM), D_MODEL),
        "mq_norm_g": gain(ks[9], (DEPTH, HEAD_DIM)),
        "mk_norm_g": gain(ks[10], (DEPTH, HEAD_DIM)),
        "conv_in_b": bias(ks[11], (DEPTH, 2 * CONV_CH)),
        "conv_w": nrm(ks[12], (DEPTH, CONV_WIDTH, CONV_CH), CONV_WIDTH),
        "conv_b": bias(ks[13], (DEPTH, CONV_CH)),
        "conv_ln_g": gain(ks[14], (DEPTH, CONV_CH)),
        "conv_ln_b": bias(ks[15], (DEPTH, CONV_CH)),
        "gate_b": bias(ks[16], (DEPTH, N_BRANCH * D_MODEL)),
        "w_attn_o": nrm(ks[17], (DEPTH, ATTN_DIM, D_MODEL), ATTN_DIM),
        "w_mem_o": nrm(ks[18], (DEPTH, MEM_DIM, D_MODEL), MEM_DIM),
        "w_conv_o": nrm(ks[19], (DEPTH, CONV_CH, D_MODEL), CONV_CH),
        "conv_o_b": bias(ks[20], (DEPTH, D_MODEL)),
        "w_out": nrm(ks[21], (DEPTH, D_MODEL, D_MODEL), D_MODEL, res_scale),
        "norm2_g": gain(ks[22], (DEPTH, D_MODEL)),
        "w_up": nrm(ks[23], (DEPTH, D_MODEL, FFN_DIM), D_MODEL),
        "w_down": nrm(ks[24], (DEPTH, FFN_DIM, D_MODEL), FFN_DIM, res_scale),
    }


def reference(x, mem, positions, norm1_g, w_in, q_norm_g, k_norm_g, mem_norm_g,
              w_mem_kv, mq_norm_g, mk_norm_g, conv_in_b, conv_w, conv_b,
              conv_ln_g, conv_ln_b, gate_b, w_attn_o, w_mem_o, w_conv_o,
              conv_o_b, w_out, norm2_g, w_up, w_down):
    B, S, _ = x.shape
    M = mem.shape[1]
    cos_h, sin_h = rope_tables(positions, HEAD_DIM // ROT_FRACTION)
    cos_i, sin_i = rope_tables(positions, IDX_DIM // ROT_FRACTION)
    bounds = np.cumsum(SPLITS)[:-1].tolist()

    for l in range(DEPTH):
        h = rms_norm(x, norm1_g[l])
        proj = h @ w_in[l]
        q, k, v, iq, ik, iw, glu_in, mq, gates = jnp.split(proj, bounds, axis=-1)

        q = apply_partial_rope(rms_norm(q.reshape(B, S, N_HEADS, HEAD_DIM), q_norm_g[l]), cos_h, sin_h)
        k = apply_partial_rope(rms_norm(k.reshape(B, S, N_KV_HEADS, HEAD_DIM), k_norm_g[l]), cos_h, sin_h)
        v = v.reshape(B, S, N_KV_HEADS, HEAD_DIM)
        iq = apply_partial_rope(iq.reshape(B, S, IDX_HEADS, IDX_DIM), cos_i, sin_i)
        ik = apply_partial_rope(ik[:, :, None, :], cos_i, sin_i)[:, :, 0, :]
        y_attn = dsa_sparse_attention(q, k, v, iq, ik, iw) @ w_attn_o[l]

        m = rms_norm(mem, mem_norm_g[l])
        mkv = (m @ w_mem_kv[l]).reshape(B, M, 2, MEM_HEADS, HEAD_DIM)
        mk = rms_norm(mkv[:, :, 0], mk_norm_g[l])
        mv = mkv[:, :, 1]
        mq = rms_norm(mq.reshape(B, S, MEM_HEADS, HEAD_DIM), mq_norm_g[l])
        y_mem = memory_cross_attention(mq, mk, mv) @ w_mem_o[l]

        y_conv = conformer_conv(glu_in, conv_in_b[l], conv_w[l], conv_b[l],
                                conv_ln_g[l], conv_ln_b[l]) @ w_conv_o[l] + conv_o_b[l]

        g = jax.nn.sigmoid(gates + gate_b[l]).reshape(B, S, N_BRANCH, D_MODEL)
        merged = g[:, :, 0] * y_attn + g[:, :, 1] * y_mem + g[:, :, 2] * y_conv
        x = x + merged @ w_out[l]

        h2 = rms_norm(x, norm2_g[l])
        x = x + jnp.square(jax.nn.relu(h2 @ w_up[l])) @ w_down[l]
    return x
```

```python
import functools
import math

import jax
import jax.numpy as jnp
from jax import lax
from jax.experimental import pallas as pl
from jax.experimental.pallas import tpu as pltpu

F32 = jnp.float32
BF16 = jnp.bfloat16
I32 = jnp.int32

D_MODEL = 2048
HEAD_DIM = 128
N_HEADS = 8
N_KV_HEADS = 2
IDX_HEADS = 16
IDX_DIM = 64
TOPK = 256
CONV_CH = 512
CONV_WIDTH = 31
MEM_HEADS = 4
FFN_DIM = 4 * D_MODEL
ROPE_THETA = 500000.0
N_BRANCH = 3
EPS = 1e-6
ATTN_DIM = N_HEADS * HEAD_DIM
KV_DIM = N_KV_HEADS * HEAD_DIM
MEM_DIM = MEM_HEADS * HEAD_DIM
HEAD_ROT_HALF = HEAD_DIM // 8
IDX_ROT_HALF = IDX_DIM // 8

LANES = 128
VMEM_LIMIT = 56 * 1024 * 1024

P_Q, P_IQ, P_GLU, P_MQ, P_K, P_V, P_GATES = 0, 1024, 2048, 3072, 3584, 3840, 4096
P_WIDTH = P_GATES + N_BRANCH * D_MODEL
PS_WIDTH = LANES

QB = 128
KC = 512
NEG_BIG = -1e30
INT_MIN = -2 ** 31
KEY_NEG_INF = -2139095041
KEY_POS_INF = 2139095040


def _cparams(sem, vmem=VMEM_LIMIT):
    return pltpu.CompilerParams(dimension_semantics=sem, vmem_limit_bytes=vmem)


def _dot(a, b):
    return jnp.dot(a, b, preferred_element_type=F32)


def _dot_nt(a, b):
    return lax.dot_general(a, b, (((1,), (1,)), ((), ())), preferred_element_type=F32)


def _rms(xf, g):
    return xf * lax.rsqrt(jnp.mean(xf * xf, axis=-1, keepdims=True) + EPS) * g


def _rope_table_kernel(pos_ref, ch_ref, sh1_ref, sh2_ref, ci_ref, si1_ref, si2_ref):
    pos = pos_ref[...].astype(F32)
    lane = lax.broadcasted_iota(I32, (1, LANES), 1)

    def tables(period, half, c_ref, s1_ref, s2_ref):
        r = lane & (period - 1)
        fi = (r & (half - 1)).astype(F32)
        inv = jnp.exp(fi * (-math.log(ROPE_THETA) / half))
        ang = pos * inv
        c = jnp.cos(ang)
        s = jnp.sin(ang)
        first = r < half
        second = (r >= half) & (r < 2 * half)
        c_ref[...] = jnp.where(first | second, c, 1.0)
        s1_ref[...] = jnp.where(second, s, 0.0)
        s2_ref[...] = jnp.where(first, -s, 0.0)

    tables(HEAD_DIM, HEAD_ROT_HALF, ch_ref, sh1_ref, sh2_ref)
    tables(IDX_DIM, IDX_ROT_HALF, ci_ref, si1_ref, si2_ref)


def _rope_tables(positions):
    t = positions.size
    tp = 512
    tab = jax.ShapeDtypeStruct((t, LANES), F32)
    spec = pl.BlockSpec((tp, LANES), lambda i: (i, 0))
    return pl.pallas_call(
        _rope_table_kernel,
        grid=(t // tp,),
        in_specs=[pl.BlockSpec((tp, 1), lambda i: (i, 0))],
        out_specs=[spec] * 6,
        out_shape=[tab] * 6,
        compiler_params=_cparams(("parallel",)),
        name="rope_tables",
    )(positions.reshape(t, 1))


def _rope(t, c, s1, s2, half):
    return t * c + pltpu.roll(t, half, 1) * s1 + pltpu.roll(t, LANES - half, 1) * s2


def _proj_kernel(x_ref, g_ref, w_ref, ws_ref, o_ref, os_ref, h_ref):
    @pl.when(pl.program_id(1) == 0)
    def _():
        h = _rms(x_ref[...], g_ref[...]).astype(BF16)
        h_ref[...] = h
        os_ref[...] = _dot(h, ws_ref[...])

    o_ref[...] = _dot(h_ref[...], w_ref[...]).astype(o_ref.dtype)


def _proj(x, g, w, ws):
    t = x.shape[0]
    tm, tn = 1024, 1024
    return pl.pallas_call(
        _proj_kernel,
        grid=(t // tm, P_WIDTH // tn),
        in_specs=[
            pl.BlockSpec((tm, D_MODEL), lambda i, j: (i, 0)),
            pl.BlockSpec((1, D_MODEL), lambda i, j: (0, 0)),
            pl.BlockSpec((D_MODEL, tn), lambda i, j: (0, j)),
            pl.BlockSpec((D_MODEL, PS_WIDTH), lambda i, j: (0, 0)),
        ],
        out_specs=[
            pl.BlockSpec((tm, tn), lambda i, j: (i, j)),
            pl.BlockSpec((tm, PS_WIDTH), lambda i, j: (i, 0)),
        ],
        out_shape=[
            jax.ShapeDtypeStruct((t, P_WIDTH), BF16),
            jax.ShapeDtypeStruct((t, PS_WIDTH), F32),
        ],
        scratch_shapes=[pltpu.VMEM((tm, D_MODEL), BF16)],
        compiler_params=_cparams(("parallel", "arbitrary")),
        name="proj",
    )(x, g, w, ws)


def _prep_kernel(q_ref, iq_ref, kv_ref, ps_ref, ch_ref, sh1_ref, sh2_ref, ci_ref, si1_ref,
                 si2_ref, qg_ref, kg_ref, qn_ref, iqr_ref, kn_ref, vx_ref, ika_ref, ikb_ref,
                 iws_ref):
    ch, sh1, sh2 = ch_ref[...], sh1_ref[...], sh2_ref[...]
    ci, si1, si2 = ci_ref[...], si1_ref[...], si2_ref[...]
    scale = HEAD_DIM ** -0.5
    for h in range(N_HEADS):
        sl = slice(h * HEAD_DIM, (h + 1) * HEAD_DIM)
        qh = _rms(q_ref[:, sl].astype(F32), qg_ref[...])
        qn_ref[:, sl] = (_rope(qh, ch, sh1, sh2, HEAD_ROT_HALF) * scale).astype(BF16)
        iqh = iq_ref[:, sl].astype(F32)
        iqr_ref[:, sl] = _rope(iqh, ci, si1, si2, IDX_ROT_HALF).astype(BF16)
    ones = jnp.ones((kv_ref.shape[0], HEAD_DIM), BF16)
    for g in range(N_KV_HEADS):
        sl = slice(g * HEAD_DIM, (g + 1) * HEAD_DIM)
        kh = _rms(kv_ref[:, sl].astype(F32), kg_ref[...])
        kn_ref[:, sl] = _rope(kh, ch, sh1, sh2, HEAD_ROT_HALF).astype(BF16)
        vx_ref[:, 2 * g * HEAD_DIM:(2 * g + 1) * HEAD_DIM] = kv_ref[:, KV_DIM + g * HEAD_DIM:
                                                                    KV_DIM + (g + 1) * HEAD_DIM]
        vx_ref[:, (2 * g + 1) * HEAD_DIM:(2 * g + 2) * HEAD_DIM] = ones
    ps = ps_ref[...]
    lane = lax.broadcasted_iota(I32, (1, LANES), 1)
    ikr = jnp.where(lane < IDX_DIM, _rope(ps, ci, si1, si2, IDX_ROT_HALF), 0.0)
    ika_ref[...] = ikr.astype(BF16)
    ikb_ref[...] = pltpu.roll(ikr, IDX_DIM, 1).astype(BF16)
    idx_scale = (IDX_DIM ** -0.5) * (IDX_HEADS ** -0.5)
    iws_ref[...] = jnp.where(lane < IDX_HEADS, pltpu.roll(ps, LANES - IDX_DIM, 1) * idx_scale, 0.0)


def _prep(p, ps, tabs, qg, kg):
    t = p.shape[0]
    tp = 512
    tab_spec = pl.BlockSpec((tp, LANES), lambda i: (i, 0))
    vec_spec = pl.BlockSpec((1, HEAD_DIM), lambda i: (0, 0))

    def out(width, dtype):
        return (pl.BlockSpec((tp, width), lambda i: (i, 0)), jax.ShapeDtypeStruct((t, width), dtype))

    outs = [out(ATTN_DIM, BF16), out(ATTN_DIM, BF16), out(KV_DIM, BF16), out(2 * KV_DIM, BF16),
            out(LANES, BF16), out(LANES, BF16), out(LANES, F32)]
    return pl.pallas_call(
        _prep_kernel,
        grid=(t // tp,),
        in_specs=[
            pl.BlockSpec((tp, ATTN_DIM), lambda i: (i, P_Q // ATTN_DIM)),
            pl.BlockSpec((tp, ATTN_DIM), lambda i: (i, P_IQ // ATTN_DIM)),
            pl.BlockSpec((tp, 2 * KV_DIM), lambda i: (i, P_K // (2 * KV_DIM))),
            tab_spec,
        ] + [tab_spec] * 6 + [vec_spec, vec_spec],
        out_specs=[o[0] for o in outs],
        out_shape=[o[1] for o in outs],
        compiler_params=_cparams(("parallel",)),
        name="prep",
    )(p, p, p, ps, *tabs, qg, kg)


def _sortable(x):
    bits = lax.bitcast_convert_type(x, I32)
    return bits ^ ((bits >> 31) & 0x7FFFFFFF)


def _dsa_kernel(qn_ref, iq_ref, iw_ref, kn_ref, vx_ref, ika_ref, ikb_ref, o_ref,
                key_ref, bias_ref, m_ref, acc_ref):
    j = pl.program_id(1)
    nck = j // (KC // QB) + 1
    half = KC // 2

    iw = iw_ref[...]
    q_pos = j * QB + lax.broadcasted_iota(I32, (QB, half), 0)

    def score_chunk(c, carry):
        for hf in range(2):
            start = pl.multiple_of(c * KC + hf * half, half)
            ka = ika_ref[pl.ds(start, half), :]
            kb = ikb_ref[pl.ds(start, half), :]
            acc = jnp.zeros((QB, half), F32)
            for g in range(IDX_HEADS // 2):
                iqg = iq_ref[:, g * LANES:(g + 1) * LANES]
                s0 = _dot_nt(iqg, ka)
                s1 = _dot_nt(iqg, kb)
                acc = acc + jnp.maximum(s0, 0.0) * iw[:, 2 * g:2 * g + 1]
                acc = acc + jnp.maximum(s1, 0.0) * iw[:, 2 * g + 1:2 * g + 2]
            k_pos = start + lax.broadcasted_iota(I32, (QB, half), 1)
            acc = jnp.where(k_pos <= q_pos, acc, -jnp.inf)
            key_ref[c, :, hf * half:(hf + 1) * half] = _sortable(acc)
        return carry

    lax.fori_loop(0, nck, score_chunk, 0)

    def search_pass(p, tu):
        bit = lax.shift_left(jnp.int32(1), 31 - p)
        cand_u = tu | bit
        cand = jnp.broadcast_to(cand_u ^ INT_MIN, (QB, KC))

        def count_chunk(c, cnt):
            return cnt + jnp.where(key_ref[c] >= cand, 1, 0)

        cnt = lax.fori_loop(0, nck, count_chunk, jnp.zeros((QB, KC), I32))
        tot = jnp.sum(cnt, axis=1, keepdims=True)
        return jnp.where(tot >= TOPK, cand_u, tu)

    tu = lax.fori_loop(0, 32, search_pass, jnp.zeros((QB, 1), I32))
    thr = jnp.broadcast_to(jnp.maximum(tu ^ INT_MIN, KEY_NEG_INF + 1), (QB, KC))

    def bias_chunk(c, carry):
        key = key_ref[c]
        sel = (key >= thr) & (key < KEY_POS_INF)
        bias_ref[c] = jnp.where(sel, 0.0, NEG_BIG)
        return carry

    lax.fori_loop(0, nck, bias_chunk, 0)

    m_ref[...] = jnp.full(m_ref.shape, NEG_BIG, F32)
    acc_ref[...] = jnp.zeros(acc_ref.shape, F32)

    def attn_chunk(c, carry):
        start = pl.multiple_of(c * KC, KC)
        bias = bias_ref[c]
        for h in range(N_HEADS):
            g = h // (N_HEADS // N_KV_HEADS)
            kc = kn_ref[pl.ds(start, KC), g * HEAD_DIM:(g + 1) * HEAD_DIM]
            vc = vx_ref[pl.ds(start, KC), 2 * g * HEAD_DIM:(2 * g + 2) * HEAD_DIM]
            qh = qn_ref[:, h * HEAD_DIM:(h + 1) * HEAD_DIM]
            s = _dot_nt(qh, kc) + bias
            m_prev = m_ref[h]
            m_new = jnp.maximum(m_prev, jnp.max(s, axis=1, keepdims=True))
            alpha = jnp.exp(m_prev - m_new)
            p = jnp.exp(s - m_new[:, 0:1])
            m_ref[h] = m_new
            acc_ref[h] = (acc_ref[h] * jnp.concatenate([alpha, alpha], axis=1)
                          + _dot(p.astype(BF16), vc))
        return carry

    lax.fori_loop(0, nck, attn_chunk, 0)

    for h in range(N_HEADS):
        a = acc_ref[h]
        o_ref[:, h * HEAD_DIM:(h + 1) * HEAD_DIM] = (a[:, :HEAD_DIM] / a[:, HEAD_DIM:]).astype(BF16)


def _dsa(qn, iqr, iws, kn, vx, ika, ikb, batch, seq):
    t = qn.shape[0]
    nb = seq // QB
    nchunk = seq // KC
    qmap = lambda b, j: (b * nb + j, 0)
    bmap = lambda b, j: (b, 0)
    return pl.pallas_call(
        _dsa_kernel,
        grid=(batch, nb),
        in_specs=[
            pl.BlockSpec((QB, ATTN_DIM), qmap),
            pl.BlockSpec((QB, ATTN_DIM), qmap),
            pl.BlockSpec((QB, LANES), qmap),
            pl.BlockSpec((seq, KV_DIM), bmap),
            pl.BlockSpec((seq, 2 * KV_DIM), bmap),
            pl.BlockSpec((seq, LANES), bmap),
            pl.BlockSpec((seq, LANES), bmap),
        ],
        out_specs=pl.BlockSpec((QB, ATTN_DIM), qmap),
        out_shape=jax.ShapeDtypeStruct((t, ATTN_DIM), BF16),
        scratch_shapes=[
            pltpu.VMEM((nchunk, QB, KC), I32),
            pltpu.VMEM((nchunk, QB, KC), F32),
            pltpu.VMEM((N_HEADS, QB, HEAD_DIM), F32),
            pltpu.VMEM((N_HEADS, QB, 2 * HEAD_DIM), F32),
        ],
        compiler_params=_cparams(("parallel", "arbitrary")),
        name="dsa",
    )(qn, iqr, iws, kn, vx, ika, ikb)


def _mem_kv_kernel(mem_ref, g_ref, w_ref, kg_ref, mk_ref, mv_ref):
    m = _rms(mem_ref[...], g_ref[...]).astype(BF16)
    kv = _dot(m, w_ref[...])
    for h in range(MEM_HEADS):
        sl = slice(h * HEAD_DIM, (h + 1) * HEAD_DIM)
        mk_ref[:, sl] = _rms(kv[:, sl], kg_ref[...]).astype(BF16)
    mv_ref[...] = kv[:, MEM_DIM:].astype(BF16)


def _mem_kv(mem2d, g, w, kg, batch):
    n = mem2d.shape[0]
    m = n // batch
    out = jax.ShapeDtypeStruct((n, MEM_DIM), BF16)
    ospec = pl.BlockSpec((m, MEM_DIM), lambda b: (b, 0))
    return pl.pallas_call(
        _mem_kv_kernel,
        grid=(batch,),
        in_specs=[
            pl.BlockSpec((m, D_MODEL), lambda b: (b, 0)),
            pl.BlockSpec((1, D_MODEL), lambda b: (0, 0)),
            pl.BlockSpec((D_MODEL, 2 * MEM_DIM), lambda b: (0, 0)),
            pl.BlockSpec((1, HEAD_DIM), lambda b: (0, 0)),
        ],
        out_specs=[ospec, ospec],
        out_shape=[out, out],
        compiler_params=_cparams(("parallel",)),
        name="mem_kv",
    )(mem2d, g, w, kg)


def _mem_attn_kernel(q_ref, mk_ref, mv_ref, qg_ref, o_ref):
    scale = HEAD_DIM ** -0.5
    for h in range(MEM_HEADS):
        sl = slice(h * HEAD_DIM, (h + 1) * HEAD_DIM)
        qh = (_rms(q_ref[:, sl].astype(F32), qg_ref[...]) * scale).astype(BF16)
        s = _dot_nt(qh, mk_ref[:, sl])
        p = jnp.exp(s - jnp.max(s, axis=1, keepdims=True))
        l = jnp.sum(p, axis=1, keepdims=True)
        o = _dot((p / l).astype(BF16), mv_ref[:, sl])
        o_ref[:, sl] = o.astype(BF16)


def _mem_attn(p, mk, mv, qg, batch, seq):
    t = p.shape[0]
    tq = 512
    nq = seq // tq
    m = mk.shape[0] // batch
    return pl.pallas_call(
        _mem_attn_kernel,
        grid=(batch, nq),
        in_specs=[
            pl.BlockSpec((tq, MEM_DIM), lambda b, i: (b * nq + i, P_MQ // MEM_DIM)),
            pl.BlockSpec((m, MEM_DIM), lambda b, i: (b, 0)),
            pl.BlockSpec((m, MEM_DIM), lambda b, i: (b, 0)),
            pl.BlockSpec((1, HEAD_DIM), lambda b, i: (0, 0)),
        ],
        out_specs=pl.BlockSpec((tq, MEM_DIM), lambda b, i: (b * nq + i, 0)),
        out_shape=jax.ShapeDtypeStruct((t, MEM_DIM), BF16),
        compiler_params=_cparams(("parallel", "parallel")),
        name="mem_attn",
    )(p, mk, mv, qg)


CONV_HALO = 32
CONV_ROWS = 64


def _conv_kernel(cur_ref, prev_ref, inb_ref, cw_ref, cb_ref, lg_ref, lb_ref, o_ref, u_ref):
    ts = cur_ref.shape[0]

    def glu(x):
        x = x.astype(F32) + inb_ref[...]
        return x[:, :CONV_CH] * jax.nn.sigmoid(x[:, CONV_CH:])

    u_prev = glu(prev_ref[ts - CONV_HALO:, :])
    u_ref[:CONV_HALO, :] = jnp.where(pl.program_id(1) == 0, 0.0, u_prev)
    u_ref[CONV_HALO:, :] = glu(cur_ref[...])

    first_tap = CONV_HALO - (CONV_WIDTH - 1)
    for r in range(ts // CONV_ROWS):
        base = r * CONV_ROWS
        y = jnp.broadcast_to(cb_ref[...], (CONV_ROWS, CONV_CH))
        for w in range(CONV_WIDTH):
            y = y + u_ref[base + first_tap + w:base + first_tap + w + CONV_ROWS, :] * cw_ref[w:w + 1, :]
        mu = jnp.mean(y, axis=-1, keepdims=True)
        d = y - mu
        var = jnp.mean(d * d, axis=-1, keepdims=True)
        z = d * lax.rsqrt(var + EPS) * lg_ref[...] + lb_ref[...]
        o_ref[base:base + CONV_ROWS, :] = (z * jax.nn.sigmoid(z)).astype(BF16)


def _conv(p, inb, cw, cb, lg, lb, batch, seq):
    t = p.shape[0]
    ts = 512
    ns = seq // ts
    glu_blk = P_GLU // (2 * CONV_CH)
    vec = lambda w: pl.BlockSpec((1, w), lambda b, i: (0, 0))
    return pl.pallas_call(
        _conv_kernel,
        grid=(batch, ns),
        in_specs=[
            pl.BlockSpec((ts, 2 * CONV_CH), lambda b, i: (b * ns + i, glu_blk)),
            pl.BlockSpec((ts, 2 * CONV_CH), lambda b, i: (b * ns + jnp.maximum(i - 1, 0), glu_blk)),
            vec(2 * CONV_CH),
            pl.BlockSpec((CONV_WIDTH, CONV_CH), lambda b, i: (0, 0)),
            vec(CONV_CH), vec(CONV_CH), vec(CONV_CH),
        ],
        out_specs=pl.BlockSpec((ts, CONV_CH), lambda b, i: (b * ns + i, 0)),
        out_shape=jax.ShapeDtypeStruct((t, CONV_CH), BF16),
        scratch_shapes=[pltpu.VMEM((CONV_HALO + ts, CONV_CH), F32)],
        compiler_params=_cparams(("parallel", "parallel")),
        name="conv",
    )(p, p, inb, cw, cb, lg, lb)


MERGE_TN = 512


def _merge_kernel(a_ref, m_ref, c_ref, g0_ref, g1_ref, g2_ref, gb0_ref, gb1_ref, gb2_ref,
                  wa_ref, wm_ref, wc_ref, cob_ref, o_ref):
    a, m, c = a_ref[...], m_ref[...], c_ref[...]
    for n in range(D_MODEL // MERGE_TN):
        sl = slice(n * MERGE_TN, (n + 1) * MERGE_TN)

        def gate(g_ref, gb_ref):
            return jax.nn.sigmoid(g_ref[:, sl].astype(F32) + gb_ref[:, sl])

        y = gate(g0_ref, gb0_ref) * _dot(a, wa_ref[:, sl])
        y = y + gate(g1_ref, gb1_ref) * _dot(m, wm_ref[:, sl])
        y = y + gate(g2_ref, gb2_ref) * (_dot(c, wc_ref[:, sl]) + cob_ref[:, sl])
        o_ref[:, sl] = y.astype(BF16)


def _merge(attn, memo, convo, p, gate_b, wa, wm, wc, cob):
    t = attn.shape[0]
    tm = 512
    gblk = P_GATES // D_MODEL
    row = lambda w: pl.BlockSpec((tm, w), lambda i: (i, 0))
    full = lambda r, c: pl.BlockSpec((r, c), lambda i: (0, 0))
    gspec = lambda k: pl.BlockSpec((tm, D_MODEL), lambda i: (i, gblk + k))
    gbspec = lambda k: pl.BlockSpec((1, D_MODEL), lambda i: (0, k))
    return pl.pallas_call(
        _merge_kernel,
        grid=(t // tm,),
        in_specs=[row(ATTN_DIM), row(MEM_DIM), row(CONV_CH),
                  gspec(0), gspec(1), gspec(2), gbspec(0), gbspec(1), gbspec(2),
                  full(ATTN_DIM, D_MODEL), full(MEM_DIM, D_MODEL), full(CONV_CH, D_MODEL),
                  full(1, D_MODEL)],
        out_specs=row(D_MODEL),
        out_shape=jax.ShapeDtypeStruct((t, D_MODEL), BF16),
        compiler_params=_cparams(("parallel",)),
        name="merge",
    )(attn, memo, convo, p, p, p, gate_b, gate_b, gate_b, wa, wm, wc, cob)


def _outproj_kernel(x_ref, mg_ref, w_ref, g_ref, x1_ref, h2_ref):
    x1 = x_ref[...] + _dot(mg_ref[...], w_ref[...])
    x1_ref[...] = x1
    h2_ref[...] = _rms(x1, g_ref[...]).astype(BF16)


def _outproj(x, merged, w, g):
    t = x.shape[0]
    tm = 512
    row = pl.BlockSpec((tm, D_MODEL), lambda i: (i, 0))
    return pl.pallas_call(
        _outproj_kernel,
        grid=(t // tm,),
        in_specs=[row, row,
                  pl.BlockSpec((D_MODEL, D_MODEL), lambda i: (0, 0)),
                  pl.BlockSpec((1, D_MODEL), lambda i: (0, 0))],
        out_specs=[row, row],
        out_shape=[jax.ShapeDtypeStruct((t, D_MODEL), F32), jax.ShapeDtypeStruct((t, D_MODEL), BF16)],
        compiler_params=_cparams(("parallel",)),
        name="outproj",
    )(x, merged, w, g)


def _mlp_kernel(h_ref, x_ref, wu_ref, wd_ref, o_ref):
    @pl.when(pl.program_id(1) == 0)
    def _():
        o_ref[...] = x_ref[...]

    u = jnp.maximum(_dot(h_ref[...], wu_ref[...]), 0.0)
    o_ref[...] += _dot((u * u).astype(BF16), wd_ref[...])


def _mlp(h2, x1, wu, wd):
    t = x1.shape[0]
    tm, tf = 512, 1024
    return pl.pallas_call(
        _mlp_kernel,
        grid=(t // tm, FFN_DIM // tf),
        in_specs=[
            pl.BlockSpec((tm, D_MODEL), lambda i, f: (i, 0)),
            pl.BlockSpec((tm, D_MODEL), lambda i, f: (i, 0)),
            pl.BlockSpec((D_MODEL, tf), lambda i, f: (0, f)),
            pl.BlockSpec((tf, D_MODEL), lambda i, f: (f, 0)),
        ],
        out_specs=pl.BlockSpec((tm, D_MODEL), lambda i, f: (i, 0)),
        out_shape=jax.ShapeDtypeStruct((t, D_MODEL), F32),
        compiler_params=_cparams(("parallel", "arbitrary")),
        name="mlp",
    )(h2, x1, wu, wd)


def _regroup_w_in(w):
    o_k = ATTN_DIM
    o_v = o_k + KV_DIM
    o_iq = o_v + KV_DIM
    o_ik = o_iq + IDX_HEADS * IDX_DIM
    o_iw = o_ik + IDX_DIM
    o_glu = o_iw + IDX_HEADS
    o_mq = o_glu + 2 * CONV_CH
    o_g = o_mq + MEM_DIM
    main = jnp.concatenate([w[:, :o_k], w[:, o_iq:o_ik], w[:, o_glu:o_mq], w[:, o_mq:o_g],
                            w[:, o_k:o_v], w[:, o_v:o_iq], w[:, o_g:]], axis=1).astype(BF16)
    small = jnp.pad(w[:, o_ik:o_glu], ((0, 0), (0, PS_WIDTH - IDX_DIM - IDX_HEADS))).astype(BF16)
    return main, small


def kernel(x, mem, positions, norm1_g, w_in, q_norm_g, k_norm_g, mem_norm_g, w_mem_kv, mq_norm_g,
           mk_norm_g, conv_in_b, conv_w, conv_b, conv_ln_g, conv_ln_b, gate_b, w_attn_o, w_mem_o,
           w_conv_o, conv_o_b, w_out, norm2_g, w_up, w_down):
    batch, seq, _ = x.shape
    depth = w_in.shape[0]
    t = batch * seq
    xf = x.reshape(t, D_MODEL)
    mem2d = mem.reshape(batch * mem.shape[1], D_MODEL)
    tabs = _rope_tables(positions)
    row = lambda v: v.reshape(1, -1)

    for l in range(depth):
        w_main, w_small = _regroup_w_in(w_in[l])
        p, ps = _proj(xf, row(norm1_g[l]), w_main, w_small)
        qn, iqr, kn, vx, ika, ikb, iws = _prep(p, ps, tabs, row(q_norm_g[l]), row(k_norm_g[l]))
        attn = _dsa(qn, iqr, iws, kn, vx, ika, ikb, batch, seq)
        mk, mv = _mem_kv(mem2d, row(mem_norm_g[l]), w_mem_kv[l].astype(BF16), row(mk_norm_g[l]), batch)
        memo = _mem_attn(p, mk, mv, row(mq_norm_g[l]), batch, seq)
        convo = _conv(p, row(conv_in_b[l]), conv_w[l], row(conv_b[l]), row(conv_ln_g[l]),
                      row(conv_ln_b[l]), batch, seq)
        merged = _merge(attn, memo, convo, p, row(gate_b[l]), w_attn_o[l].astype(BF16),
                        w_mem_o[l].astype(BF16), w_conv_o[l].astype(BF16), row(conv_o_b[l]))
        x1, h2 = _outproj(xf, merged, w_out[l].astype(BF16), row(norm2_g[l]))
        xf = _mlp(h2, x1, w_up[l].astype(BF16), w_down[l].astype(BF16))
    return xf.reshape(batch, seq, D_MODEL)
```

```python
import functools
import math

import jax
import jax.numpy as jnp
from jax import lax
from jax.experimental import pallas as pl
from jax.experimental.pallas import tpu as pltpu

F32 = jnp.float32
BF16 = jnp.bfloat16
I32 = jnp.int32

D_MODEL = 2048
HEAD_DIM = 128
N_HEADS = 8
N_KV_HEADS = 2
IDX_HEADS = 16
IDX_DIM = 64
TOPK = 256
CONV_CH = 512
CONV_WIDTH = 31
MEM_HEADS = 4
FFN_DIM = 4 * D_MODEL
ROPE_THETA = 500000.0
N_BRANCH = 3
EPS = 1e-6
ATTN_DIM = N_HEADS * HEAD_DIM
KV_DIM = N_KV_HEADS * HEAD_DIM
MEM_DIM = MEM_HEADS * HEAD_DIM
HEAD_ROT_HALF = HEAD_DIM // 8
IDX_ROT_HALF = IDX_DIM // 8

LANES = 128
VMEM_LIMIT = 56 * 1024 * 1024

P_Q, P_IQ, P_GLU, P_MQ, P_K, P_V, P_GATES = 0, 1024, 2048, 3072, 3584, 3840, 4096
P_WIDTH = P_GATES + N_BRANCH * D_MODEL
PS_WIDTH = LANES

QB = 128
KC = 512
SUB = 256
SUBLANES = 8
LOG2E = 1.4426950408889634
NEG_BIG = -1e30
INT_MIN = -2 ** 31
KEY_NEG_INF = -2139095041
KEY_POS_INF = 2139095040


def _cparams(sem, vmem=VMEM_LIMIT):
    return pltpu.CompilerParams(dimension_semantics=sem, vmem_limit_bytes=vmem)


def _dot(a, b):
    return jnp.dot(a, b, preferred_element_type=F32)


def _dot_nt(a, b):
    return lax.dot_general(a, b, (((1,), (1,)), ((), ())), preferred_element_type=F32)


def _rms(xf, g):
    return xf * lax.rsqrt(jnp.mean(xf * xf, axis=-1, keepdims=True) + EPS) * g


def _rope_table_kernel(pos_ref, ch_ref, sh1_ref, sh2_ref, ci_ref, si1_ref, si2_ref):
    pos = pos_ref[...].astype(F32)
    lane = lax.broadcasted_iota(I32, (1, LANES), 1)

    def tables(period, half, c_ref, s1_ref, s2_ref):
        r = lane & (period - 1)
        fi = (r & (half - 1)).astype(F32)
        inv = jnp.exp(fi * (-math.log(ROPE_THETA) / half))
        ang = pos * inv
        c = jnp.cos(ang)
        s = jnp.sin(ang)
        first = r < half
        second = (r >= half) & (r < 2 * half)
        c_ref[...] = jnp.where(first | second, c, 1.0)
        s1_ref[...] = jnp.where(second, s, 0.0)
        s2_ref[...] = jnp.where(first, -s, 0.0)

    tables(HEAD_DIM, HEAD_ROT_HALF, ch_ref, sh1_ref, sh2_ref)
    tables(IDX_DIM, IDX_ROT_HALF, ci_ref, si1_ref, si2_ref)


def _rope_tables(positions):
    t = positions.size
    tp = 512
    tab = jax.ShapeDtypeStruct((t, LANES), F32)
    spec = pl.BlockSpec((tp, LANES), lambda i: (i, 0))
    return pl.pallas_call(
        _rope_table_kernel,
        grid=(t // tp,),
        in_specs=[pl.BlockSpec((tp, 1), lambda i: (i, 0))],
        out_specs=[spec] * 6,
        out_shape=[tab] * 6,
        compiler_params=_cparams(("parallel",)),
        name="rope_tables",
    )(positions.reshape(t, 1))


def _rope(t, c, s1, s2, half):
    return t * c + pltpu.roll(t, half, 1) * s1 + pltpu.roll(t, LANES - half, 1) * s2


def _proj_kernel(x_ref, g_ref, w_ref, ws_ref, o_ref, os_ref, h_ref):
    @pl.when(pl.program_id(1) == 0)
    def _():
        h = _rms(x_ref[...], g_ref[...]).astype(BF16)
        h_ref[...] = h
        os_ref[...] = _dot(h, ws_ref[...])

    o_ref[...] = _dot(h_ref[...], w_ref[...]).astype(o_ref.dtype)


def _proj(x, g, w, ws):
    t = x.shape[0]
    tm, tn = 1024, 1024
    return pl.pallas_call(
        _proj_kernel,
        grid=(t // tm, P_WIDTH // tn),
        in_specs=[
            pl.BlockSpec((tm, D_MODEL), lambda i, j: (i, 0)),
            pl.BlockSpec((1, D_MODEL), lambda i, j: (0, 0)),
            pl.BlockSpec((D_MODEL, tn), lambda i, j: (0, j)),
            pl.BlockSpec((D_MODEL, PS_WIDTH), lambda i, j: (0, 0)),
        ],
        out_specs=[
            pl.BlockSpec((tm, tn), lambda i, j: (i, j)),
            pl.BlockSpec((tm, PS_WIDTH), lambda i, j: (i, 0)),
        ],
        out_shape=[
            jax.ShapeDtypeStruct((t, P_WIDTH), BF16),
            jax.ShapeDtypeStruct((t, PS_WIDTH), F32),
        ],
        scratch_shapes=[pltpu.VMEM((tm, D_MODEL), BF16)],
        compiler_params=_cparams(("parallel", "arbitrary")),
        name="proj",
    )(x, g, w, ws)


def _prep_kernel(q_ref, iq_ref, kv_ref, ps_ref, ch_ref, sh1_ref, sh2_ref, ci_ref, si1_ref,
                 si2_ref, qg_ref, kg_ref, qn_ref, iqr_ref, kn_ref, vt_ref, ika_ref, ikb_ref,
                 iwt_ref):
    ch, sh1, sh2 = ch_ref[...], sh1_ref[...], sh2_ref[...]
    ci, si1, si2 = ci_ref[...], si1_ref[...], si2_ref[...]
    scale = (HEAD_DIM ** -0.5) * LOG2E
    for h in range(N_HEADS):
        sl = slice(h * HEAD_DIM, (h + 1) * HEAD_DIM)
        qh = _rms(q_ref[:, sl].astype(F32), qg_ref[...])
        qn_ref[:, sl] = (_rope(qh, ch, sh1, sh2, HEAD_ROT_HALF) * scale).astype(BF16)
        iqh = iq_ref[:, sl].astype(F32)
        iqr_ref[:, sl] = _rope(iqh, ci, si1, si2, IDX_ROT_HALF).astype(BF16)
    for g in range(N_KV_HEADS):
        sl = slice(g * HEAD_DIM, (g + 1) * HEAD_DIM)
        kh = _rms(kv_ref[:, sl].astype(F32), kg_ref[...])
        kn_ref[:, sl] = _rope(kh, ch, sh1, sh2, HEAD_ROT_HALF).astype(BF16)
        vh = kv_ref[:, KV_DIM + g * HEAD_DIM:KV_DIM + (g + 1) * HEAD_DIM].astype(F32)
        vt_ref[0, sl, :] = vh.T.astype(BF16)
    ps = ps_ref[...]
    lane = lax.broadcasted_iota(I32, (1, LANES), 1)
    ikr = jnp.where(lane < IDX_DIM, _rope(ps, ci, si1, si2, IDX_ROT_HALF), 0.0)
    ika_ref[...] = ikr.astype(BF16)
    ikb_ref[...] = pltpu.roll(ikr, IDX_DIM, 1).astype(BF16)
    idx_scale = (IDX_DIM ** -0.5) * (IDX_HEADS ** -0.5)
    iws = jnp.where(lane < IDX_HEADS, pltpu.roll(ps, LANES - IDX_DIM, 1) * idx_scale, 0.0)
    iwt_ref[...] = iws.T


def _prep(p, ps, tabs, qg, kg):
    t = p.shape[0]
    tp = KC
    tab_spec = pl.BlockSpec((tp, LANES), lambda i: (i, 0))
    vec_spec = pl.BlockSpec((1, HEAD_DIM), lambda i: (0, 0))

    def out(width, dtype):
        return (pl.BlockSpec((tp, width), lambda i: (i, 0)), jax.ShapeDtypeStruct((t, width), dtype))

    outs = [out(ATTN_DIM, BF16), out(ATTN_DIM, BF16), out(KV_DIM, BF16),
            (pl.BlockSpec((1, KV_DIM, tp), lambda i: (i, 0, 0)),
             jax.ShapeDtypeStruct((t // tp, KV_DIM, tp), BF16)),
            out(LANES, BF16), out(LANES, BF16),
            (pl.BlockSpec((LANES, tp), lambda i: (0, i)), jax.ShapeDtypeStruct((LANES, t), F32))]
    return pl.pallas_call(
        _prep_kernel,
        grid=(t // tp,),
        in_specs=[
            pl.BlockSpec((tp, ATTN_DIM), lambda i: (i, P_Q // ATTN_DIM)),
            pl.BlockSpec((tp, ATTN_DIM), lambda i: (i, P_IQ // ATTN_DIM)),
            pl.BlockSpec((tp, 2 * KV_DIM), lambda i: (i, P_K // (2 * KV_DIM))),
            tab_spec,
        ] + [tab_spec] * 6 + [vec_spec, vec_spec],
        out_specs=[o[0] for o in outs],
        out_shape=[o[1] for o in outs],
        compiler_params=_cparams(("parallel",)),
        name="prep",
    )(p, p, p, ps, *tabs, qg, kg)


def _sortable(x):
    bits = lax.bitcast_convert_type(x, I32)
    return bits ^ ((bits >> 31) & 0x7FFFFFFF)


N_PAIRS = N_HEADS // 2
N_PART = 4


def _dsa_kernel(qn_ref, iq_ref, iwt_ref, kn_ref, vt_ref, ika_ref, ikb_ref, o_ref,
                key_ref, bias_ref, xlim_ref, m_ref, l_ref, acc_ref, s_ref, p_ref):
    j = pl.program_id(1)
    nck = j // (KC // QB) + 1
    seq = kn_ref.shape[0]

    iwt = iwt_ref[...]
    q_pos = j * QB + lax.broadcasted_iota(I32, (SUB, QB), 1)
    n_grp = IDX_HEADS // 2

    def score_chunk(c, carry):
        for hf in range(KC // SUB):
            start = pl.multiple_of(c * KC + hf * SUB, SUB)
            ka = ika_ref[pl.ds(start, SUB), :]
            kb = ikb_ref[pl.ds(start, SUB), :]
            acc = jnp.zeros((SUB, QB), F32)
            for a in range(n_grp // 2):
                b = a + n_grp // 2
                iq2 = jnp.concatenate([iq_ref[:, a * LANES:(a + 1) * LANES],
                                       iq_ref[:, b * LANES:(b + 1) * LANES]], axis=0)
                sa = _dot_nt(ka, iq2)
                sb = _dot_nt(kb, iq2)
                acc = acc + jnp.maximum(sa[:, :QB], 0.0) * iwt[2 * a:2 * a + 1, :]
                acc = acc + jnp.maximum(sb[:, :QB], 0.0) * iwt[2 * a + 1:2 * a + 2, :]
                acc = acc + jnp.maximum(sa[:, QB:], 0.0) * iwt[2 * b:2 * b + 1, :]
                acc = acc + jnp.maximum(sb[:, QB:], 0.0) * iwt[2 * b + 1:2 * b + 2, :]
            k_pos = start + lax.broadcasted_iota(I32, (SUB, QB), 0)
            acc = jnp.where(k_pos <= q_pos, acc, -jnp.inf)
            key_ref[c, hf * SUB:(hf + 1) * SUB, :] = _sortable(acc)
        return carry

    lax.fori_loop(0, nck, score_chunk, 0)

    def count(pred):
        def chunk(c, parts):
            parts = list(parts)
            for r in range(KC // SUBLANES):
                k = key_ref[c, r * SUBLANES:(r + 1) * SUBLANES, :]
                hit = pred(k, c * KC + r * SUBLANES)
                parts[r % N_PART] = parts[r % N_PART] + jnp.where(hit, 1, 0)
            return tuple(parts)

        zero = jnp.zeros((SUBLANES, QB), I32)
        parts = lax.fori_loop(0, nck, chunk, (zero,) * N_PART)
        tot = (parts[0] + parts[1]) + (parts[2] + parts[3])
        return jnp.broadcast_to(jnp.sum(tot, axis=0, keepdims=True), (SUBLANES, QB))

    def search_pass(p, tu):
        cand_u = tu | lax.shift_left(jnp.int32(1), 31 - p)
        cand = cand_u ^ INT_MIN
        return jnp.where(count(lambda k, _: k >= cand) >= TOPK, cand_u, tu)

    tu = lax.fori_loop(0, 32, search_pass, jnp.zeros((SUBLANES, QB), I32))
    thr = tu ^ INT_MIN

    row_iota = lax.broadcasted_iota(I32, (SUBLANES, QB), 0)
    n_gt = count(lambda k, _: k > thr)
    n_ge = count(lambda k, _: k >= thr)
    need = TOPK - n_gt
    tied = (n_ge > TOPK) & (thr > KEY_NEG_INF)
    xlim_ref[...] = jnp.full((SUBLANES, QB), seq, I32)

    @pl.when(jnp.max(jnp.where(tied, 1, 0)) > 0)
    def _():
        def pos_pass(p, x):
            cand = x | lax.shift_left(jnp.int32(1), (seq.bit_length() - 2) - p)
            n = count(lambda k, r0: (k == thr) & (row_iota + r0 < cand))
            return jnp.where(n < need, cand, x)

        xlim_ref[...] = lax.fori_loop(0, seq.bit_length() - 1, pos_pass,
                                      jnp.zeros((SUBLANES, QB), I32))

    xlim = xlim_ref[...]

    def bias_chunk(c, carry):
        for r in range(KC // SUBLANES):
            k = key_ref[c, r * SUBLANES:(r + 1) * SUBLANES, :]
            pos = row_iota + (c * KC + r * SUBLANES)
            sel = (k > thr) | ((k == thr) & (pos <= xlim))
            sel = sel & (k > KEY_NEG_INF) & (k < KEY_POS_INF)
            bias_ref[c, r * SUBLANES:(r + 1) * SUBLANES, :] = jnp.where(sel, 0.0, NEG_BIG)
        return carry

    lax.fori_loop(0, nck, bias_chunk, 0)

    m_ref[...] = jnp.full(m_ref.shape, NEG_BIG, F32)
    l_ref[...] = jnp.zeros(l_ref.shape, F32)
    acc_ref[...] = jnp.zeros(acc_ref.shape, F32)
    group = N_HEADS // N_KV_HEADS

    def attn_chunk(c, carry):
        col_max = []
        for pr in range(N_PAIRS):
            g = (2 * pr) // group
            q2 = jnp.concatenate([qn_ref[:, (2 * pr) * HEAD_DIM:(2 * pr + 1) * HEAD_DIM],
                                  qn_ref[:, (2 * pr + 1) * HEAD_DIM:(2 * pr + 2) * HEAD_DIM]],
                                 axis=0)
            mx = None
            for hf in range(KC // SUB):
                start = pl.multiple_of(c * KC + hf * SUB, SUB)
                kc = kn_ref[pl.ds(start, SUB), g * HEAD_DIM:(g + 1) * HEAD_DIM]
                bias = bias_ref[c, hf * SUB:(hf + 1) * SUB, :]
                s = _dot_nt(kc, q2) + jnp.concatenate([bias, bias], axis=1)
                s_ref[pr, hf * SUB:(hf + 1) * SUB, :] = s
                cm = jnp.max(s, axis=0, keepdims=True)
                mx = cm if mx is None else jnp.maximum(mx, cm)
            col_max.append(mx)
        alphas = []
        for pr in range(N_PAIRS):
            m_prev = m_ref[pr]
            m_new = jnp.maximum(m_prev, col_max[pr])
            alpha = jnp.exp2(m_prev - m_new)
            p = jnp.exp2(s_ref[pr] - m_new[0:1, :])
            m_ref[pr] = m_new
            l_ref[pr] = alpha * l_ref[pr] + jnp.sum(p, axis=0, keepdims=True)
            p_ref[pr] = p.astype(BF16)
            alphas.append(alpha)
        for pr in range(N_PAIRS):
            g = (2 * pr) // group
            vt = vt_ref[c, g * HEAD_DIM:(g + 1) * HEAD_DIM, :]
            acc_ref[pr] = acc_ref[pr] * alphas[pr][0:1, :] + _dot(vt, p_ref[pr])
        return carry

    lax.fori_loop(0, nck, attn_chunk, 0)

    for pr in range(N_PAIRS):
        o_t = acc_ref[pr] / l_ref[pr][0:1, :]
        for i in range(2):
            h = 2 * pr + i
            o_ref[:, h * HEAD_DIM:(h + 1) * HEAD_DIM] = o_t[:, i * QB:(i + 1) * QB].T.astype(BF16)


def _dsa(qn, iqr, iwt, kn, vt, ika, ikb, batch, seq):
    t = qn.shape[0]
    nb = seq // QB
    nchunk = seq // KC
    qmap = lambda b, j: (b * nb + j, 0)
    bmap = lambda b, j: (b, 0)
    return pl.pallas_call(
        _dsa_kernel,
        grid=(batch, nb),
        in_specs=[
            pl.BlockSpec((QB, ATTN_DIM), qmap),
            pl.BlockSpec((QB, ATTN_DIM), qmap),
            pl.BlockSpec((IDX_HEADS, QB), lambda b, j: (0, b * nb + j)),
            pl.BlockSpec((seq, KV_DIM), bmap),
            pl.BlockSpec((nchunk, KV_DIM, KC), lambda b, j: (b, 0, 0)),
            pl.BlockSpec((seq, LANES), bmap),
            pl.BlockSpec((seq, LANES), bmap),
        ],
        out_specs=pl.BlockSpec((QB, ATTN_DIM), qmap),
        out_shape=jax.ShapeDtypeStruct((t, ATTN_DIM), BF16),
        scratch_shapes=[
            pltpu.VMEM((nchunk, KC, QB), I32),
            pltpu.VMEM((nchunk, KC, QB), F32),
            pltpu.VMEM((SUBLANES, QB), I32),
            pltpu.VMEM((N_PAIRS, SUBLANES, 2 * QB), F32),
            pltpu.VMEM((N_PAIRS, SUBLANES, 2 * QB), F32),
            pltpu.VMEM((N_PAIRS, HEAD_DIM, 2 * QB), F32),
            pltpu.VMEM((N_PAIRS, KC, 2 * QB), F32),
            pltpu.VMEM((N_PAIRS, KC, 2 * QB), BF16),
        ],
        compiler_params=_cparams(("parallel", "arbitrary")),
        name="dsa",
    )(qn, iqr, iwt, kn, vt, ika, ikb)


def _mem_kv_kernel(mem_ref, g_ref, w_ref, kg_ref, mk_ref, mv_ref):
    m = _rms(mem_ref[...], g_ref[...]).astype(BF16)
    kv = _dot(m, w_ref[...])
    for h in range(MEM_HEADS):
        sl = slice(h * HEAD_DIM, (h + 1) * HEAD_DIM)
        mk_ref[:, sl] = _rms(kv[:, sl], kg_ref[...]).astype(BF16)
    mv_ref[...] = kv[:, MEM_DIM:].astype(BF16)


def _mem_kv(mem2d, g, w, kg, batch):
    n = mem2d.shape[0]
    m = n // batch
    out = jax.ShapeDtypeStruct((n, MEM_DIM), BF16)
    ospec = pl.BlockSpec((m, MEM_DIM), lambda b: (b, 0))
    return pl.pallas_call(
        _mem_kv_kernel,
        grid=(batch,),
        in_specs=[
            pl.BlockSpec((m, D_MODEL), lambda b: (b, 0)),
            pl.BlockSpec((1, D_MODEL), lambda b: (0, 0)),
            pl.BlockSpec((D_MODEL, 2 * MEM_DIM), lambda b: (0, 0)),
            pl.BlockSpec((1, HEAD_DIM), lambda b: (0, 0)),
        ],
        out_specs=[ospec, ospec],
        out_shape=[out, out],
        compiler_params=_cparams(("parallel",)),
        name="mem_kv",
    )(mem2d, g, w, kg)


def _mem_attn_kernel(q_ref, mk_ref, mv_ref, qg_ref, o_ref):
    scale = HEAD_DIM ** -0.5
    for h in range(MEM_HEADS):
        sl = slice(h * HEAD_DIM, (h + 1) * HEAD_DIM)
        qh = (_rms(q_ref[:, sl].astype(F32), qg_ref[...]) * scale).astype(BF16)
        s = _dot_nt(qh, mk_ref[:, sl])
        p = jnp.exp(s - jnp.max(s, axis=1, keepdims=True))
        l = jnp.sum(p, axis=1, keepdims=True)
        o = _dot((p / l).astype(BF16), mv_ref[:, sl])
        o_ref[:, sl] = o.astype(BF16)


def _mem_attn(p, mk, mv, qg, batch, seq):
    t = p.shape[0]
    tq = 512
    nq = seq // tq
    m = mk.shape[0] // batch
    return pl.pallas_call(
        _mem_attn_kernel,
        grid=(batch, nq),
        in_specs=[
            pl.BlockSpec((tq, MEM_DIM), lambda b, i: (b * nq + i, P_MQ // MEM_DIM)),
            pl.BlockSpec((m, MEM_DIM), lambda b, i: (b, 0)),
            pl.BlockSpec((m, MEM_DIM), lambda b, i: (b, 0)),
            pl.BlockSpec((1, HEAD_DIM), lambda b, i: (0, 0)),
        ],
        out_specs=pl.BlockSpec((tq, MEM_DIM), lambda b, i: (b * nq + i, 0)),
        out_shape=jax.ShapeDtypeStruct((t, MEM_DIM), BF16),
        compiler_params=_cparams(("parallel", "parallel")),
        name="mem_attn",
    )(p, mk, mv, qg)


CONV_HALO = 32
CONV_ROWS = 64


def _conv_kernel(cur_ref, prev_ref, inb_ref, cw_ref, cb_ref, lg_ref, lb_ref, o_ref, u_ref):
    ts = cur_ref.shape[0]

    def glu(x):
        x = x.astype(F32) + inb_ref[...]
        return x[:, :CONV_CH] * jax.nn.sigmoid(x[:, CONV_CH:])

    u_prev = glu(prev_ref[ts - CONV_HALO:, :])
    u_ref[:CONV_HALO, :] = jnp.where(pl.program_id(1) == 0, 0.0, u_prev)
    u_ref[CONV_HALO:, :] = glu(cur_ref[...])

    first_tap = CONV_HALO - (CONV_WIDTH - 1)
    for r in range(ts // CONV_ROWS):
        base = r * CONV_ROWS
        y = jnp.broadcast_to(cb_ref[...], (CONV_ROWS, CONV_CH))
        for w in range(CONV_WIDTH):
            y = y + u_ref[base + first_tap + w:base + first_tap + w + CONV_ROWS, :] * cw_ref[w:w + 1, :]
        mu = jnp.mean(y, axis=-1, keepdims=True)
        d = y - mu
        var = jnp.mean(d * d, axis=-1, keepdims=True)
        z = d * lax.rsqrt(var + EPS) * lg_ref[...] + lb_ref[...]
        o_ref[base:base + CONV_ROWS, :] = (z * jax.nn.sigmoid(z)).astype(BF16)


def _conv(p, inb, cw, cb, lg, lb, batch, seq):
    t = p.shape[0]
    ts = 512
    ns = seq // ts
    glu_blk = P_GLU // (2 * CONV_CH)
    vec = lambda w: pl.BlockSpec((1, w), lambda b, i: (0, 0))
    return pl.pallas_call(
        _conv_kernel,
        grid=(batch, ns),
        in_specs=[
            pl.BlockSpec((ts, 2 * CONV_CH), lambda b, i: (b * ns + i, glu_blk)),
            pl.BlockSpec((ts, 2 * CONV_CH), lambda b, i: (b * ns + jnp.maximum(i - 1, 0), glu_blk)),
            vec(2 * CONV_CH),
            pl.BlockSpec((CONV_WIDTH, CONV_CH), lambda b, i: (0, 0)),
            vec(CONV_CH), vec(CONV_CH), vec(CONV_CH),
        ],
        out_specs=pl.BlockSpec((ts, CONV_CH), lambda b, i: (b * ns + i, 0)),
        out_shape=jax.ShapeDtypeStruct((t, CONV_CH), BF16),
        scratch_shapes=[pltpu.VMEM((CONV_HALO + ts, CONV_CH), F32)],
        compiler_params=_cparams(("parallel", "parallel")),
        name="conv",
    )(p, p, inb, cw, cb, lg, lb)


MERGE_TN = 512


def _merge_kernel(a_ref, m_ref, c_ref, g0_ref, g1_ref, g2_ref, gb0_ref, gb1_ref, gb2_ref,
                  wa_ref, wm_ref, wc_ref, cob_ref, o_ref):
    a, m, c = a_ref[...], m_ref[...], c_ref[...]
    for n in range(D_MODEL // MERGE_TN):
        sl = slice(n * MERGE_TN, (n + 1) * MERGE_TN)

        def gate(g_ref, gb_ref):
            return jax.nn.sigmoid(g_ref[:, sl].astype(F32) + gb_ref[:, sl])

        y = gate(g0_ref, gb0_ref) * _dot(a, wa_ref[:, sl])
        y = y + gate(g1_ref, gb1_ref) * _dot(m, wm_ref[:, sl])
        y = y + gate(g2_ref, gb2_ref) * (_dot(c, wc_ref[:, sl]) + cob_ref[:, sl])
        o_ref[:, sl] = y.astype(BF16)


def _merge(attn, memo, convo, p, gate_b, wa, wm, wc, cob):
    t = attn.shape[0]
    tm = 512
    gblk = P_GATES // D_MODEL
    row = lambda w: pl.BlockSpec((tm, w), lambda i: (i, 0))
    full = lambda r, c: pl.BlockSpec((r, c), lambda i: (0, 0))
    gspec = lambda k: pl.BlockSpec((tm, D_MODEL), lambda i: (i, gblk + k))
    gbspec = lambda k: pl.BlockSpec((1, D_MODEL), lambda i: (0, k))
    return pl.pallas_call(
        _merge_kernel,
        grid=(t // tm,),
        in_specs=[row(ATTN_DIM), row(MEM_DIM), row(CONV_CH),
                  gspec(0), gspec(1), gspec(2), gbspec(0), gbspec(1), gbspec(2),
                  full(ATTN_DIM, D_MODEL), full(MEM_DIM, D_MODEL), full(CONV_CH, D_MODEL),
                  full(1, D_MODEL)],
        out_specs=row(D_MODEL),
        out_shape=jax.ShapeDtypeStruct((t, D_MODEL), BF16),
        compiler_params=_cparams(("parallel",)),
        name="merge",
    )(attn, memo, convo, p, p, p, gate_b, gate_b, gate_b, wa, wm, wc, cob)


def _outproj_kernel(x_ref, mg_ref, w_ref, g_ref, x1_ref, h2_ref):
    x1 = x_ref[...] + _dot(mg_ref[...], w_ref[...])
    x1_ref[...] = x1
    h2_ref[...] = _rms(x1, g_ref[...]).astype(BF16)


def _outproj(x, merged, w, g):
    t = x.shape[0]
    tm = 512
    row = pl.BlockSpec((tm, D_MODEL), lambda i: (i, 0))
    return pl.pallas_call(
        _outproj_kernel,
        grid=(t // tm,),
        in_specs=[row, row,
                  pl.BlockSpec((D_MODEL, D_MODEL), lambda i: (0, 0)),
                  pl.BlockSpec((1, D_MODEL), lambda i: (0, 0))],
        out_specs=[row, row],
        out_shape=[jax.ShapeDtypeStruct((t, D_MODEL), F32), jax.ShapeDtypeStruct((t, D_MODEL), BF16)],
        compiler_params=_cparams(("parallel",)),
        name="outproj",
    )(x, merged, w, g)


def _mlp_kernel(h_ref, x_ref, wu_ref, wd_ref, o_ref):
    @pl.when(pl.program_id(1) == 0)
    def _():
        o_ref[...] = x_ref[...]

    u = jnp.maximum(_dot(h_ref[...], wu_ref[...]), 0.0)
    o_ref[...] += _dot((u * u).astype(BF16), wd_ref[...])


def _mlp(h2, x1, wu, wd):
    t = x1.shape[0]
    tm, tf = 512, 1024
    return pl.pallas_call(
        _mlp_kernel,
        grid=(t // tm, FFN_DIM // tf),
        in_specs=[
            pl.BlockSpec((tm, D_MODEL), lambda i, f: (i, 0)),
            pl.BlockSpec((tm, D_MODEL), lambda i, f: (i, 0)),
            pl.BlockSpec((D_MODEL, tf), lambda i, f: (0, f)),
            pl.BlockSpec((tf, D_MODEL), lambda i, f: (f, 0)),
        ],
        out_specs=pl.BlockSpec((tm, D_MODEL), lambda i, f: (i, 0)),
        out_shape=jax.ShapeDtypeStruct((t, D_MODEL), F32),
        compiler_params=_cparams(("parallel", "arbitrary")),
        name="mlp",
    )(h2, x1, wu, wd)


def _regroup_w_in(w):
    o_k = ATTN_DIM
    o_v = o_k + KV_DIM
    o_iq = o_v + KV_DIM
    o_ik = o_iq + IDX_HEADS * IDX_DIM
    o_iw = o_ik + IDX_DIM
    o_glu = o_iw + IDX_HEADS
    o_mq = o_glu + 2 * CONV_CH
    o_g = o_mq + MEM_DIM
    main = jnp.concatenate([w[:, :o_k], w[:, o_iq:o_ik], w[:, o_glu:o_mq], w[:, o_mq:o_g],
                            w[:, o_k:o_v], w[:, o_v:o_iq], w[:, o_g:]], axis=1).astype(BF16)
    small = jnp.pad(w[:, o_ik:o_glu], ((0, 0), (0, PS_WIDTH - IDX_DIM - IDX_HEADS))).astype(BF16)
    return main, small


def kernel(x, mem, positions, norm1_g, w_in, q_norm_g, k_norm_g, mem_norm_g, w_mem_kv, mq_norm_g,
           mk_norm_g, conv_in_b, conv_w, conv_b, conv_ln_g, conv_ln_b, gate_b, w_attn_o, w_mem_o,
           w_conv_o, conv_o_b, w_out, norm2_g, w_up, w_down):
    batch, seq, _ = x.shape
    depth = w_in.shape[0]
    t = batch * seq
    xf = x.reshape(t, D_MODEL)
    mem2d = mem.reshape(batch * mem.shape[1], D_MODEL)
    tabs = _rope_tables(positions)
    row = lambda v: v.reshape(1, -1)

    for l in range(depth):
        w_main, w_small = _regroup_w_in(w_in[l])
        p, ps = _proj(xf, row(norm1_g[l]), w_main, w_small)
        qn, iqr, kn, vt, ika, ikb, iwt = _prep(p, ps, tabs, row(q_norm_g[l]), row(k_norm_g[l]))
        attn = _dsa(qn, iqr, iwt, kn, vt, ika, ikb, batch, seq)
        mk, mv = _mem_kv(mem2d, row(mem_norm_g[l]), w_mem_kv[l].astype(BF16), row(mk_norm_g[l]), batch)
        memo = _mem_attn(p, mk, mv, row(mq_norm_g[l]), batch, seq)
        convo = _conv(p, row(conv_in_b[l]), conv_w[l], row(conv_b[l]), row(conv_ln_g[l]),
                      row(conv_ln_b[l]), batch, seq)
        merged = _merge(attn, memo, convo, p, row(gate_b[l]), w_attn_o[l].astype(BF16),
                        w_mem_o[l].astype(BF16), w_conv_o[l].astype(BF16), row(conv_o_b[l]))
        x1, h2 = _outproj(xf, merged, w_out[l].astype(BF16), row(norm2_g[l]))
        xf = _mlp(h2, x1, w_up[l].astype(BF16), w_down[l].astype(BF16))
    return xf.reshape(batch, seq, D_MODEL)
```

```python
import functools
import math

import jax
import jax.numpy as jnp
from jax import lax
from jax.experimental import pallas as pl
from jax.experimental.pallas import tpu as pltpu

F32 = jnp.float32
BF16 = jnp.bfloat16
I32 = jnp.int32
I16 = jnp.int16

D_MODEL = 2048
HEAD_DIM = 128
N_HEADS = 8
N_KV_HEADS = 2
IDX_HEADS = 16
IDX_DIM = 64
TOPK = 256
CONV_CH = 512
CONV_WIDTH = 31
MEM_HEADS = 4
FFN_DIM = 4 * D_MODEL
ROPE_THETA = 500000.0
N_BRANCH = 3
EPS = 1e-6
ATTN_DIM = N_HEADS * HEAD_DIM
KV_DIM = N_KV_HEADS * HEAD_DIM
MEM_DIM = MEM_HEADS * HEAD_DIM
HEAD_ROT_HALF = HEAD_DIM // 8
IDX_ROT_HALF = IDX_DIM // 8

LANES = 128
VMEM_LIMIT = 56 * 1024 * 1024

P_Q, P_IQ, P_GLU, P_MQ, P_K, P_V, P_GATES = 0, 1024, 2048, 3072, 3584, 3840, 4096
P_WIDTH = P_GATES + N_BRANCH * D_MODEL
PS_WIDTH = LANES

QB = 128
KC = 512
SUB = 256
SLAB = 32
SUBLANES = 8
PACK16 = 16
LOG2E = 1.4426950408889634
NEG_BIG = -1e30
INT_MIN = -2 ** 31
KEY_NEG_INF = -2139095041
KEY_POS_INF = 2139095040


def _cparams(sem, vmem=VMEM_LIMIT):
    return pltpu.CompilerParams(dimension_semantics=sem, vmem_limit_bytes=vmem)


def _dot(a, b):
    return jnp.dot(a, b, preferred_element_type=F32)


def _dot_nt(a, b):
    return lax.dot_general(a, b, (((1,), (1,)), ((), ())), preferred_element_type=F32)


def _rms(xf, g):
    return xf * lax.rsqrt(jnp.mean(xf * xf, axis=-1, keepdims=True) + EPS) * g


def _rope_table_kernel(pos_ref, ch_ref, sh1_ref, sh2_ref, ci_ref, si1_ref, si2_ref):
    pos = pos_ref[...].astype(F32)
    lane = lax.broadcasted_iota(I32, (1, LANES), 1)

    def tables(period, half, c_ref, s1_ref, s2_ref):
        r = lane & (period - 1)
        fi = (r & (half - 1)).astype(F32)
        inv = jnp.exp(fi * (-math.log(ROPE_THETA) / half))
        ang = pos * inv
        c = jnp.cos(ang)
        s = jnp.sin(ang)
        first = r < half
        second = (r >= half) & (r < 2 * half)
        c_ref[...] = jnp.where(first | second, c, 1.0)
        s1_ref[...] = jnp.where(second, s, 0.0)
        s2_ref[...] = jnp.where(first, -s, 0.0)

    tables(HEAD_DIM, HEAD_ROT_HALF, ch_ref, sh1_ref, sh2_ref)
    tables(IDX_DIM, IDX_ROT_HALF, ci_ref, si1_ref, si2_ref)


def _rope_tables(positions):
    t = positions.size
    tp = 512
    tab = jax.ShapeDtypeStruct((t, LANES), F32)
    spec = pl.BlockSpec((tp, LANES), lambda i: (i, 0))
    return pl.pallas_call(
        _rope_table_kernel,
        grid=(t // tp,),
        in_specs=[pl.BlockSpec((tp, 1), lambda i: (i, 0))],
        out_specs=[spec] * 6,
        out_shape=[tab] * 6,
        compiler_params=_cparams(("parallel",)),
        name="rope_tables",
    )(positions.reshape(t, 1))


def _rope(t, c, s1, s2, half):
    return t * c + pltpu.roll(t, half, 1) * s1 + pltpu.roll(t, LANES - half, 1) * s2


def _proj_kernel(x_ref, g_ref, w_ref, ws_ref, o_ref, os_ref, h_ref):
    @pl.when(pl.program_id(1) == 0)
    def _():
        h = _rms(x_ref[...], g_ref[...]).astype(BF16)
        h_ref[...] = h
        os_ref[...] = _dot(h, ws_ref[...])

    o_ref[...] = _dot(h_ref[...], w_ref[...]).astype(o_ref.dtype)


def _proj(x, g, w, ws):
    t = x.shape[0]
    tm, tn = 1024, 1024
    return pl.pallas_call(
        _proj_kernel,
        grid=(t // tm, P_WIDTH // tn),
        in_specs=[
            pl.BlockSpec((tm, D_MODEL), lambda i, j: (i, 0)),
            pl.BlockSpec((1, D_MODEL), lambda i, j: (0, 0)),
            pl.BlockSpec((D_MODEL, tn), lambda i, j: (0, j)),
            pl.BlockSpec((D_MODEL, PS_WIDTH), lambda i, j: (0, 0)),
        ],
        out_specs=[
            pl.BlockSpec((tm, tn), lambda i, j: (i, j)),
            pl.BlockSpec((tm, PS_WIDTH), lambda i, j: (i, 0)),
        ],
        out_shape=[
            jax.ShapeDtypeStruct((t, P_WIDTH), BF16),
            jax.ShapeDtypeStruct((t, PS_WIDTH), F32),
        ],
        scratch_shapes=[pltpu.VMEM((tm, D_MODEL), BF16)],
        compiler_params=_cparams(("parallel", "arbitrary")),
        name="proj",
    )(x, g, w, ws)


def _prep_kernel(q_ref, iq_ref, kv_ref, ps_ref, ch_ref, sh1_ref, sh2_ref, ci_ref, si1_ref,
                 si2_ref, qg_ref, kg_ref, qn_ref, iqr_ref, kn_ref, vt_ref, ika_ref, ikb_ref,
                 iwt_ref):
    ch, sh1, sh2 = ch_ref[...], sh1_ref[...], sh2_ref[...]
    ci, si1, si2 = ci_ref[...], si1_ref[...], si2_ref[...]
    scale = (HEAD_DIM ** -0.5) * LOG2E
    for h in range(N_HEADS):
        sl = slice(h * HEAD_DIM, (h + 1) * HEAD_DIM)
        qh = _rms(q_ref[:, sl].astype(F32), qg_ref[...])
        qn_ref[:, sl] = (_rope(qh, ch, sh1, sh2, HEAD_ROT_HALF) * scale).astype(BF16)
        iqh = iq_ref[:, sl].astype(F32)
        iqr_ref[:, sl] = _rope(iqh, ci, si1, si2, IDX_ROT_HALF).astype(BF16)
    for g in range(N_KV_HEADS):
        sl = slice(g * HEAD_DIM, (g + 1) * HEAD_DIM)
        kh = _rms(kv_ref[:, sl].astype(F32), kg_ref[...])
        kn_ref[:, sl] = _rope(kh, ch, sh1, sh2, HEAD_ROT_HALF).astype(BF16)
        vh = kv_ref[:, KV_DIM + g * HEAD_DIM:KV_DIM + (g + 1) * HEAD_DIM].astype(F32)
        vt_ref[0, sl, :] = vh.T.astype(BF16)
    ps = ps_ref[...]
    lane = lax.broadcasted_iota(I32, (1, LANES), 1)
    ikr = jnp.where(lane < IDX_DIM, _rope(ps, ci, si1, si2, IDX_ROT_HALF), 0.0)
    ika_ref[...] = ikr.astype(BF16)
    ikb_ref[...] = pltpu.roll(ikr, IDX_DIM, 1).astype(BF16)
    idx_scale = (IDX_DIM ** -0.5) * (IDX_HEADS ** -0.5)
    iws = jnp.where(lane < IDX_HEADS, pltpu.roll(ps, LANES - IDX_DIM, 1) * idx_scale, 0.0)
    iwt_ref[...] = iws.T


def _prep(p, ps, tabs, qg, kg):
    t = p.shape[0]
    tp = KC
    tab_spec = pl.BlockSpec((tp, LANES), lambda i: (i, 0))
    vec_spec = pl.BlockSpec((1, HEAD_DIM), lambda i: (0, 0))

    def out(width, dtype):
        return (pl.BlockSpec((tp, width), lambda i: (i, 0)), jax.ShapeDtypeStruct((t, width), dtype))

    outs = [out(ATTN_DIM, BF16), out(ATTN_DIM, BF16), out(KV_DIM, BF16),
            (pl.BlockSpec((1, KV_DIM, tp), lambda i: (i, 0, 0)),
             jax.ShapeDtypeStruct((t // tp, KV_DIM, tp), BF16)),
            out(LANES, BF16), out(LANES, BF16),
            (pl.BlockSpec((LANES, tp), lambda i: (0, i)), jax.ShapeDtypeStruct((LANES, t), F32))]
    return pl.pallas_call(
        _prep_kernel,
        grid=(t // tp,),
        in_specs=[
            pl.BlockSpec((tp, ATTN_DIM), lambda i: (i, P_Q // ATTN_DIM)),
            pl.BlockSpec((tp, ATTN_DIM), lambda i: (i, P_IQ // ATTN_DIM)),
            pl.BlockSpec((tp, 2 * KV_DIM), lambda i: (i, P_K // (2 * KV_DIM))),
            tab_spec,
        ] + [tab_spec] * 6 + [vec_spec, vec_spec],
        out_specs=[o[0] for o in outs],
        out_shape=[o[1] for o in outs],
        compiler_params=_cparams(("parallel",)),
        name="prep",
    )(p, p, p, ps, *tabs, qg, kg)


def _sortable(x):
    bits = lax.bitcast_convert_type(x, I32)
    return bits ^ ((bits >> 31) & 0x7FFFFFFF)


N_PAIRS = N_HEADS // 2
N_PART = 4


def _dsa_kernel(qn_ref, iq_ref, iwt_ref, kn_ref, vt_ref, ika_ref, ikb_ref, o_ref,
                key_ref, hi_ref, lo_ref, bias_ref, m_ref, l_ref, acc_ref, s_ref, p_ref):
    j = pl.program_id(1)
    nck = j // (KC // QB) + 1
    seq = kn_ref.shape[0]

    iwt = iwt_ref[...]
    q_pos = j * QB + lax.broadcasted_iota(I32, (SUB, QB), 1)
    n_grp = IDX_HEADS // 2

    def score_chunk(c, carry):
        for hf in range(KC // SUB):
            start = pl.multiple_of(c * KC + hf * SUB, SUB)
            ka = ika_ref[pl.ds(start, SUB), :]
            kb = ikb_ref[pl.ds(start, SUB), :]
            acc = jnp.zeros((SUB, QB), F32)
            for a in range(n_grp // 2):
                b = a + n_grp // 2
                iq2 = jnp.concatenate([iq_ref[:, a * LANES:(a + 1) * LANES],
                                       iq_ref[:, b * LANES:(b + 1) * LANES]], axis=0)
                sa = _dot_nt(ka, iq2)
                sb = _dot_nt(kb, iq2)
                acc = acc + jnp.maximum(sa[:, :QB], 0.0) * iwt[2 * a:2 * a + 1, :]
                acc = acc + jnp.maximum(sb[:, :QB], 0.0) * iwt[2 * a + 1:2 * a + 2, :]
                acc = acc + jnp.maximum(sa[:, QB:], 0.0) * iwt[2 * b:2 * b + 1, :]
                acc = acc + jnp.maximum(sb[:, QB:], 0.0) * iwt[2 * b + 1:2 * b + 2, :]
            k_pos = start + lax.broadcasted_iota(I32, (SUB, QB), 0)
            acc = jnp.where(k_pos <= q_pos, acc, -jnp.inf)
            key = _sortable(acc)
            key_ref[c, hf * SUB:(hf + 1) * SUB, :] = key
            hi_ref[c, hf * SUB:(hf + 1) * SUB, :] = (key >> 16).astype(I16)
        return carry

    lax.fori_loop(0, nck, score_chunk, 0)

    rep16 = lambda v: jnp.concatenate([v, v], axis=0).astype(I16)

    def count16(ref, cand16):
        one = jnp.ones((PACK16, QB), I16)
        zero = jnp.zeros((PACK16, QB), I16)

        def chunk(c, parts):
            parts = list(parts)
            for r in range(KC // PACK16):
                k = ref[c, r * PACK16:(r + 1) * PACK16, :]
                parts[r % N_PART] = parts[r % N_PART] + jnp.where(k >= cand16, one, zero)
            return tuple(parts)

        parts = lax.fori_loop(0, nck, chunk, (zero,) * N_PART)
        tot = ((parts[0] + parts[1]) + (parts[2] + parts[3])).astype(I32)
        return jnp.broadcast_to(jnp.sum(tot, axis=0, keepdims=True), (SUBLANES, QB))

    def search16(ref, base, n_all):
        def search_pass(p, carry):
            tu, n_at = carry
            cand_u = tu | lax.shift_left(jnp.int32(1), 15 - p)
            n = base + count16(ref, rep16(cand_u ^ 0x8000))
            ok = n >= TOPK
            return jnp.where(ok, cand_u, tu), jnp.where(ok, n, n_at)

        return lax.fori_loop(0, 16, search_pass, (jnp.zeros((SUBLANES, QB), I32), n_all))

    n_loaded = jnp.broadcast_to(nck * KC, (SUBLANES, QB)).astype(I32)
    thi_u, n_ge = search16(hi_ref, jnp.zeros((SUBLANES, QB), I32), n_loaded)
    thi16 = rep16(thi_u ^ 0x8000)
    n_hi_gt = jnp.where(thi_u == 0xFFFF, 0,
                        count16(hi_ref, rep16(jnp.minimum(thi_u + 1, 0xFFFF) ^ 0x8000)))

    def low_chunk(c, carry):
        for r in range(KC // PACK16):
            sl = slice(r * PACK16, (r + 1) * PACK16)
            lo = ((key_ref[c, sl, :] & 0xFFFF) ^ 0x8000).astype(I16)
            lo_ref[c, sl, :] = jnp.where(hi_ref[c, sl, :] == thi16, lo, jnp.int16(-0x8000))
        return carry

    lax.fori_loop(0, nck, low_chunk, 0)
    tlo_u, n_ge = search16(lo_ref, n_hi_gt, n_ge)
    thr = (lax.shift_left(thi_u, 16) | tlo_u) ^ INT_MIN

    row_iota = lax.broadcasted_iota(I32, (SUBLANES, QB), 0)
    tied = (n_ge > TOPK) & (thr > KEY_NEG_INF)
    any_tied = jnp.max(jnp.where(tied, 1, 0)) > 0

    def count32(pred):
        def chunk(c, parts):
            parts = list(parts)
            for r in range(KC // SUBLANES):
                k = key_ref[c, r * SUBLANES:(r + 1) * SUBLANES, :]
                hit = pred(k, c * KC + r * SUBLANES)
                parts[r % N_PART] = parts[r % N_PART] + jnp.where(hit, 1, 0)
            return tuple(parts)

        zero = jnp.zeros((SUBLANES, QB), I32)
        parts = lax.fori_loop(0, nck, chunk, (zero,) * N_PART)
        tot = (parts[0] + parts[1]) + (parts[2] + parts[3])
        return jnp.broadcast_to(jnp.sum(tot, axis=0, keepdims=True), (SUBLANES, QB))

    thr_fin = jnp.maximum(thr, KEY_NEG_INF + 1)

    @pl.when(jnp.logical_not(any_tied))
    def _():
        def bias_chunk(c, carry):
            for r in range(KC // SUBLANES):
                sl = slice(r * SUBLANES, (r + 1) * SUBLANES)
                k = key_ref[c, sl, :]
                sel = (k >= thr_fin) & (k < KEY_POS_INF)
                bias_ref[c, sl, :] = jnp.where(sel, 0.0, NEG_BIG)
            return carry

        lax.fori_loop(0, nck, bias_chunk, 0)

    @pl.when(any_tied)
    def _():
        need = TOPK - count32(lambda k, _: k > thr)

        def pos_pass(p, x):
            cand = x | lax.shift_left(jnp.int32(1), (seq.bit_length() - 2) - p)
            n = count32(lambda k, r0: (k == thr) & (row_iota + r0 < cand))
            return jnp.where(n < need, cand, x)

        xlim = lax.fori_loop(0, seq.bit_length() - 1, pos_pass, jnp.zeros((SUBLANES, QB), I32))

        def bias_chunk(c, carry):
            for r in range(KC // SUBLANES):
                sl = slice(r * SUBLANES, (r + 1) * SUBLANES)
                k = key_ref[c, sl, :]
                pos = row_iota + (c * KC + r * SUBLANES)
                sel = (k > thr) | ((k == thr) & (pos <= xlim))
                sel = sel & (k >= thr_fin) & (k < KEY_POS_INF)
                bias_ref[c, sl, :] = jnp.where(sel, 0.0, NEG_BIG)
            return carry

        lax.fori_loop(0, nck, bias_chunk, 0)

    m_ref[...] = jnp.full(m_ref.shape, NEG_BIG, F32)
    l_ref[...] = jnp.zeros(l_ref.shape, F32)
    acc_ref[...] = jnp.zeros(acc_ref.shape, F32)
    group = N_HEADS // N_KV_HEADS

    def attn_chunk(c, carry):
        col_max = []
        for pr in range(N_PAIRS):
            g = (2 * pr) // group
            q2 = jnp.concatenate([qn_ref[:, (2 * pr) * HEAD_DIM:(2 * pr + 1) * HEAD_DIM],
                                  qn_ref[:, (2 * pr + 1) * HEAD_DIM:(2 * pr + 2) * HEAD_DIM]],
                                 axis=0)
            mx = jnp.full((SUBLANES, 2 * QB), NEG_BIG, F32)
            for hf in range(KC // SUB):
                start = pl.multiple_of(c * KC + hf * SUB, SUB)
                kc = kn_ref[pl.ds(start, SUB), g * HEAD_DIM:(g + 1) * HEAD_DIM]
                bias = bias_ref[c, hf * SUB:(hf + 1) * SUB, :]
                s = _dot_nt(kc, q2) + jnp.concatenate([bias, bias], axis=1)
                s_ref[pr, hf * SUB:(hf + 1) * SUB, :] = s
                for r in range(SUB // SUBLANES):
                    mx = jnp.maximum(mx, s[r * SUBLANES:(r + 1) * SUBLANES, :])
            col_max.append(jnp.max(mx, axis=0, keepdims=True))
        alphas = []
        for pr in range(N_PAIRS):
            m_prev = m_ref[pr]
            m_new = jnp.maximum(m_prev, col_max[pr])
            alpha = jnp.exp2(m_prev - m_new)
            lsum = jnp.zeros((SUBLANES, 2 * QB), F32)
            for t in range(KC // SLAB):
                rows = slice(t * SLAB, (t + 1) * SLAB)
                p = jnp.exp2(s_ref[pr, rows, :] - m_new[0:1, :])
                for r in range(SLAB // SUBLANES):
                    lsum = lsum + p[r * SUBLANES:(r + 1) * SUBLANES, :]
                p_ref[pr, rows, :] = p.astype(BF16)
            m_ref[pr] = m_new
            l_ref[pr] = alpha * l_ref[pr] + jnp.sum(lsum, axis=0, keepdims=True)
            alphas.append(alpha)
        for pr in range(N_PAIRS):
            g = (2 * pr) // group
            vt = vt_ref[c, g * HEAD_DIM:(g + 1) * HEAD_DIM, :]
            acc_ref[pr] = acc_ref[pr] * alphas[pr][0:1, :] + _dot(vt, p_ref[pr])
        return carry

    lax.fori_loop(0, nck, attn_chunk, 0)

    for pr in range(N_PAIRS):
        o_t = acc_ref[pr] / l_ref[pr][0:1, :]
        for i in range(2):
            h = 2 * pr + i
            o_ref[:, h * HEAD_DIM:(h + 1) * HEAD_DIM] = o_t[:, i * QB:(i + 1) * QB].T.astype(BF16)


def _dsa(qn, iqr, iwt, kn, vt, ika, ikb, batch, seq):
    t = qn.shape[0]
    nb = seq // QB
    nchunk = seq // KC
    qmap = lambda b, j: (b * nb + j, 0)
    bmap = lambda b, j: (b, 0)
    return pl.pallas_call(
        _dsa_kernel,
        grid=(batch, nb),
        in_specs=[
            pl.BlockSpec((QB, ATTN_DIM), qmap),
            pl.BlockSpec((QB, ATTN_DIM), qmap),
            pl.BlockSpec((IDX_HEADS, QB), lambda b, j: (0, b * nb + j)),
            pl.BlockSpec((seq, KV_DIM), bmap),
            pl.BlockSpec((nchunk, KV_DIM, KC), lambda b, j: (b, 0, 0)),
            pl.BlockSpec((seq, LANES), bmap),
            pl.BlockSpec((seq, LANES), bmap),
        ],
        out_specs=pl.BlockSpec((QB, ATTN_DIM), qmap),
        out_shape=jax.ShapeDtypeStruct((t, ATTN_DIM), BF16),
        scratch_shapes=[
            pltpu.VMEM((nchunk, KC, QB), I32),
            pltpu.VMEM((nchunk, KC, QB), I16),
            pltpu.VMEM((nchunk, KC, QB), I16),
            pltpu.VMEM((nchunk, KC, QB), F32),
            pltpu.VMEM((N_PAIRS, SUBLANES, 2 * QB), F32),
            pltpu.VMEM((N_PAIRS, SUBLANES, 2 * QB), F32),
            pltpu.VMEM((N_PAIRS, HEAD_DIM, 2 * QB), F32),
            pltpu.VMEM((N_PAIRS, KC, 2 * QB), F32),
            pltpu.VMEM((N_PAIRS, KC, 2 * QB), BF16),
        ],
        compiler_params=_cparams(("parallel", "arbitrary")),
        name="dsa",
    )(qn, iqr, iwt, kn, vt, ika, ikb)


def _mem_kv_kernel(mem_ref, g_ref, w_ref, kg_ref, mk_ref, mv_ref):
    m = _rms(mem_ref[...], g_ref[...]).astype(BF16)
    kv = _dot(m, w_ref[...])
    for h in range(MEM_HEADS):
        sl = slice(h * HEAD_DIM, (h + 1) * HEAD_DIM)
        mk_ref[:, sl] = _rms(kv[:, sl], kg_ref[...]).astype(BF16)
    mv_ref[...] = kv[:, MEM_DIM:].astype(BF16)


def _mem_kv(mem2d, g, w, kg, batch):
    n = mem2d.shape[0]
    m = n // batch
    out = jax.ShapeDtypeStruct((n, MEM_DIM), BF16)
    ospec = pl.BlockSpec((m, MEM_DIM), lambda b: (b, 0))
    return pl.pallas_call(
        _mem_kv_kernel,
        grid=(batch,),
        in_specs=[
            pl.BlockSpec((m, D_MODEL), lambda b: (b, 0)),
            pl.BlockSpec((1, D_MODEL), lambda b: (0, 0)),
            pl.BlockSpec((D_MODEL, 2 * MEM_DIM), lambda b: (0, 0)),
            pl.BlockSpec((1, HEAD_DIM), lambda b: (0, 0)),
        ],
        out_specs=[ospec, ospec],
        out_shape=[out, out],
        compiler_params=_cparams(("parallel",)),
        name="mem_kv",
    )(mem2d, g, w, kg)


def _mem_attn_kernel(q_ref, mk_ref, mv_ref, qg_ref, o_ref):
    scale = HEAD_DIM ** -0.5
    for h in range(MEM_HEADS):
        sl = slice(h * HEAD_DIM, (h + 1) * HEAD_DIM)
        qh = (_rms(q_ref[:, sl].astype(F32), qg_ref[...]) * scale).astype(BF16)
        s = _dot_nt(qh, mk_ref[:, sl])
        p = jnp.exp(s - jnp.max(s, axis=1, keepdims=True))
        l = jnp.sum(p, axis=1, keepdims=True)
        o = _dot((p / l).astype(BF16), mv_ref[:, sl])
        o_ref[:, sl] = o.astype(BF16)


def _mem_attn(p, mk, mv, qg, batch, seq):
    t = p.shape[0]
    tq = 512
    nq = seq // tq
    m = mk.shape[0] // batch
    return pl.pallas_call(
        _mem_attn_kernel,
        grid=(batch, nq),
        in_specs=[
            pl.BlockSpec((tq, MEM_DIM), lambda b, i: (b * nq + i, P_MQ // MEM_DIM)),
            pl.BlockSpec((m, MEM_DIM), lambda b, i: (b, 0)),
            pl.BlockSpec((m, MEM_DIM), lambda b, i: (b, 0)),
            pl.BlockSpec((1, HEAD_DIM), lambda b, i: (0, 0)),
        ],
        out_specs=pl.BlockSpec((tq, MEM_DIM), lambda b, i: (b * nq + i, 0)),
        out_shape=jax.ShapeDtypeStruct((t, MEM_DIM), BF16),
        compiler_params=_cparams(("parallel", "parallel")),
        name="mem_attn",
    )(p, mk, mv, qg)


CONV_HALO = 32
CONV_ROWS = 64


def _conv_kernel(cur_ref, prev_ref, inb_ref, cw_ref, cb_ref, lg_ref, lb_ref, o_ref, u_ref, sh_ref):
    ts = cur_ref.shape[0]
    n = CONV_HALO + ts

    def glu(x):
        x = x.astype(F32) + inb_ref[...]
        return x[:, :CONV_CH] * jax.nn.sigmoid(x[:, CONV_CH:])

    u_prev = glu(prev_ref[ts - CONV_HALO:, :])
    u_ref[:CONV_HALO, :] = jnp.where(pl.program_id(1) == 0, 0.0, u_prev)
    u_ref[CONV_HALO:n, :] = glu(cur_ref[...])
    u_ref[n:, :] = jnp.zeros((SUBLANES, CONV_CH), F32)
    for r in range(1, SUBLANES):
        sh_ref[r - 1] = u_ref[r:r + n, :]

    first_tap = CONV_HALO - (CONV_WIDTH - 1)
    for t in range(ts // CONV_ROWS):
        base = t * CONV_ROWS
        y = jnp.broadcast_to(cb_ref[...], (CONV_ROWS, CONV_CH))
        for w in range(CONV_WIDTH):
            r = (first_tap + w) % SUBLANES
            a = base + first_tap + w - r
            rows = u_ref[a:a + CONV_ROWS, :] if r == 0 else sh_ref[r - 1, a:a + CONV_ROWS, :]
            y = y + rows * cw_ref[w:w + 1, :]
        mu = jnp.mean(y, axis=-1, keepdims=True)
        d = y - mu
        var = jnp.mean(d * d, axis=-1, keepdims=True)
        z = d * lax.rsqrt(var + EPS) * lg_ref[...] + lb_ref[...]
        o_ref[base:base + CONV_ROWS, :] = (z * jax.nn.sigmoid(z)).astype(BF16)


def _conv(p, inb, cw, cb, lg, lb, batch, seq):
    t = p.shape[0]
    ts = 512
    ns = seq // ts
    glu_blk = P_GLU // (2 * CONV_CH)
    vec = lambda w: pl.BlockSpec((1, w), lambda b, i: (0, 0))
    return pl.pallas_call(
        _conv_kernel,
        grid=(batch, ns),
        in_specs=[
            pl.BlockSpec((ts, 2 * CONV_CH), lambda b, i: (b * ns + i, glu_blk)),
            pl.BlockSpec((ts, 2 * CONV_CH), lambda b, i: (b * ns + jnp.maximum(i - 1, 0), glu_blk)),
            vec(2 * CONV_CH),
            pl.BlockSpec((CONV_WIDTH, CONV_CH), lambda b, i: (0, 0)),
            vec(CONV_CH), vec(CONV_CH), vec(CONV_CH),
        ],
        out_specs=pl.BlockSpec((ts, CONV_CH), lambda b, i: (b * ns + i, 0)),
        out_shape=jax.ShapeDtypeStruct((t, CONV_CH), BF16),
        scratch_shapes=[pltpu.VMEM((CONV_HALO + ts + SUBLANES, CONV_CH), F32),
                        pltpu.VMEM((SUBLANES - 1, CONV_HALO + ts, CONV_CH), F32)],
        compiler_params=_cparams(("parallel", "parallel")),
        name="conv",
    )(p, p, inb, cw, cb, lg, lb)


MERGE_TN = 512


def _merge_kernel(a_ref, m_ref, c_ref, g0_ref, g1_ref, g2_ref, gb0_ref, gb1_ref, gb2_ref,
                  wa_ref, wm_ref, wc_ref, cob_ref, o_ref):
    a, m, c = a_ref[...], m_ref[...], c_ref[...]
    for n in range(D_MODEL // MERGE_TN):
        sl = slice(n * MERGE_TN, (n + 1) * MERGE_TN)

        def gate(g_ref, gb_ref):
            return jax.nn.sigmoid(g_ref[:, sl].astype(F32) + gb_ref[:, sl])

        y = gate(g0_ref, gb0_ref) * _dot(a, wa_ref[:, sl])
        y = y + gate(g1_ref, gb1_ref) * _dot(m, wm_ref[:, sl])
        y = y + gate(g2_ref, gb2_ref) * (_dot(c, wc_ref[:, sl]) + cob_ref[:, sl])
        o_ref[:, sl] = y.astype(BF16)


def _merge(attn, memo, convo, p, gate_b, wa, wm, wc, cob):
    t = attn.shape[0]
    tm = 512
    gblk = P_GATES // D_MODEL
    row = lambda w: pl.BlockSpec((tm, w), lambda i: (i, 0))
    full = lambda r, c: pl.BlockSpec((r, c), lambda i: (0, 0))
    gspec = lambda k: pl.BlockSpec((tm, D_MODEL), lambda i: (i, gblk + k))
    gbspec = lambda k: pl.BlockSpec((1, D_MODEL), lambda i: (0, k))
    return pl.pallas_call(
        _merge_kernel,
        grid=(t // tm,),
        in_specs=[row(ATTN_DIM), row(MEM_DIM), row(CONV_CH),
                  gspec(0), gspec(1), gspec(2), gbspec(0), gbspec(1), gbspec(2),
                  full(ATTN_DIM, D_MODEL), full(MEM_DIM, D_MODEL), full(CONV_CH, D_MODEL),
                  full(1, D_MODEL)],
        out_specs=row(D_MODEL),
        out_shape=jax.ShapeDtypeStruct((t, D_MODEL), BF16),
        compiler_params=_cparams(("parallel",)),
        name="merge",
    )(attn, memo, convo, p, p, p, gate_b, gate_b, gate_b, wa, wm, wc, cob)


def _outproj_kernel(x_ref, mg_ref, w_ref, g_ref, x1_ref, h2_ref):
    x1 = x_ref[...] + _dot(mg_ref[...], w_ref[...])
    x1_ref[...] = x1
    h2_ref[...] = _rms(x1, g_ref[...]).astype(BF16)


def _outproj(x, merged, w, g):
    t = x.shape[0]
    tm = 512
    row = pl.BlockSpec((tm, D_MODEL), lambda i: (i, 0))
    return pl.pallas_call(
        _outproj_kernel,
        grid=(t // tm,),
        in_specs=[row, row,
                  pl.BlockSpec((D_MODEL, D_MODEL), lambda i: (0, 0)),
                  pl.BlockSpec((1, D_MODEL), lambda i: (0, 0))],
        out_specs=[row, row],
        out_shape=[jax.ShapeDtypeStruct((t, D_MODEL), F32), jax.ShapeDtypeStruct((t, D_MODEL), BF16)],
        compiler_params=_cparams(("parallel",)),
        name="outproj",
    )(x, merged, w, g)


def _mlp_kernel(h_ref, x_ref, wu_ref, wd_ref, o_ref):
    @pl.when(pl.program_id(1) == 0)
    def _():
        o_ref[...] = x_ref[...]

    u = jnp.maximum(_dot(h_ref[...], wu_ref[...]), 0.0)
    o_ref[...] += _dot((u * u).astype(BF16), wd_ref[...])


def _mlp(h2, x1, wu, wd):
    t = x1.shape[0]
    tm, tf = 512, 1024
    return pl.pallas_call(
        _mlp_kernel,
        grid=(t // tm, FFN_DIM // tf),
        in_specs=[
            pl.BlockSpec((tm, D_MODEL), lambda i, f: (i, 0)),
            pl.BlockSpec((tm, D_MODEL), lambda i, f: (i, 0)),
            pl.BlockSpec((D_MODEL, tf), lambda i, f: (0, f)),
            pl.BlockSpec((tf, D_MODEL), lambda i, f: (f, 0)),
        ],
        out_specs=pl.BlockSpec((tm, D_MODEL), lambda i, f: (i, 0)),
        out_shape=jax.ShapeDtypeStruct((t, D_MODEL), F32),
        compiler_params=_cparams(("parallel", "arbitrary")),
        name="mlp",
    )(h2, x1, wu, wd)


def _regroup_w_in(w):
    o_k = ATTN_DIM
    o_v = o_k + KV_DIM
    o_iq = o_v + KV_DIM
    o_ik = o_iq + IDX_HEADS * IDX_DIM
    o_iw = o_ik + IDX_DIM
    o_glu = o_iw + IDX_HEADS
    o_mq = o_glu + 2 * CONV_CH
    o_g = o_mq + MEM_DIM
    main = jnp.concatenate([w[:, :o_k], w[:, o_iq:o_ik], w[:, o_glu:o_mq], w[:, o_mq:o_g],
                            w[:, o_k:o_v], w[:, o_v:o_iq], w[:, o_g:]], axis=1).astype(BF16)
    small = jnp.pad(w[:, o_ik:o_glu], ((0, 0), (0, PS_WIDTH - IDX_DIM - IDX_HEADS))).astype(BF16)
    return main, small


def kernel(x, mem, positions, norm1_g, w_in, q_norm_g, k_norm_g, mem_norm_g, w_mem_kv, mq_norm_g,
           mk_norm_g, conv_in_b, conv_w, conv_b, conv_ln_g, conv_ln_b, gate_b, w_attn_o, w_mem_o,
           w_conv_o, conv_o_b, w_out, norm2_g, w_up, w_down):
    batch, seq, _ = x.shape
    depth = w_in.shape[0]
    t = batch * seq
    xf = x.reshape(t, D_MODEL)
    mem2d = mem.reshape(batch * mem.shape[1], D_MODEL)
    tabs = _rope_tables(positions)
    row = lambda v: v.reshape(1, -1)

    for l in range(depth):
        w_main, w_small = _regroup_w_in(w_in[l])
        p, ps = _proj(xf, row(norm1_g[l]), w_main, w_small)
        qn, iqr, kn, vt, ika, ikb, iwt = _prep(p, ps, tabs, row(q_norm_g[l]), row(k_norm_g[l]))
        attn = _dsa(qn, iqr, iwt, kn, vt, ika, ikb, batch, seq)
        mk, mv = _mem_kv(mem2d, row(mem_norm_g[l]), w_mem_kv[l].astype(BF16), row(mk_norm_g[l]), batch)
        memo = _mem_attn(p, mk, mv, row(mq_norm_g[l]), batch, seq)
        convo = _conv(p, row(conv_in_b[l]), conv_w[l], row(conv_b[l]), row(conv_ln_g[l]),
                      row(conv_ln_b[l]), batch, seq)
        merged = _merge(attn, memo, convo, p, row(gate_b[l]), w_attn_o[l].astype(BF16),
                        w_mem_o[l].astype(BF16), w_conv_o[l].astype(BF16), row(conv_o_b[l]))
        x1, h2 = _outproj(xf, merged, w_out[l].astype(BF16), row(norm2_g[l]))
        xf = _mlp(h2, x1, w_up[l].astype(BF16), w_down[l].astype(BF16))
    return xf.reshape(batch, seq, D_MODEL)
```

```python
import functools
import math

import jax
import jax.numpy as jnp
from jax import lax
from jax.experimental import pallas as pl
from jax.experimental.pallas import tpu as pltpu

F32 = jnp.float32
BF16 = jnp.bfloat16
I32 = jnp.int32
I16 = jnp.int16

D_MODEL = 2048
HEAD_DIM = 128
N_HEADS = 8
N_KV_HEADS = 2
IDX_HEADS = 16
IDX_DIM = 64
TOPK = 256
CONV_CH = 512
CONV_WIDTH = 31
MEM_HEADS = 4
FFN_DIM = 4 * D_MODEL
ROPE_THETA = 500000.0
N_BRANCH = 3
EPS = 1e-6
ATTN_DIM = N_HEADS * HEAD_DIM
KV_DIM = N_KV_HEADS * HEAD_DIM
MEM_DIM = MEM_HEADS * HEAD_DIM
HEAD_ROT_HALF = HEAD_DIM // 8
IDX_ROT_HALF = IDX_DIM // 8

LANES = 128
VMEM_LIMIT = 56 * 1024 * 1024

P_Q, P_IQ, P_GLU, P_MQ, P_K, P_V, P_GATES = 0, 1024, 2048, 3072, 3584, 3840, 4096
P_WIDTH = P_GATES + N_BRANCH * D_MODEL
PS_WIDTH = LANES

QB = 128
KC = 512
SUB = 256
SLAB = 32
SUBLANES = 8
PACK16 = 16
LOG2E = 1.4426950408889634
NEG_BIG = -1e30
INT_MIN = -2 ** 31
KEY_NEG_INF = -2139095041
KEY_POS_INF = 2139095040


def _cparams(sem, vmem=VMEM_LIMIT):
    return pltpu.CompilerParams(dimension_semantics=sem, vmem_limit_bytes=vmem)


def _dot(a, b):
    return jnp.dot(a, b, preferred_element_type=F32)


def _dot_nt(a, b):
    return lax.dot_general(a, b, (((1,), (1,)), ((), ())), preferred_element_type=F32)


def _rms(xf, g):
    return xf * lax.rsqrt(jnp.mean(xf * xf, axis=-1, keepdims=True) + EPS) * g


def _rope_table_kernel(pos_ref, ch_ref, sh1_ref, sh2_ref, ci_ref, si1_ref, si2_ref):
    pos = pos_ref[...].astype(F32)
    lane = lax.broadcasted_iota(I32, (1, LANES), 1)

    def tables(period, half, c_ref, s1_ref, s2_ref):
        r = lane & (period - 1)
        fi = (r & (half - 1)).astype(F32)
        inv = jnp.exp(fi * (-math.log(ROPE_THETA) / half))
        ang = pos * inv
        c = jnp.cos(ang)
        s = jnp.sin(ang)
        first = r < half
        second = (r >= half) & (r < 2 * half)
        c_ref[...] = jnp.where(first | second, c, 1.0)
        s1_ref[...] = jnp.where(second, s, 0.0)
        s2_ref[...] = jnp.where(first, -s, 0.0)

    tables(HEAD_DIM, HEAD_ROT_HALF, ch_ref, sh1_ref, sh2_ref)
    tables(IDX_DIM, IDX_ROT_HALF, ci_ref, si1_ref, si2_ref)


def _rope_tables(positions):
    t = positions.size
    tp = 512
    tab = jax.ShapeDtypeStruct((t, LANES), F32)
    spec = pl.BlockSpec((tp, LANES), lambda i: (i, 0))
    return pl.pallas_call(
        _rope_table_kernel,
        grid=(t // tp,),
        in_specs=[pl.BlockSpec((tp, 1), lambda i: (i, 0))],
        out_specs=[spec] * 6,
        out_shape=[tab] * 6,
        compiler_params=_cparams(("parallel",)),
        name="rope_tables",
    )(positions.reshape(t, 1))


def _rope(t, c, s1, s2, half):
    return t * c + pltpu.roll(t, half, 1) * s1 + pltpu.roll(t, LANES - half, 1) * s2


def _proj_kernel(x_ref, g_ref, w_ref, ws_ref, o_ref, os_ref, h_ref):
    @pl.when(pl.program_id(1) == 0)
    def _():
        h = _rms(x_ref[...], g_ref[...]).astype(BF16)
        h_ref[...] = h
        os_ref[...] = _dot(h, ws_ref[...])

    o_ref[...] = _dot(h_ref[...], w_ref[...]).astype(o_ref.dtype)


def _proj(x, g, w, ws):
    t = x.shape[0]
    tm, tn = 1024, 1024
    return pl.pallas_call(
        _proj_kernel,
        grid=(t // tm, P_WIDTH // tn),
        in_specs=[
            pl.BlockSpec((tm, D_MODEL), lambda i, j: (i, 0)),
            pl.BlockSpec((1, D_MODEL), lambda i, j: (0, 0)),
            pl.BlockSpec((D_MODEL, tn), lambda i, j: (0, j)),
            pl.BlockSpec((D_MODEL, PS_WIDTH), lambda i, j: (0, 0)),
        ],
        out_specs=[
            pl.BlockSpec((tm, tn), lambda i, j: (i, j)),
            pl.BlockSpec((tm, PS_WIDTH), lambda i, j: (i, 0)),
        ],
        out_shape=[
            jax.ShapeDtypeStruct((t, P_WIDTH), BF16),
            jax.ShapeDtypeStruct((t, PS_WIDTH), F32),
        ],
        scratch_shapes=[pltpu.VMEM((tm, D_MODEL), BF16)],
        compiler_params=_cparams(("parallel", "arbitrary")),
        name="proj",
    )(x, g, w, ws)


def _prep_kernel(q_ref, iq_ref, kv_ref, ps_ref, ch_ref, sh1_ref, sh2_ref, ci_ref, si1_ref,
                 si2_ref, qg_ref, kg_ref, qn_ref, iqr_ref, kn_ref, vt_ref, ika_ref, ikb_ref,
                 iwt_ref):
    ch, sh1, sh2 = ch_ref[...], sh1_ref[...], sh2_ref[...]
    ci, si1, si2 = ci_ref[...], si1_ref[...], si2_ref[...]
    scale = (HEAD_DIM ** -0.5) * LOG2E
    for h in range(N_HEADS):
        sl = slice(h * HEAD_DIM, (h + 1) * HEAD_DIM)
        qh = _rms(q_ref[:, sl].astype(F32), qg_ref[...])
        qn_ref[:, sl] = (_rope(qh, ch, sh1, sh2, HEAD_ROT_HALF) * scale).astype(BF16)
        iqh = iq_ref[:, sl].astype(F32)
        iqr_ref[:, sl] = _rope(iqh, ci, si1, si2, IDX_ROT_HALF).astype(BF16)
    for g in range(N_KV_HEADS):
        sl = slice(g * HEAD_DIM, (g + 1) * HEAD_DIM)
        kh = _rms(kv_ref[:, sl].astype(F32), kg_ref[...])
        kn_ref[:, sl] = _rope(kh, ch, sh1, sh2, HEAD_ROT_HALF).astype(BF16)
        vh = kv_ref[:, KV_DIM + g * HEAD_DIM:KV_DIM + (g + 1) * HEAD_DIM].astype(F32)
        vt_ref[0, sl, :] = vh.T.astype(BF16)
    ps = ps_ref[...]
    lane = lax.broadcasted_iota(I32, (1, LANES), 1)
    ikr = jnp.where(lane < IDX_DIM, _rope(ps, ci, si1, si2, IDX_ROT_HALF), 0.0)
    ika_ref[...] = ikr.astype(BF16)
    ikb_ref[...] = pltpu.roll(ikr, IDX_DIM, 1).astype(BF16)
    idx_scale = (IDX_DIM ** -0.5) * (IDX_HEADS ** -0.5)
    iws = jnp.where(lane < IDX_HEADS, pltpu.roll(ps, LANES - IDX_DIM, 1) * idx_scale, 0.0)
    iwt_ref[...] = iws.T


def _prep(p, ps, tabs, qg, kg):
    t = p.shape[0]
    tp = KC
    tab_spec = pl.BlockSpec((tp, LANES), lambda i: (i, 0))
    vec_spec = pl.BlockSpec((1, HEAD_DIM), lambda i: (0, 0))

    def out(width, dtype):
        return (pl.BlockSpec((tp, width), lambda i: (i, 0)), jax.ShapeDtypeStruct((t, width), dtype))

    outs = [out(ATTN_DIM, BF16), out(ATTN_DIM, BF16), out(KV_DIM, BF16),
            (pl.BlockSpec((1, KV_DIM, tp), lambda i: (i, 0, 0)),
             jax.ShapeDtypeStruct((t // tp, KV_DIM, tp), BF16)),
            out(LANES, BF16), out(LANES, BF16),
            (pl.BlockSpec((LANES, tp), lambda i: (0, i)), jax.ShapeDtypeStruct((LANES, t), F32))]
    return pl.pallas_call(
        _prep_kernel,
        grid=(t // tp,),
        in_specs=[
            pl.BlockSpec((tp, ATTN_DIM), lambda i: (i, P_Q // ATTN_DIM)),
            pl.BlockSpec((tp, ATTN_DIM), lambda i: (i, P_IQ // ATTN_DIM)),
            pl.BlockSpec((tp, 2 * KV_DIM), lambda i: (i, P_K // (2 * KV_DIM))),
            tab_spec,
        ] + [tab_spec] * 6 + [vec_spec, vec_spec],
        out_specs=[o[0] for o in outs],
        out_shape=[o[1] for o in outs],
        compiler_params=_cparams(("parallel",)),
        name="prep",
    )(p, p, p, ps, *tabs, qg, kg)


def _sortable(x):
    bits = lax.bitcast_convert_type(x, I32)
    return bits ^ ((bits >> 31) & 0x7FFFFFFF)


N_PAIRS = N_HEADS // 2
N_PART = 4


def _dsa_kernel(qn_ref, iq_ref, iwt_ref, kn_ref, vt_ref, ika_ref, ikb_ref, o_ref,
                key_ref, hi_ref, lo_ref, thr_ref, nge_ref, bias_ref, m_ref, l_ref, acc_ref,
                sa_ref, sb_ref, p_ref):
    j = pl.program_id(1)
    nck = j // (KC // QB) + 1
    seq = kn_ref.shape[0]

    iwt = iwt_ref[...]
    q_pos = j * QB + lax.broadcasted_iota(I32, (SUB, QB), 1)
    n_grp = IDX_HEADS // 2

    def score_chunk(c, carry):
        for hf in range(KC // SUB):
            start = pl.multiple_of(c * KC + hf * SUB, SUB)
            ka = ika_ref[pl.ds(start, SUB), :]
            kb = ikb_ref[pl.ds(start, SUB), :]
            acc = jnp.zeros((SUB, QB), F32)
            for a in range(n_grp // 2):
                b = a + n_grp // 2
                iq2 = jnp.concatenate([iq_ref[:, a * LANES:(a + 1) * LANES],
                                       iq_ref[:, b * LANES:(b + 1) * LANES]], axis=0)
                sa = _dot_nt(ka, iq2)
                sb = _dot_nt(kb, iq2)
                acc = acc + jnp.maximum(sa[:, :QB], 0.0) * iwt[2 * a:2 * a + 1, :]
                acc = acc + jnp.maximum(sb[:, :QB], 0.0) * iwt[2 * a + 1:2 * a + 2, :]
                acc = acc + jnp.maximum(sa[:, QB:], 0.0) * iwt[2 * b:2 * b + 1, :]
                acc = acc + jnp.maximum(sb[:, QB:], 0.0) * iwt[2 * b + 1:2 * b + 2, :]
            k_pos = start + lax.broadcasted_iota(I32, (SUB, QB), 0)
            acc = jnp.where(k_pos <= q_pos, acc, -jnp.inf)
            key = _sortable(acc)
            key_ref[c, hf * SUB:(hf + 1) * SUB, :] = key
            hi_ref[c, hf * SUB:(hf + 1) * SUB, :] = (key >> 16).astype(I16)
        return carry

    lax.fori_loop(0, nck, score_chunk, 0)

    rep16 = lambda v: jnp.concatenate([v, v], axis=0).astype(I16)
    n_loaded = jnp.broadcast_to(nck * KC, (SUBLANES, QB)).astype(I32)

    @pl.when(nck % 2 == 1)
    def _():
        key_ref[nck] = jnp.full((KC, QB), INT_MIN, I32)
        hi_ref[nck] = jnp.full((KC, QB), -0x8000, I16)

    def search(nc):
        def count16(ref, cand16):
            one = jnp.ones((PACK16, QB), I16)
            zero = jnp.zeros((PACK16, QB), I16)
            parts = [zero] * N_PART
            for c in range(nc):
                for r in range(KC // PACK16):
                    k = ref[c, r * PACK16:(r + 1) * PACK16, :]
                    parts[r % N_PART] = parts[r % N_PART] + jnp.where(k >= cand16, one, zero)
            tot = ((parts[0] + parts[1]) + (parts[2] + parts[3])).astype(I32)
            return jnp.broadcast_to(jnp.sum(tot, axis=0, keepdims=True), (SUBLANES, QB))

        def search16(ref, base, n_all):
            def search_pass(p, carry):
                tu, n_at = carry
                cand_u = tu | lax.shift_left(jnp.int32(1), 15 - p)
                n = base + count16(ref, rep16(cand_u ^ 0x8000))
                ok = n >= TOPK
                return jnp.where(ok, cand_u, tu), jnp.where(ok, n, n_at)

            return lax.fori_loop(0, 16, search_pass, (jnp.zeros((SUBLANES, QB), I32), n_all))

        thi_u, n_ge = search16(hi_ref, jnp.zeros((SUBLANES, QB), I32), n_loaded)
        thi16 = rep16(thi_u ^ 0x8000)
        n_hi_gt = jnp.where(thi_u == 0xFFFF, 0,
                            count16(hi_ref, rep16(jnp.minimum(thi_u + 1, 0xFFFF) ^ 0x8000)))
        for c in range(nc):
            for r in range(KC // PACK16):
                sl = slice(r * PACK16, (r + 1) * PACK16)
                lo = ((key_ref[c, sl, :] & 0xFFFF) ^ 0x8000).astype(I16)
                lo_ref[c, sl, :] = jnp.where(hi_ref[c, sl, :] == thi16, lo, jnp.int16(-0x8000))
        tlo_u, n_ge = search16(lo_ref, n_hi_gt, n_ge)
        thr_ref[...] = (lax.shift_left(thi_u, 16) | tlo_u) ^ INT_MIN
        nge_ref[...] = n_ge

    for cls in range(1, key_ref.shape[0] // 2 + 1):
        pl.when((nck + 1) // 2 == cls)(functools.partial(search, 2 * cls))
    thr = thr_ref[...]
    n_ge = nge_ref[...]

    row_iota = lax.broadcasted_iota(I32, (SUBLANES, QB), 0)
    tied = (n_ge > TOPK) & (thr > KEY_NEG_INF)
    any_tied = jnp.max(jnp.where(tied, 1, 0)) > 0

    def count32(pred):
        def chunk(c, parts):
            parts = list(parts)
            for r in range(KC // SUBLANES):
                k = key_ref[c, r * SUBLANES:(r + 1) * SUBLANES, :]
                hit = pred(k, c * KC + r * SUBLANES)
                parts[r % N_PART] = parts[r % N_PART] + jnp.where(hit, 1, 0)
            return tuple(parts)

        zero = jnp.zeros((SUBLANES, QB), I32)
        parts = lax.fori_loop(0, nck, chunk, (zero,) * N_PART)
        tot = (parts[0] + parts[1]) + (parts[2] + parts[3])
        return jnp.broadcast_to(jnp.sum(tot, axis=0, keepdims=True), (SUBLANES, QB))

    thr_fin = jnp.maximum(thr, KEY_NEG_INF + 1)

    @pl.when(jnp.logical_not(any_tied))
    def _():
        def bias_chunk(c, carry):
            for r in range(KC // SUBLANES):
                sl = slice(r * SUBLANES, (r + 1) * SUBLANES)
                k = key_ref[c, sl, :]
                sel = (k >= thr_fin) & (k < KEY_POS_INF)
                bias_ref[c, sl, :] = jnp.where(sel, 0.0, NEG_BIG)
            return carry

        lax.fori_loop(0, nck, bias_chunk, 0)

    @pl.when(any_tied)
    def _():
        need = TOPK - count32(lambda k, _: k > thr)

        def pos_pass(p, x):
            cand = x | lax.shift_left(jnp.int32(1), (seq.bit_length() - 2) - p)
            n = count32(lambda k, r0: (k == thr) & (row_iota + r0 < cand))
            return jnp.where(n < need, cand, x)

        xlim = lax.fori_loop(0, seq.bit_length() - 1, pos_pass, jnp.zeros((SUBLANES, QB), I32))

        def bias_chunk(c, carry):
            for r in range(KC // SUBLANES):
                sl = slice(r * SUBLANES, (r + 1) * SUBLANES)
                k = key_ref[c, sl, :]
                pos = row_iota + (c * KC + r * SUBLANES)
                sel = (k > thr) | ((k == thr) & (pos <= xlim))
                sel = sel & (k >= thr_fin) & (k < KEY_POS_INF)
                bias_ref[c, sl, :] = jnp.where(sel, 0.0, NEG_BIG)
            return carry

        lax.fori_loop(0, nck, bias_chunk, 0)

    m_ref[...] = jnp.full(m_ref.shape, NEG_BIG, F32)
    l_ref[...] = jnp.zeros(l_ref.shape, F32)
    acc_ref[...] = jnp.zeros(acc_ref.shape, F32)
    group = N_HEADS // N_KV_HEADS

    def logits(c, s_ref):
        col_max = []
        for pr in range(N_PAIRS):
            g = (2 * pr) // group
            q2 = jnp.concatenate([qn_ref[:, (2 * pr) * HEAD_DIM:(2 * pr + 1) * HEAD_DIM],
                                  qn_ref[:, (2 * pr + 1) * HEAD_DIM:(2 * pr + 2) * HEAD_DIM]],
                                 axis=0)
            mx = jnp.full((SUBLANES, 2 * QB), NEG_BIG, F32)
            for hf in range(KC // SUB):
                start = pl.multiple_of(c * KC + hf * SUB, SUB)
                kc = kn_ref[pl.ds(start, SUB), g * HEAD_DIM:(g + 1) * HEAD_DIM]
                bias = bias_ref[c, hf * SUB:(hf + 1) * SUB, :]
                s = _dot_nt(kc, q2) + jnp.concatenate([bias, bias], axis=1)
                s_ref[pr, hf * SUB:(hf + 1) * SUB, :] = s
                for r in range(SUB // SUBLANES):
                    mx = jnp.maximum(mx, s[r * SUBLANES:(r + 1) * SUBLANES, :])
            col_max.append(jnp.broadcast_to(jnp.max(mx, axis=0, keepdims=True), (SUBLANES, 2 * QB)))
        return tuple(col_max)

    def finish(c, s_ref, col_max):
        alphas = []
        for pr in range(N_PAIRS):
            m_prev = m_ref[pr]
            m_new = jnp.maximum(m_prev, col_max[pr])
            alpha = jnp.exp2(m_prev - m_new)
            lsum = jnp.zeros((SUBLANES, 2 * QB), F32)
            for t in range(KC // SLAB):
                rows = slice(t * SLAB, (t + 1) * SLAB)
                p = jnp.exp2(s_ref[pr, rows, :] - m_new[0:1, :])
                for r in range(SLAB // SUBLANES):
                    lsum = lsum + p[r * SUBLANES:(r + 1) * SUBLANES, :]
                p_ref[pr, rows, :] = p.astype(BF16)
            m_ref[pr] = m_new
            l_ref[pr] = alpha * l_ref[pr] + jnp.sum(lsum, axis=0, keepdims=True)
            alphas.append(alpha)
        for pr in range(N_PAIRS):
            g = (2 * pr) // group
            vt = vt_ref[c, g * HEAD_DIM:(g + 1) * HEAD_DIM, :]
            acc_ref[pr] = acc_ref[pr] * alphas[pr][0:1, :] + _dot(vt, p_ref[pr])

    def attn_step(i, col_max):
        c = 2 * i
        mid = logits(c + 1, sb_ref)
        finish(c, sa_ref, col_max)
        nxt = logits(c + 2, sa_ref)
        finish(c + 1, sb_ref, mid)
        return nxt

    n_steps = (nck - 1) // 2
    tail = 2 * n_steps
    col_max = lax.fori_loop(0, n_steps, attn_step, logits(0, sa_ref))

    @pl.when(tail == nck - 1)
    def _():
        finish(tail, sa_ref, col_max)

    @pl.when(tail != nck - 1)
    def _():
        mid = logits(tail + 1, sb_ref)
        finish(tail, sa_ref, col_max)
        finish(tail + 1, sb_ref, mid)

    for pr in range(N_PAIRS):
        o_t = acc_ref[pr] / l_ref[pr][0:1, :]
        for i in range(2):
            h = 2 * pr + i
            o_ref[:, h * HEAD_DIM:(h + 1) * HEAD_DIM] = o_t[:, i * QB:(i + 1) * QB].T.astype(BF16)


def _dsa(qn, iqr, iwt, kn, vt, ika, ikb, batch, seq):
    t = qn.shape[0]
    nb = seq // QB
    nchunk = seq // KC
    qmap = lambda b, j: (b * nb + j, 0)
    bmap = lambda b, j: (b, 0)
    return pl.pallas_call(
        _dsa_kernel,
        grid=(batch, nb),
        in_specs=[
            pl.BlockSpec((QB, ATTN_DIM), qmap),
            pl.BlockSpec((QB, ATTN_DIM), qmap),
            pl.BlockSpec((IDX_HEADS, QB), lambda b, j: (0, b * nb + j)),
            pl.BlockSpec((seq, KV_DIM), bmap),
            pl.BlockSpec((nchunk, KV_DIM, KC), lambda b, j: (b, 0, 0)),
            pl.BlockSpec((seq, LANES), bmap),
            pl.BlockSpec((seq, LANES), bmap),
        ],
        out_specs=pl.BlockSpec((QB, ATTN_DIM), qmap),
        out_shape=jax.ShapeDtypeStruct((t, ATTN_DIM), BF16),
        scratch_shapes=[
            pltpu.VMEM((nchunk, KC, QB), I32),
            pltpu.VMEM((nchunk, KC, QB), I16),
            pltpu.VMEM((nchunk, KC, QB), I16),
            pltpu.VMEM((SUBLANES, QB), I32),
            pltpu.VMEM((SUBLANES, QB), I32),
            pltpu.VMEM((nchunk, KC, QB), F32),
            pltpu.VMEM((N_PAIRS, SUBLANES, 2 * QB), F32),
            pltpu.VMEM((N_PAIRS, SUBLANES, 2 * QB), F32),
            pltpu.VMEM((N_PAIRS, HEAD_DIM, 2 * QB), F32),
            pltpu.VMEM((N_PAIRS, KC, 2 * QB), F32),
            pltpu.VMEM((N_PAIRS, KC, 2 * QB), F32),
            pltpu.VMEM((N_PAIRS, KC, 2 * QB), BF16),
        ],
        compiler_params=_cparams(("parallel", "arbitrary")),
        name="dsa",
    )(qn, iqr, iwt, kn, vt, ika, ikb)


def _mem_kv_kernel(mem_ref, g_ref, w_ref, kg_ref, mk_ref, mv_ref):
    m = _rms(mem_ref[...], g_ref[...]).astype(BF16)
    kv = _dot(m, w_ref[...])
    for h in range(MEM_HEADS):
        sl = slice(h * HEAD_DIM, (h + 1) * HEAD_DIM)
        mk_ref[:, sl] = _rms(kv[:, sl], kg_ref[...]).astype(BF16)
    mv_ref[...] = kv[:, MEM_DIM:].astype(BF16)


def _mem_kv(mem2d, g, w, kg, batch):
    n = mem2d.shape[0]
    m = n // batch
    out = jax.ShapeDtypeStruct((n, MEM_DIM), BF16)
    ospec = pl.BlockSpec((m, MEM_DIM), lambda b: (b, 0))
    return pl.pallas_call(
        _mem_kv_kernel,
        grid=(batch,),
        in_specs=[
            pl.BlockSpec((m, D_MODEL), lambda b: (b, 0)),
            pl.BlockSpec((1, D_MODEL), lambda b: (0, 0)),
            pl.BlockSpec((D_MODEL, 2 * MEM_DIM), lambda b: (0, 0)),
            pl.BlockSpec((1, HEAD_DIM), lambda b: (0, 0)),
        ],
        out_specs=[ospec, ospec],
        out_shape=[out, out],
        compiler_params=_cparams(("parallel",)),
        name="mem_kv",
    )(mem2d, g, w, kg)


def _mem_attn_kernel(q_ref, mk_ref, mv_ref, qg_ref, o_ref):
    scale = HEAD_DIM ** -0.5
    for h in range(MEM_HEADS):
        sl = slice(h * HEAD_DIM, (h + 1) * HEAD_DIM)
        qh = (_rms(q_ref[:, sl].astype(F32), qg_ref[...]) * scale).astype(BF16)
        s = _dot_nt(qh, mk_ref[:, sl])
        p = jnp.exp(s - jnp.max(s, axis=1, keepdims=True))
        l = jnp.sum(p, axis=1, keepdims=True)
        o = _dot((p / l).astype(BF16), mv_ref[:, sl])
        o_ref[:, sl] = o.astype(BF16)


def _mem_attn(p, mk, mv, qg, batch, seq):
    t = p.shape[0]
    tq = 512
    nq = seq // tq
    m = mk.shape[0] // batch
    return pl.pallas_call(
        _mem_attn_kernel,
        grid=(batch, nq),
        in_specs=[
            pl.BlockSpec((tq, MEM_DIM), lambda b, i: (b * nq + i, P_MQ // MEM_DIM)),
            pl.BlockSpec((m, MEM_DIM), lambda b, i: (b, 0)),
            pl.BlockSpec((m, MEM_DIM), lambda b, i: (b, 0)),
            pl.BlockSpec((1, HEAD_DIM), lambda b, i: (0, 0)),
        ],
        out_specs=pl.BlockSpec((tq, MEM_DIM), lambda b, i: (b * nq + i, 0)),
        out_shape=jax.ShapeDtypeStruct((t, MEM_DIM), BF16),
        compiler_params=_cparams(("parallel", "parallel")),
        name="mem_attn",
    )(p, mk, mv, qg)


CONV_HALO = 32
CONV_ROWS = 64


def _conv_kernel(cur_ref, prev_ref, inb_ref, cw_ref, cb_ref, lg_ref, lb_ref, o_ref, u_ref, sh_ref):
    ts = cur_ref.shape[0]
    n = CONV_HALO + ts

    def glu(x):
        x = x.astype(F32) + inb_ref[...]
        return x[:, :CONV_CH] * jax.nn.sigmoid(x[:, CONV_CH:])

    u_prev = glu(prev_ref[ts - CONV_HALO:, :])
    u_ref[:CONV_HALO, :] = jnp.where(pl.program_id(1) == 0, 0.0, u_prev)
    u_ref[CONV_HALO:n, :] = glu(cur_ref[...])
    u_ref[n:, :] = jnp.zeros((SUBLANES, CONV_CH), F32)
    for r in range(1, SUBLANES):
        sh_ref[r - 1] = u_ref[r:r + n, :]

    first_tap = CONV_HALO - (CONV_WIDTH - 1)
    for t in range(ts // CONV_ROWS):
        base = t * CONV_ROWS
        y = jnp.broadcast_to(cb_ref[...], (CONV_ROWS, CONV_CH))
        for w in range(CONV_WIDTH):
            r = (first_tap + w) % SUBLANES
            a = base + first_tap + w - r
            rows = u_ref[a:a + CONV_ROWS, :] if r == 0 else sh_ref[r - 1, a:a + CONV_ROWS, :]
            y = y + rows * cw_ref[w:w + 1, :]
        mu = jnp.mean(y, axis=-1, keepdims=True)
        d = y - mu
        var = jnp.mean(d * d, axis=-1, keepdims=True)
        z = d * lax.rsqrt(var + EPS) * lg_ref[...] + lb_ref[...]
        o_ref[base:base + CONV_ROWS, :] = (z * jax.nn.sigmoid(z)).astype(BF16)


def _conv(p, inb, cw, cb, lg, lb, batch, seq):
    t = p.shape[0]
    ts = 512
    ns = seq // ts
    glu_blk = P_GLU // (2 * CONV_CH)
    vec = lambda w: pl.BlockSpec((1, w), lambda b, i: (0, 0))
    return pl.pallas_call(
        _conv_kernel,
        grid=(batch, ns),
        in_specs=[
            pl.BlockSpec((ts, 2 * CONV_CH), lambda b, i: (b * ns + i, glu_blk)),
            pl.BlockSpec((ts, 2 * CONV_CH), lambda b, i: (b * ns + jnp.maximum(i - 1, 0), glu_blk)),
            vec(2 * CONV_CH),
            pl.BlockSpec((CONV_WIDTH, CONV_CH), lambda b, i: (0, 0)),
            vec(CONV_CH), vec(CONV_CH), vec(CONV_CH),
        ],
        out_specs=pl.BlockSpec((ts, CONV_CH), lambda b, i: (b * ns + i, 0)),
        out_shape=jax.ShapeDtypeStruct((t, CONV_CH), BF16),
        scratch_shapes=[pltpu.VMEM((CONV_HALO + ts + SUBLANES, CONV_CH), F32),
                        pltpu.VMEM((SUBLANES - 1, CONV_HALO + ts, CONV_CH), F32)],
        compiler_params=_cparams(("parallel", "parallel")),
        name="conv",
    )(p, p, inb, cw, cb, lg, lb)


MERGE_TN = 512


def _merge_kernel(a_ref, m_ref, c_ref, g0_ref, g1_ref, g2_ref, gb0_ref, gb1_ref, gb2_ref,
                  wa_ref, wm_ref, wc_ref, cob_ref, o_ref):
    a, m, c = a_ref[...], m_ref[...], c_ref[...]
    for n in range(D_MODEL // MERGE_TN):
        sl = slice(n * MERGE_TN, (n + 1) * MERGE_TN)

        def gate(g_ref, gb_ref):
            return jax.nn.sigmoid(g_ref[:, sl].astype(F32) + gb_ref[:, sl])

        y = gate(g0_ref, gb0_ref) * _dot(a, wa_ref[:, sl])
        y = y + gate(g1_ref, gb1_ref) * _dot(m, wm_ref[:, sl])
        y = y + gate(g2_ref, gb2_ref) * (_dot(c, wc_ref[:, sl]) + cob_ref[:, sl])
        o_ref[:, sl] = y.astype(BF16)


def _merge(attn, memo, convo, p, gate_b, wa, wm, wc, cob):
    t = attn.shape[0]
    tm = 512
    gblk = P_GATES // D_MODEL
    row = lambda w: pl.BlockSpec((tm, w), lambda i: (i, 0))
    full = lambda r, c: pl.BlockSpec((r, c), lambda i: (0, 0))
    gspec = lambda k: pl.BlockSpec((tm, D_MODEL), lambda i: (i, gblk + k))
    gbspec = lambda k: pl.BlockSpec((1, D_MODEL), lambda i: (0, k))
    return pl.pallas_call(
        _merge_kernel,
        grid=(t // tm,),
        in_specs=[row(ATTN_DIM), row(MEM_DIM), row(CONV_CH),
                  gspec(0), gspec(1), gspec(2), gbspec(0), gbspec(1), gbspec(2),
                  full(ATTN_DIM, D_MODEL), full(MEM_DIM, D_MODEL), full(CONV_CH, D_MODEL),
                  full(1, D_MODEL)],
        out_specs=row(D_MODEL),
        out_shape=jax.ShapeDtypeStruct((t, D_MODEL), BF16),
        compiler_params=_cparams(("parallel",)),
        name="merge",
    )(attn, memo, convo, p, p, p, gate_b, gate_b, gate_b, wa, wm, wc, cob)


def _outproj_kernel(x_ref, mg_ref, w_ref, g_ref, x1_ref, h2_ref):
    x1 = x_ref[...] + _dot(mg_ref[...], w_ref[...])
    x1_ref[...] = x1
    h2_ref[...] = _rms(x1, g_ref[...]).astype(BF16)


def _outproj(x, merged, w, g):
    t = x.shape[0]
    tm = 512
    row = pl.BlockSpec((tm, D_MODEL), lambda i: (i, 0))
    return pl.pallas_call(
        _outproj_kernel,
        grid=(t // tm,),
        in_specs=[row, row,
                  pl.BlockSpec((D_MODEL, D_MODEL), lambda i: (0, 0)),
                  pl.BlockSpec((1, D_MODEL), lambda i: (0, 0))],
        out_specs=[row, row],
        out_shape=[jax.ShapeDtypeStruct((t, D_MODEL), F32), jax.ShapeDtypeStruct((t, D_MODEL), BF16)],
        compiler_params=_cparams(("parallel",)),
        name="outproj",
    )(x, merged, w, g)


def _mlp_kernel(h_ref, x_ref, wu_ref, wd_ref, o_ref):
    @pl.when(pl.program_id(1) == 0)
    def _():
        o_ref[...] = x_ref[...]

    u = jnp.maximum(_dot(h_ref[...], wu_ref[...]), 0.0)
    o_ref[...] += _dot((u * u).astype(BF16), wd_ref[...])


def _mlp(h2, x1, wu, wd):
    t = x1.shape[0]
    tm, tf = 512, 1024
    return pl.pallas_call(
        _mlp_kernel,
        grid=(t // tm, FFN_DIM // tf),
        in_specs=[
            pl.BlockSpec((tm, D_MODEL), lambda i, f: (i, 0)),
            pl.BlockSpec((tm, D_MODEL), lambda i, f: (i, 0)),
            pl.BlockSpec((D_MODEL, tf), lambda i, f: (0, f)),
            pl.BlockSpec((tf, D_MODEL), lambda i, f: (f, 0)),
        ],
        out_specs=pl.BlockSpec((tm, D_MODEL), lambda i, f: (i, 0)),
        out_shape=jax.ShapeDtypeStruct((t, D_MODEL), F32),
        compiler_params=_cparams(("parallel", "arbitrary")),
        name="mlp",
    )(h2, x1, wu, wd)


def _regroup_w_in(w):
    o_k = ATTN_DIM
    o_v = o_k + KV_DIM
    o_iq = o_v + KV_DIM
    o_ik = o_iq + IDX_HEADS * IDX_DIM
    o_iw = o_ik + IDX_DIM
    o_glu = o_iw + IDX_HEADS
    o_mq = o_glu + 2 * CONV_CH
    o_g = o_mq + MEM_DIM
    main = jnp.concatenate([w[:, :o_k], w[:, o_iq:o_ik], w[:, o_glu:o_mq], w[:, o_mq:o_g],
                            w[:, o_k:o_v], w[:, o_v:o_iq], w[:, o_g:]], axis=1).astype(BF16)
    small = jnp.pad(w[:, o_ik:o_glu], ((0, 0), (0, PS_WIDTH - IDX_DIM - IDX_HEADS))).astype(BF16)
    return main, small


def kernel(x, mem, positions, norm1_g, w_in, q_norm_g, k_norm_g, mem_norm_g, w_mem_kv, mq_norm_g,
           mk_norm_g, conv_in_b, conv_w, conv_b, conv_ln_g, conv_ln_b, gate_b, w_attn_o, w_mem_o,
           w_conv_o, conv_o_b, w_out, norm2_g, w_up, w_down):
    batch, seq, _ = x.shape
    depth = w_in.shape[0]
    t = batch * seq
    xf = x.reshape(t, D_MODEL)
    mem2d = mem.reshape(batch * mem.shape[1], D_MODEL)
    tabs = _rope_tables(positions)
    row = lambda v: v.reshape(1, -1)

    for l in range(depth):
        w_main, w_small = _regroup_w_in(w_in[l])
        p, ps = _proj(xf, row(norm1_g[l]), w_main, w_small)
        qn, iqr, kn, vt, ika, ikb, iwt = _prep(p, ps, tabs, row(q_norm_g[l]), row(k_norm_g[l]))
        attn = _dsa(qn, iqr, iwt, kn, vt, ika, ikb, batch, seq)
        mk, mv = _mem_kv(mem2d, row(mem_norm_g[l]), w_mem_kv[l].astype(BF16), row(mk_norm_g[l]), batch)
        memo = _mem_attn(p, mk, mv, row(mq_norm_g[l]), batch, seq)
        convo = _conv(p, row(conv_in_b[l]), conv_w[l], row(conv_b[l]), row(conv_ln_g[l]),
                      row(conv_ln_b[l]), batch, seq)
        merged = _merge(attn, memo, convo, p, row(gate_b[l]), w_attn_o[l].astype(BF16),
                        w_mem_o[l].astype(BF16), w_conv_o[l].astype(BF16), row(conv_o_b[l]))
        x1, h2 = _outproj(xf, merged, w_out[l].astype(BF16), row(norm2_g[l]))
        xf = _mlp(h2, x1, w_up[l].astype(BF16), w_down[l].astype(BF16))
    return xf.reshape(batch, seq, D_MODEL)
```

```python
import functools
import math

import jax
import jax.numpy as jnp
from jax import lax
from jax.experimental import pallas as pl
from jax.experimental.pallas import tpu as pltpu

F32 = jnp.float32
BF16 = jnp.bfloat16
I32 = jnp.int32

D_MODEL = 2048
HEAD_DIM = 128
N_HEADS = 8
N_KV_HEADS = 2
IDX_HEADS = 16
IDX_DIM = 64
TOPK = 256
CONV_CH = 512
CONV_WIDTH = 31
MEM_HEADS = 4
FFN_DIM = 4 * D_MODEL
ROPE_THETA = 500000.0
N_BRANCH = 3
EPS = 1e-6
ATTN_DIM = N_HEADS * HEAD_DIM
KV_DIM = N_KV_HEADS * HEAD_DIM
MEM_DIM = MEM_HEADS * HEAD_DIM
HEAD_ROT_HALF = HEAD_DIM // 8
IDX_ROT_HALF = IDX_DIM // 8

LANES = 128
VMEM_LIMIT = 56 * 1024 * 1024

P_Q = 0
P_K = P_Q + ATTN_DIM
P_V = P_K + KV_DIM
P_IQ = P_V + KV_DIM
P_HEAD = P_IQ + IDX_HEADS * IDX_DIM
P_GLU = P_HEAD
P_MQ = P_GLU + 2 * CONV_CH
P_GATES = P_MQ + MEM_DIM
P_WIDTH = P_GATES + N_BRANCH * D_MODEL
P_TAIL = P_WIDTH - P_HEAD
PS_WIDTH = LANES

QB = 128
KC = 512
SUB = 256
SLAB = 32
SUBLANES = 8
LOG2E = 1.4426950408889634
NEG_BIG = -1e30
INT_MIN = -2 ** 31
KEY_NEG_INF = -2139095041
KEY_POS_INF = 2139095040


def _cparams(sem, vmem=VMEM_LIMIT):
    return pltpu.CompilerParams(dimension_semantics=sem, vmem_limit_bytes=vmem)


def _dot(a, b):
    return jnp.dot(a, b, preferred_element_type=F32)


def _dot_nt(a, b):
    return lax.dot_general(a, b, (((1,), (1,)), ((), ())), preferred_element_type=F32)


def _rms(xf, g):
    return xf * lax.rsqrt(jnp.mean(xf * xf, axis=-1, keepdims=True) + EPS) * g


def _rope_table_kernel(pos_ref, ch_ref, sh1_ref, sh2_ref, ci_ref, si1_ref, si2_ref):
    pos = pos_ref[...].astype(F32)
    lane = lax.broadcasted_iota(I32, (1, LANES), 1)

    def tables(period, half, c_ref, s1_ref, s2_ref):
        r = lane & (period - 1)
        fi = (r & (half - 1)).astype(F32)
        inv = jnp.exp(fi * (-math.log(ROPE_THETA) / half))
        ang = pos * inv
        c = jnp.cos(ang)
        s = jnp.sin(ang)
        first = r < half
        second = (r >= half) & (r < 2 * half)
        c_ref[...] = jnp.where(first | second, c, 1.0)
        s1_ref[...] = jnp.where(second, s, 0.0)
        s2_ref[...] = jnp.where(first, -s, 0.0)

    tables(HEAD_DIM, HEAD_ROT_HALF, ch_ref, sh1_ref, sh2_ref)
    tables(IDX_DIM, IDX_ROT_HALF, ci_ref, si1_ref, si2_ref)


def _rope_tables(positions):
    t = positions.size
    tp = 512
    tab = jax.ShapeDtypeStruct((t, LANES), F32)
    spec = pl.BlockSpec((tp, LANES), lambda i: (i, 0))
    return pl.pallas_call(
        _rope_table_kernel,
        grid=(t // tp,),
        in_specs=[pl.BlockSpec((tp, 1), lambda i: (i, 0))],
        out_specs=[spec] * 6,
        out_shape=[tab] * 6,
        compiler_params=_cparams(("parallel",)),
        name="rope_tables",
    )(positions.reshape(t, 1))


def _rope(t, c, s1, s2, half):
    return t * c + pltpu.roll(t, half, 1) * s1 + pltpu.roll(t, LANES - half, 1) * s2


PROJ_TN = 1280


def _proj_kernel(x_ref, g_ref, wh_ref, wt_ref, ws_ref, o_ref, os_ref, h_ref):
    j = pl.program_id(1)

    @pl.when(j == 0)
    def _():
        h = _rms(x_ref[...], g_ref[...]).astype(BF16)
        h_ref[...] = h
        os_ref[...] = _dot(h, ws_ref[...])

    @pl.when(j < P_HEAD // PROJ_TN)
    def _():
        o_ref[...] = _dot(h_ref[...], wh_ref[...]).astype(o_ref.dtype)

    @pl.when(j >= P_HEAD // PROJ_TN)
    def _():
        o_ref[...] = _dot(h_ref[...], wt_ref[...]).astype(o_ref.dtype)


def _proj(x, g, wh, wt, ws):
    t = x.shape[0]
    tm, tn = 1024, PROJ_TN
    n_head = P_HEAD // tn
    return pl.pallas_call(
        _proj_kernel,
        grid=(t // tm, P_WIDTH // tn),
        in_specs=[
            pl.BlockSpec((tm, D_MODEL), lambda i, j: (i, 0)),
            pl.BlockSpec((1, D_MODEL), lambda i, j: (0, 0)),
            pl.BlockSpec((D_MODEL, tn), lambda i, j: (0, jnp.minimum(j, n_head - 1))),
            pl.BlockSpec((D_MODEL, tn), lambda i, j: (0, jnp.maximum(j - n_head, 0))),
            pl.BlockSpec((D_MODEL, PS_WIDTH), lambda i, j: (0, 0)),
        ],
        out_specs=[
            pl.BlockSpec((tm, tn), lambda i, j: (i, j)),
            pl.BlockSpec((tm, PS_WIDTH), lambda i, j: (i, 0)),
        ],
        out_shape=[
            jax.ShapeDtypeStruct((t, P_WIDTH), BF16),
            jax.ShapeDtypeStruct((t, PS_WIDTH), F32),
        ],
        scratch_shapes=[pltpu.VMEM((tm, D_MODEL), BF16)],
        compiler_params=_cparams(("parallel", "arbitrary")),
        name="proj",
    )(x, g, wh, wt, ws)


def _prep_kernel(q_ref, iql_ref, iqh_ref, kv_ref, ps_ref, ch_ref, sh1_ref, sh2_ref, ci_ref, si1_ref,
                 si2_ref, qg_ref, kg_ref, qn_ref, iqr_ref, kn_ref, vt_ref, ika_ref, ikb_ref,
                 iwt_ref):
    ch, sh1, sh2 = ch_ref[...], sh1_ref[...], sh2_ref[...]
    ci, si1, si2 = ci_ref[...], si1_ref[...], si2_ref[...]
    scale = (HEAD_DIM ** -0.5) * LOG2E
    half_groups = N_HEADS // 2
    for h in range(N_HEADS):
        sl = slice(h * HEAD_DIM, (h + 1) * HEAD_DIM)
        qh = _rms(q_ref[:, sl].astype(F32), qg_ref[...])
        qn_ref[:, sl] = (_rope(qh, ch, sh1, sh2, HEAD_ROT_HALF) * scale).astype(BF16)
        src = iql_ref if h < half_groups else iqh_ref
        hs = h % half_groups
        iqh = src[:, hs * LANES:(hs + 1) * LANES].astype(F32)
        iqr_ref[:, sl] = _rope(iqh, ci, si1, si2, IDX_ROT_HALF).astype(BF16)
    for g in range(N_KV_HEADS):
        sl = slice(g * HEAD_DIM, (g + 1) * HEAD_DIM)
        kh = _rms(kv_ref[:, sl].astype(F32), kg_ref[...])
        kn_ref[:, sl] = _rope(kh, ch, sh1, sh2, HEAD_ROT_HALF).astype(BF16)
        vh = kv_ref[:, KV_DIM + g * HEAD_DIM:KV_DIM + (g + 1) * HEAD_DIM].astype(F32)
        vt_ref[0, sl, :] = vh.T.astype(BF16)
    ps = ps_ref[...]
    lane = lax.broadcasted_iota(I32, (1, LANES), 1)
    ikr = jnp.where(lane < IDX_DIM, _rope(ps, ci, si1, si2, IDX_ROT_HALF), 0.0)
    ika_ref[...] = ikr.astype(BF16)
    ikb_ref[...] = pltpu.roll(ikr, IDX_DIM, 1).astype(BF16)
    idx_scale = (IDX_DIM ** -0.5) * (IDX_HEADS ** -0.5)
    iws = jnp.where(lane < IDX_HEADS, pltpu.roll(ps, LANES - IDX_DIM, 1) * idx_scale, 0.0)
    iwt_ref[...] = iws.T


def _prep(p, ps, tabs, qg, kg):
    t = p.shape[0]
    tp = KC
    half_iq = IDX_HEADS * IDX_DIM // 2
    tab_spec = pl.BlockSpec((tp, LANES), lambda i: (i, 0))
    vec_spec = pl.BlockSpec((1, HEAD_DIM), lambda i: (0, 0))

    def out(width, dtype):
        return (pl.BlockSpec((tp, width), lambda i: (i, 0)), jax.ShapeDtypeStruct((t, width), dtype))

    outs = [out(ATTN_DIM, BF16), out(ATTN_DIM, BF16), out(KV_DIM, BF16),
            (pl.BlockSpec((1, KV_DIM, tp), lambda i: (i, 0, 0)),
             jax.ShapeDtypeStruct((t // tp, KV_DIM, tp), BF16)),
            out(LANES, BF16), out(LANES, BF16),
            (pl.BlockSpec((LANES, tp), lambda i: (0, i)), jax.ShapeDtypeStruct((LANES, t), F32))]
    return pl.pallas_call(
        _prep_kernel,
        grid=(t // tp,),
        in_specs=[
            pl.BlockSpec((tp, ATTN_DIM), lambda i: (i, P_Q // ATTN_DIM)),
            pl.BlockSpec((tp, half_iq), lambda i: (i, P_IQ // half_iq)),
            pl.BlockSpec((tp, half_iq), lambda i: (i, P_IQ // half_iq + 1)),
            pl.BlockSpec((tp, 2 * KV_DIM), lambda i: (i, P_K // (2 * KV_DIM))),
            tab_spec,
        ] + [tab_spec] * 6 + [vec_spec, vec_spec],
        out_specs=[o[0] for o in outs],
        out_shape=[o[1] for o in outs],
        compiler_params=_cparams(("parallel",)),
        name="prep",
    )(p, p, p, p, ps, *tabs, qg, kg)


def _sortable(x):
    bits = lax.bitcast_convert_type(x, I32)
    return bits ^ ((bits >> 31) & 0x7FFFFFFF)


N_PAIRS = N_HEADS // 2
N_PART = 4


def _dsa_kernel(qn_ref, iq_ref, iwt_ref, kn_ref, vt_ref, ika_ref, ikb_ref, o_ref,
                key_ref, thr_ref, nge_ref, bias_ref, m_ref, l_ref, acc_ref, sa_ref, sb_ref, p_ref):
    j = pl.program_id(1)
    nck = j // (KC // QB) + 1
    seq = kn_ref.shape[0]

    iwt = iwt_ref[...]
    q_pos = j * QB + lax.broadcasted_iota(I32, (SUB, QB), 1)
    n_grp = IDX_HEADS // 2

    def score_chunk(c, carry):
        for hf in range(KC // SUB):
            start = pl.multiple_of(c * KC + hf * SUB, SUB)
            ka = ika_ref[pl.ds(start, SUB), :]
            kb = ikb_ref[pl.ds(start, SUB), :]
            acc = jnp.zeros((SUB, QB), F32)
            for a in range(n_grp // 2):
                b = a + n_grp // 2
                iq2 = jnp.concatenate([iq_ref[:, a * LANES:(a + 1) * LANES],
                                       iq_ref[:, b * LANES:(b + 1) * LANES]], axis=0)
                sa = _dot_nt(ka, iq2)
                sb = _dot_nt(kb, iq2)
                acc = acc + jnp.maximum(sa[:, :QB], 0.0) * iwt[2 * a:2 * a + 1, :]
                acc = acc + jnp.maximum(sb[:, :QB], 0.0) * iwt[2 * a + 1:2 * a + 2, :]
                acc = acc + jnp.maximum(sa[:, QB:], 0.0) * iwt[2 * b:2 * b + 1, :]
                acc = acc + jnp.maximum(sb[:, QB:], 0.0) * iwt[2 * b + 1:2 * b + 2, :]
            k_pos = start + lax.broadcasted_iota(I32, (SUB, QB), 0)
            acc = jnp.where(k_pos <= q_pos, acc, -jnp.inf)
            key_ref[c, hf * SUB:(hf + 1) * SUB, :] = _sortable(acc)
        return carry

    lax.fori_loop(0, nck, score_chunk, 0)

    n_loaded = jnp.broadcast_to(nck * KC, (SUBLANES, QB)).astype(I32)

    def search(nc):
        def search_pass(p, carry):
            tu, n_at = carry
            cand_u = tu | lax.shift_left(jnp.int32(1), 31 - p)
            cand = cand_u ^ INT_MIN
            parts = [jnp.zeros((SUBLANES, QB), I32)] * N_PART
            for c in range(nc):
                for r in range(KC // SUBLANES):
                    k = key_ref[c, r * SUBLANES:(r + 1) * SUBLANES, :]
                    parts[r % N_PART] = parts[r % N_PART] + jnp.where(k >= cand, 1, 0)
            tot = (parts[0] + parts[1]) + (parts[2] + parts[3])
            n = jnp.broadcast_to(jnp.sum(tot, axis=0, keepdims=True), (SUBLANES, QB))
            ok = n >= TOPK
            return jnp.where(ok, cand_u, tu), jnp.where(ok, n, n_at)

        tu, n_ge = lax.fori_loop(0, 32, search_pass, (jnp.zeros((SUBLANES, QB), I32), n_loaded))
        thr_ref[...] = tu ^ INT_MIN
        nge_ref[...] = n_ge

    for nc in range(1, key_ref.shape[0] + 1):
        pl.when(nck == nc)(functools.partial(search, nc))
    thr = thr_ref[...]
    n_ge = nge_ref[...]

    row_iota = lax.broadcasted_iota(I32, (SUBLANES, QB), 0)
    tied = (n_ge > TOPK) & (thr > KEY_NEG_INF)
    any_tied = jnp.max(jnp.where(tied, 1, 0)) > 0

    def count32(pred):
        def chunk(c, parts):
            parts = list(parts)
            for r in range(KC // SUBLANES):
                k = key_ref[c, r * SUBLANES:(r + 1) * SUBLANES, :]
                hit = pred(k, c * KC + r * SUBLANES)
                parts[r % N_PART] = parts[r % N_PART] + jnp.where(hit, 1, 0)
            return tuple(parts)

        zero = jnp.zeros((SUBLANES, QB), I32)
        parts = lax.fori_loop(0, nck, chunk, (zero,) * N_PART)
        tot = (parts[0] + parts[1]) + (parts[2] + parts[3])
        return jnp.broadcast_to(jnp.sum(tot, axis=0, keepdims=True), (SUBLANES, QB))

    thr_fin = jnp.maximum(thr, KEY_NEG_INF + 1)

    @pl.when(jnp.logical_not(any_tied))
    def _():
        def bias_chunk(c, carry):
            for r in range(KC // SUBLANES):
                sl = slice(r * SUBLANES, (r + 1) * SUBLANES)
                k = key_ref[c, sl, :]
                sel = (k >= thr_fin) & (k < KEY_POS_INF)
                bias_ref[c, sl, :] = jnp.where(sel, 0.0, NEG_BIG)
            return carry

        lax.fori_loop(0, nck, bias_chunk, 0)

    @pl.when(any_tied)
    def _():
        need = TOPK - count32(lambda k, _: k > thr)

        def pos_pass(p, x):
            cand = x | lax.shift_left(jnp.int32(1), (seq.bit_length() - 2) - p)
            n = count32(lambda k, r0: (k == thr) & (row_iota + r0 < cand))
            return jnp.where(n < need, cand, x)

        xlim = lax.fori_loop(0, seq.bit_length() - 1, pos_pass, jnp.zeros((SUBLANES, QB), I32))

        def bias_chunk(c, carry):
            for r in range(KC // SUBLANES):
                sl = slice(r * SUBLANES, (r + 1) * SUBLANES)
                k = key_ref[c, sl, :]
                pos = row_iota + (c * KC + r * SUBLANES)
                sel = (k > thr) | ((k == thr) & (pos <= xlim))
                sel = sel & (k >= thr_fin) & (k < KEY_POS_INF)
                bias_ref[c, sl, :] = jnp.where(sel, 0.0, NEG_BIG)
            return carry

        lax.fori_loop(0, nck, bias_chunk, 0)

    m_ref[...] = jnp.full(m_ref.shape, NEG_BIG, F32)
    l_ref[...] = jnp.zeros(l_ref.shape, F32)
    acc_ref[...] = jnp.zeros(acc_ref.shape, F32)
    group = N_HEADS // N_KV_HEADS

    def logits(c, s_ref):
        col_max = []
        for pr in range(N_PAIRS):
            g = (2 * pr) // group
            q2 = jnp.concatenate([qn_ref[:, (2 * pr) * HEAD_DIM:(2 * pr + 1) * HEAD_DIM],
                                  qn_ref[:, (2 * pr + 1) * HEAD_DIM:(2 * pr + 2) * HEAD_DIM]],
                                 axis=0)
            mx = jnp.full((SUBLANES, 2 * QB), NEG_BIG, F32)
            for hf in range(KC // SUB):
                start = pl.multiple_of(c * KC + hf * SUB, SUB)
                kc = kn_ref[pl.ds(start, SUB), g * HEAD_DIM:(g + 1) * HEAD_DIM]
                bias = bias_ref[c, hf * SUB:(hf + 1) * SUB, :]
                s = _dot_nt(kc, q2) + jnp.concatenate([bias, bias], axis=1)
                s_ref[pr, hf * SUB:(hf + 1) * SUB, :] = s
                for r in range(SUB // SUBLANES):
                    mx = jnp.maximum(mx, s[r * SUBLANES:(r + 1) * SUBLANES, :])
            col_max.append(jnp.broadcast_to(jnp.max(mx, axis=0, keepdims=True), (SUBLANES, 2 * QB)))
        return tuple(col_max)

    def finish(c, s_ref, col_max):
        alphas = []
        for pr in range(N_PAIRS):
            m_prev = m_ref[pr]
            m_new = jnp.maximum(m_prev, col_max[pr])
            alpha = jnp.exp2(m_prev - m_new)
            lsum = jnp.zeros((SUBLANES, 2 * QB), F32)
            for t in range(KC // SLAB):
                rows = slice(t * SLAB, (t + 1) * SLAB)
                p = jnp.exp2(s_ref[pr, rows, :] - m_new[0:1, :])
                for r in range(SLAB // SUBLANES):
                    lsum = lsum + p[r * SUBLANES:(r + 1) * SUBLANES, :]
                p_ref[pr, rows, :] = p.astype(BF16)
            m_ref[pr] = m_new
            l_ref[pr] = alpha * l_ref[pr] + jnp.sum(lsum, axis=0, keepdims=True)
            alphas.append(alpha)
        for pr in range(N_PAIRS):
            g = (2 * pr) // group
            vt = vt_ref[c, g * HEAD_DIM:(g + 1) * HEAD_DIM, :]
            acc_ref[pr] = acc_ref[pr] * alphas[pr][0:1, :] + _dot(vt, p_ref[pr])

    def attn_step(i, col_max):
        c = 2 * i
        mid = logits(c + 1, sb_ref)
        finish(c, sa_ref, col_max)
        nxt = logits(c + 2, sa_ref)
        finish(c + 1, sb_ref, mid)
        return nxt

    n_steps = (nck - 1) // 2
    tail = 2 * n_steps
    col_max = lax.fori_loop(0, n_steps, attn_step, logits(0, sa_ref))

    @pl.when(tail == nck - 1)
    def _():
        finish(tail, sa_ref, col_max)

    @pl.when(tail != nck - 1)
    def _():
        mid = logits(tail + 1, sb_ref)
        finish(tail, sa_ref, col_max)
        finish(tail + 1, sb_ref, mid)

    for pr in range(N_PAIRS):
        o_t = acc_ref[pr] / l_ref[pr][0:1, :]
        for i in range(2):
            h = 2 * pr + i
            o_ref[:, h * HEAD_DIM:(h + 1) * HEAD_DIM] = o_t[:, i * QB:(i + 1) * QB].T.astype(BF16)


def _dsa(qn, iqr, iwt, kn, vt, ika, ikb, batch, seq):
    t = qn.shape[0]
    nb = seq // QB
    nchunk = seq // KC
    qmap = lambda b, j: (b * nb + j, 0)
    bmap = lambda b, j: (b, 0)
    return pl.pallas_call(
        _dsa_kernel,
        grid=(batch, nb),
        in_specs=[
            pl.BlockSpec((QB, ATTN_DIM), qmap),
            pl.BlockSpec((QB, ATTN_DIM), qmap),
            pl.BlockSpec((IDX_HEADS, QB), lambda b, j: (0, b * nb + j)),
            pl.BlockSpec((seq, KV_DIM), bmap),
            pl.BlockSpec((nchunk, KV_DIM, KC), lambda b, j: (b, 0, 0)),
            pl.BlockSpec((seq, LANES), bmap),
            pl.BlockSpec((seq, LANES), bmap),
        ],
        out_specs=pl.BlockSpec((QB, ATTN_DIM), qmap),
        out_shape=jax.ShapeDtypeStruct((t, ATTN_DIM), BF16),
        scratch_shapes=[
            pltpu.VMEM((nchunk, KC, QB), I32),
            pltpu.VMEM((SUBLANES, QB), I32),
            pltpu.VMEM((SUBLANES, QB), I32),
            pltpu.VMEM((nchunk, KC, QB), F32),
            pltpu.VMEM((N_PAIRS, SUBLANES, 2 * QB), F32),
            pltpu.VMEM((N_PAIRS, SUBLANES, 2 * QB), F32),
            pltpu.VMEM((N_PAIRS, HEAD_DIM, 2 * QB), F32),
            pltpu.VMEM((N_PAIRS, KC, 2 * QB), F32),
            pltpu.VMEM((N_PAIRS, KC, 2 * QB), F32),
            pltpu.VMEM((N_PAIRS, KC, 2 * QB), BF16),
        ],
        compiler_params=_cparams(("parallel", "arbitrary")),
        name="dsa",
    )(qn, iqr, iwt, kn, vt, ika, ikb)


def _mem_kv_kernel(mem_ref, g_ref, w_ref, kg_ref, mk_ref, mv_ref):
    m = _rms(mem_ref[...], g_ref[...]).astype(BF16)
    kv = _dot(m, w_ref[...])
    for h in range(MEM_HEADS):
        sl = slice(h * HEAD_DIM, (h + 1) * HEAD_DIM)
        mk_ref[:, sl] = _rms(kv[:, sl], kg_ref[...]).astype(BF16)
    mv_ref[...] = kv[:, MEM_DIM:].astype(BF16)


def _mem_kv(mem2d, g, w, kg, batch):
    n = mem2d.shape[0]
    m = n // batch
    out = jax.ShapeDtypeStruct((n, MEM_DIM), BF16)
    ospec = pl.BlockSpec((m, MEM_DIM), lambda b: (b, 0))
    return pl.pallas_call(
        _mem_kv_kernel,
        grid=(batch,),
        in_specs=[
            pl.BlockSpec((m, D_MODEL), lambda b: (b, 0)),
            pl.BlockSpec((1, D_MODEL), lambda b: (0, 0)),
            pl.BlockSpec((D_MODEL, 2 * MEM_DIM), lambda b: (0, 0)),
            pl.BlockSpec((1, HEAD_DIM), lambda b: (0, 0)),
        ],
        out_specs=[ospec, ospec],
        out_shape=[out, out],
        compiler_params=_cparams(("parallel",)),
        name="mem_kv",
    )(mem2d, g, w, kg)


def _mem_attn_kernel(q_ref, mk_ref, mv_ref, qg_ref, o_ref):
    scale = HEAD_DIM ** -0.5
    for h in range(MEM_HEADS):
        sl = slice(h * HEAD_DIM, (h + 1) * HEAD_DIM)
        qh = (_rms(q_ref[:, sl].astype(F32), qg_ref[...]) * scale).astype(BF16)
        s = _dot_nt(qh, mk_ref[:, sl])
        p = jnp.exp(s - jnp.max(s, axis=1, keepdims=True))
        l = jnp.sum(p, axis=1, keepdims=True)
        o = _dot((p / l).astype(BF16), mv_ref[:, sl])
        o_ref[:, sl] = o.astype(BF16)


def _mem_attn(p, mk, mv, qg, batch, seq):
    t = p.shape[0]
    tq = 512
    nq = seq // tq
    m = mk.shape[0] // batch
    return pl.pallas_call(
        _mem_attn_kernel,
        grid=(batch, nq),
        in_specs=[
            pl.BlockSpec((tq, MEM_DIM), lambda b, i: (b * nq + i, P_MQ // MEM_DIM)),
            pl.BlockSpec((m, MEM_DIM), lambda b, i: (b, 0)),
            pl.BlockSpec((m, MEM_DIM), lambda b, i: (b, 0)),
            pl.BlockSpec((1, HEAD_DIM), lambda b, i: (0, 0)),
        ],
        out_specs=pl.BlockSpec((tq, MEM_DIM), lambda b, i: (b * nq + i, 0)),
        out_shape=jax.ShapeDtypeStruct((t, MEM_DIM), BF16),
        compiler_params=_cparams(("parallel", "parallel")),
        name="mem_attn",
    )(p, mk, mv, qg)


CONV_HALO = 32
CONV_ROWS = 64


def _conv_kernel(cur_a_ref, cur_g_ref, prev_a_ref, prev_g_ref, inb_ref, cw_ref, cb_ref, lg_ref,
                 lb_ref, o_ref, u_ref, sh_ref):
    ts = cur_a_ref.shape[0]
    n = CONV_HALO + ts

    def glu(a, g):
        a = a.astype(F32) + inb_ref[:, :CONV_CH]
        g = g.astype(F32) + inb_ref[:, CONV_CH:]
        return a * jax.nn.sigmoid(g)

    u_prev = glu(prev_a_ref[ts - CONV_HALO:, :], prev_g_ref[ts - CONV_HALO:, :])
    u_ref[:CONV_HALO, :] = jnp.where(pl.program_id(1) == 0, 0.0, u_prev)
    u_ref[CONV_HALO:n, :] = glu(cur_a_ref[...], cur_g_ref[...])
    u_ref[n:, :] = jnp.zeros((SUBLANES, CONV_CH), F32)
    for r in range(1, SUBLANES):
        sh_ref[r - 1] = u_ref[r:r + n, :]

    first_tap = CONV_HALO - (CONV_WIDTH - 1)
    for t in range(ts // CONV_ROWS):
        base = t * CONV_ROWS
        y = jnp.broadcast_to(cb_ref[...], (CONV_ROWS, CONV_CH))
        for w in range(CONV_WIDTH):
            r = (first_tap + w) % SUBLANES
            a = base + first_tap + w - r
            rows = u_ref[a:a + CONV_ROWS, :] if r == 0 else sh_ref[r - 1, a:a + CONV_ROWS, :]
            y = y + rows * cw_ref[w:w + 1, :]
        mu = jnp.mean(y, axis=-1, keepdims=True)
        d = y - mu
        var = jnp.mean(d * d, axis=-1, keepdims=True)
        z = d * lax.rsqrt(var + EPS) * lg_ref[...] + lb_ref[...]
        o_ref[base:base + CONV_ROWS, :] = (z * jax.nn.sigmoid(z)).astype(BF16)


def _conv(p, inb, cw, cb, lg, lb, batch, seq):
    t = p.shape[0]
    ts = 512
    ns = seq // ts
    a_blk = P_GLU // CONV_CH
    vec = lambda w: pl.BlockSpec((1, w), lambda b, i: (0, 0))
    cur = lambda k: pl.BlockSpec((ts, CONV_CH), lambda b, i: (b * ns + i, a_blk + k))
    prev = lambda k: pl.BlockSpec((ts, CONV_CH), lambda b, i: (b * ns + jnp.maximum(i - 1, 0), a_blk + k))
    return pl.pallas_call(
        _conv_kernel,
        grid=(batch, ns),
        in_specs=[
            cur(0), cur(1), prev(0), prev(1),
            vec(2 * CONV_CH),
            pl.BlockSpec((CONV_WIDTH, CONV_CH), lambda b, i: (0, 0)),
            vec(CONV_CH), vec(CONV_CH), vec(CONV_CH),
        ],
        out_specs=pl.BlockSpec((ts, CONV_CH), lambda b, i: (b * ns + i, 0)),
        out_shape=jax.ShapeDtypeStruct((t, CONV_CH), BF16),
        scratch_shapes=[pltpu.VMEM((CONV_HALO + ts + SUBLANES, CONV_CH), F32),
                        pltpu.VMEM((SUBLANES - 1, CONV_HALO + ts, CONV_CH), F32)],
        compiler_params=_cparams(("parallel", "parallel")),
        name="conv",
    )(p, p, p, p, inb, cw, cb, lg, lb)


MERGE_TN = 512


def _merge_kernel(a_ref, m_ref, c_ref, g0_ref, g1_ref, g2_ref, gb0_ref, gb1_ref, gb2_ref,
                  wa_ref, wm_ref, wc_ref, cob_ref, o_ref):
    a, m, c = a_ref[...], m_ref[...], c_ref[...]
    for n in range(D_MODEL // MERGE_TN):
        sl = slice(n * MERGE_TN, (n + 1) * MERGE_TN)

        def gate(g_ref, gb_ref):
            return jax.nn.sigmoid(g_ref[:, sl].astype(F32) + gb_ref[:, sl])

        y = gate(g0_ref, gb0_ref) * _dot(a, wa_ref[:, sl])
        y = y + gate(g1_ref, gb1_ref) * _dot(m, wm_ref[:, sl])
        y = y + gate(g2_ref, gb2_ref) * (_dot(c, wc_ref[:, sl]) + cob_ref[:, sl])
        o_ref[:, sl] = y.astype(BF16)


def _merge(attn, memo, convo, p, gate_b, wa, wm, wc, cob):
    t = attn.shape[0]
    tm = 512
    gblk = P_GATES // D_MODEL
    row = lambda w: pl.BlockSpec((tm, w), lambda i: (i, 0))
    full = lambda r, c: pl.BlockSpec((r, c), lambda i: (0, 0))
    gspec = lambda k: pl.BlockSpec((tm, D_MODEL), lambda i: (i, gblk + k))
    gbspec = lambda k: pl.BlockSpec((1, D_MODEL), lambda i: (0, k))
    return pl.pallas_call(
        _merge_kernel,
        grid=(t // tm,),
        in_specs=[row(ATTN_DIM), row(MEM_DIM), row(CONV_CH),
                  gspec(0), gspec(1), gspec(2), gbspec(0), gbspec(1), gbspec(2),
                  full(ATTN_DIM, D_MODEL), full(MEM_DIM, D_MODEL), full(CONV_CH, D_MODEL),
                  full(1, D_MODEL)],
        out_specs=row(D_MODEL),
        out_shape=jax.ShapeDtypeStruct((t, D_MODEL), BF16),
        compiler_params=_cparams(("parallel",)),
        name="merge",
    )(attn, memo, convo, p, p, p, gate_b, gate_b, gate_b, wa, wm, wc, cob)


def _outproj_kernel(x_ref, mg_ref, w_ref, g_ref, x1_ref, h2_ref):
    x1 = x_ref[...] + _dot(mg_ref[...], w_ref[...])
    x1_ref[...] = x1
    h2_ref[...] = _rms(x1, g_ref[...]).astype(BF16)


def _outproj(x, merged, w, g):
    t = x.shape[0]
    tm = 512
    row = pl.BlockSpec((tm, D_MODEL), lambda i: (i, 0))
    return pl.pallas_call(
        _outproj_kernel,
        grid=(t // tm,),
        in_specs=[row, row,
                  pl.BlockSpec((D_MODEL, D_MODEL), lambda i: (0, 0)),
                  pl.BlockSpec((1, D_MODEL), lambda i: (0, 0))],
        out_specs=[row, row],
        out_shape=[jax.ShapeDtypeStruct((t, D_MODEL), F32), jax.ShapeDtypeStruct((t, D_MODEL), BF16)],
        compiler_params=_cparams(("parallel",)),
        name="outproj",
    )(x, merged, w, g)


def _mlp_kernel(h_ref, x_ref, wu_ref, wd_ref, o_ref):
    @pl.when(pl.program_id(1) == 0)
    def _():
        o_ref[...] = x_ref[...]

    u = jnp.maximum(_dot(h_ref[...], wu_ref[...]), 0.0)
    o_ref[...] += _dot((u * u).astype(BF16), wd_ref[...])


def _mlp(h2, x1, wu, wd):
    t = x1.shape[0]
    tm, tf = 512, 1024
    return pl.pallas_call(
        _mlp_kernel,
        grid=(t // tm, FFN_DIM // tf),
        in_specs=[
            pl.BlockSpec((tm, D_MODEL), lambda i, f: (i, 0)),
            pl.BlockSpec((tm, D_MODEL), lambda i, f: (i, 0)),
            pl.BlockSpec((D_MODEL, tf), lambda i, f: (0, f)),
            pl.BlockSpec((tf, D_MODEL), lambda i, f: (f, 0)),
        ],
        out_specs=pl.BlockSpec((tm, D_MODEL), lambda i, f: (i, 0)),
        out_shape=jax.ShapeDtypeStruct((t, D_MODEL), F32),
        compiler_params=_cparams(("parallel", "arbitrary")),
        name="mlp",
    )(h2, x1, wu, wd)


def _split_w_in(w):
    narrow = IDX_DIM + IDX_HEADS
    head = w[:, :P_HEAD].astype(BF16)
    tail = w[:, P_HEAD + narrow:].astype(BF16)
    small = jnp.pad(w[:, P_HEAD:P_HEAD + narrow], ((0, 0), (0, PS_WIDTH - narrow))).astype(BF16)
    return head, tail, small


def kernel(x, mem, positions, norm1_g, w_in, q_norm_g, k_norm_g, mem_norm_g, w_mem_kv, mq_norm_g,
           mk_norm_g, conv_in_b, conv_w, conv_b, conv_ln_g, conv_ln_b, gate_b, w_attn_o, w_mem_o,
           w_conv_o, conv_o_b, w_out, norm2_g, w_up, w_down):
    batch, seq, _ = x.shape
    depth = w_in.shape[0]
    t = batch * seq
    xf = x.reshape(t, D_MODEL)
    mem2d = mem.reshape(batch * mem.shape[1], D_MODEL)
    tabs = _rope_tables(positions)
    row = lambda v: v.reshape(1, -1)

    for l in range(depth):
        w_head, w_tail, w_small = _split_w_in(w_in[l])
        p, ps = _proj(xf, row(norm1_g[l]), w_head, w_tail, w_small)
        qn, iqr, kn, vt, ika, ikb, iwt = _prep(p, ps, tabs, row(q_norm_g[l]), row(k_norm_g[l]))
        attn = _dsa(qn, iqr, iwt, kn, vt, ika, ikb, batch, seq)
        mk, mv = _mem_kv(mem2d, row(mem_norm_g[l]), w_mem_kv[l].astype(BF16), row(mk_norm_g[l]), batch)
        memo = _mem_attn(p, mk, mv, row(mq_norm_g[l]), batch, seq)
        convo = _conv(p, row(conv_in_b[l]), conv_w[l], row(conv_b[l]), row(conv_ln_g[l]),
                      row(conv_ln_b[l]), batch, seq)
        merged = _merge(attn, memo, convo, p, row(gate_b[l]), w_attn_o[l].astype(BF16),
                        w_mem_o[l].astype(BF16), w_conv_o[l].astype(BF16), row(conv_o_b[l]))
        x1, h2 = _outproj(xf, merged, w_out[l].astype(BF16), row(norm2_g[l]))
        xf = _mlp(h2, x1, w_up[l].astype(BF16), w_down[l].astype(BF16))
    return xf.reshape(batch, seq, D_MODEL)
```

```python
import functools
import math

import jax
import jax.numpy as jnp
from jax import lax
from jax.experimental import pallas as pl
from jax.experimental.pallas import tpu as pltpu

F32 = jnp.float32
BF16 = jnp.bfloat16
I32 = jnp.int32

D_MODEL = 2048
HEAD_DIM = 128
N_HEADS = 8
N_KV_HEADS = 2
IDX_HEADS = 16
IDX_DIM = 64
TOPK = 256
CONV_CH = 512
CONV_WIDTH = 31
MEM_HEADS = 4
FFN_DIM = 4 * D_MODEL
ROPE_THETA = 500000.0
N_BRANCH = 3
EPS = 1e-6
ATTN_DIM = N_HEADS * HEAD_DIM
KV_DIM = N_KV_HEADS * HEAD_DIM
MEM_DIM = MEM_HEADS * HEAD_DIM
HEAD_ROT_HALF = HEAD_DIM // 8
IDX_ROT_HALF = IDX_DIM // 8

LANES = 128
VMEM_LIMIT = 56 * 1024 * 1024

P_Q = 0
P_K = P_Q + ATTN_DIM
P_V = P_K + KV_DIM
P_IQ = P_V + KV_DIM
P_HEAD = P_IQ + IDX_HEADS * IDX_DIM
P_GLU = P_HEAD
P_MQ = P_GLU + 2 * CONV_CH
P_GATES = P_MQ + MEM_DIM
P_WIDTH = P_GATES + N_BRANCH * D_MODEL
P_TAIL = P_WIDTH - P_HEAD
PS_WIDTH = LANES

QB = 128
KC = 512
SUB = 256
SLAB = 32
SUBLANES = 8
LOG2E = 1.4426950408889634
NEG_BIG = -1e30
INT_MIN = -2 ** 31
KEY_NEG_INF = -2139095041
KEY_POS_INF = 2139095040


def _cparams(sem, vmem=VMEM_LIMIT):
    return pltpu.CompilerParams(dimension_semantics=sem, vmem_limit_bytes=vmem)


def _lp(l, shape, imap):
    return pl.BlockSpec((pl.Squeezed(),) + shape, lambda *g: (l,) + imap(*g))


def _dot(a, b):
    return jnp.dot(a, b, preferred_element_type=F32)


def _dot_nt(a, b):
    return lax.dot_general(a, b, (((1,), (1,)), ((), ())), preferred_element_type=F32)


def _rms(xf, g):
    return xf * lax.rsqrt(jnp.mean(xf * xf, axis=-1, keepdims=True) + EPS) * g


def _rope_table_kernel(pos_ref, ch_ref, sh1_ref, sh2_ref, ci_ref, si1_ref, si2_ref):
    pos = pos_ref[...].astype(F32)
    lane = lax.broadcasted_iota(I32, (1, LANES), 1)

    def tables(period, half, c_ref, s1_ref, s2_ref):
        r = lane & (period - 1)
        fi = (r & (half - 1)).astype(F32)
        inv = jnp.exp(fi * (-math.log(ROPE_THETA) / half))
        ang = pos * inv
        c = jnp.cos(ang)
        s = jnp.sin(ang)
        first = r < half
        second = (r >= half) & (r < 2 * half)
        c_ref[...] = jnp.where(first | second, c, 1.0)
        s1_ref[...] = jnp.where(second, s, 0.0)
        s2_ref[...] = jnp.where(first, -s, 0.0)

    tables(HEAD_DIM, HEAD_ROT_HALF, ch_ref, sh1_ref, sh2_ref)
    tables(IDX_DIM, IDX_ROT_HALF, ci_ref, si1_ref, si2_ref)


def _rope_tables(positions):
    t = positions.size
    tp = 512
    tab = jax.ShapeDtypeStruct((t, LANES), F32)
    spec = pl.BlockSpec((tp, LANES), lambda i: (i, 0))
    return pl.pallas_call(
        _rope_table_kernel,
        grid=(t // tp,),
        in_specs=[pl.BlockSpec((tp, 1), lambda i: (i, 0))],
        out_specs=[spec] * 6,
        out_shape=[tab] * 6,
        compiler_params=_cparams(("parallel",)),
        name="rope_tables",
    )(positions.reshape(t, 1))


def _rope(t, c, s1, s2, half):
    return t * c + pltpu.roll(t, half, 1) * s1 + pltpu.roll(t, LANES - half, 1) * s2


PROJ_TN = 1280


def _proj_kernel(x_ref, g_ref, wh_ref, wt_ref, ws_ref, o_ref, os_ref, h_ref):
    j = pl.program_id(1)

    @pl.when(j == 0)
    def _():
        h = _rms(x_ref[...], g_ref[...]).astype(BF16)
        h_ref[...] = h
        os_ref[...] = _dot(h, ws_ref[...])

    @pl.when(j < P_HEAD // PROJ_TN)
    def _():
        o_ref[...] = _dot(h_ref[...], wh_ref[...]).astype(o_ref.dtype)

    @pl.when(j >= P_HEAD // PROJ_TN)
    def _():
        o_ref[...] = _dot(h_ref[...], wt_ref[...]).astype(o_ref.dtype)


def _proj(x, g, wh, wt, ws, l):
    t = x.shape[0]
    tm, tn = 1024, PROJ_TN
    n_head = P_HEAD // tn
    return pl.pallas_call(
        _proj_kernel,
        grid=(t // tm, P_WIDTH // tn),
        in_specs=[
            pl.BlockSpec((tm, D_MODEL), lambda i, j: (i, 0)),
            _lp(l, (1, D_MODEL), lambda i, j: (0, 0)),
            _lp(l, (D_MODEL, tn), lambda i, j: (0, jnp.minimum(j, n_head - 1))),
            _lp(l, (D_MODEL, tn), lambda i, j: (0, jnp.maximum(j - n_head, 0))),
            _lp(l, (D_MODEL, PS_WIDTH), lambda i, j: (0, 0)),
        ],
        out_specs=[
            pl.BlockSpec((tm, tn), lambda i, j: (i, j)),
            pl.BlockSpec((tm, PS_WIDTH), lambda i, j: (i, 0)),
        ],
        out_shape=[
            jax.ShapeDtypeStruct((t, P_WIDTH), BF16),
            jax.ShapeDtypeStruct((t, PS_WIDTH), F32),
        ],
        scratch_shapes=[pltpu.VMEM((tm, D_MODEL), BF16)],
        compiler_params=_cparams(("parallel", "arbitrary")),
        name="proj",
    )(x, g, wh, wt, ws)


def _prep_kernel(q_ref, iql_ref, iqh_ref, kv_ref, ps_ref, ch_ref, sh1_ref, sh2_ref, ci_ref, si1_ref,
                 si2_ref, qg_ref, kg_ref, qn_ref, iqr_ref, kn_ref, vt_ref, ika_ref, ikb_ref,
                 iwt_ref):
    ch, sh1, sh2 = ch_ref[...], sh1_ref[...], sh2_ref[...]
    ci, si1, si2 = ci_ref[...], si1_ref[...], si2_ref[...]
    scale = (HEAD_DIM ** -0.5) * LOG2E
    half_groups = N_HEADS // 2
    for h in range(N_HEADS):
        sl = slice(h * HEAD_DIM, (h + 1) * HEAD_DIM)
        qh = _rms(q_ref[:, sl].astype(F32), qg_ref[...])
        qn_ref[:, sl] = (_rope(qh, ch, sh1, sh2, HEAD_ROT_HALF) * scale).astype(BF16)
        src = iql_ref if h < half_groups else iqh_ref
        hs = h % half_groups
        iqh = src[:, hs * LANES:(hs + 1) * LANES].astype(F32)
        iqr_ref[:, sl] = _rope(iqh, ci, si1, si2, IDX_ROT_HALF).astype(BF16)
    for g in range(N_KV_HEADS):
        sl = slice(g * HEAD_DIM, (g + 1) * HEAD_DIM)
        kh = _rms(kv_ref[:, sl].astype(F32), kg_ref[...])
        kn_ref[:, sl] = _rope(kh, ch, sh1, sh2, HEAD_ROT_HALF).astype(BF16)
        vh = kv_ref[:, KV_DIM + g * HEAD_DIM:KV_DIM + (g + 1) * HEAD_DIM].astype(F32)
        vt_ref[0, sl, :] = vh.T.astype(BF16)
    ps = ps_ref[...]
    lane = lax.broadcasted_iota(I32, (1, LANES), 1)
    ikr = jnp.where(lane < IDX_DIM, _rope(ps, ci, si1, si2, IDX_ROT_HALF), 0.0)
    ika_ref[...] = ikr.astype(BF16)
    ikb_ref[...] = pltpu.roll(ikr, IDX_DIM, 1).astype(BF16)
    idx_scale = (IDX_DIM ** -0.5) * (IDX_HEADS ** -0.5)
    iws = jnp.where(lane < IDX_HEADS, pltpu.roll(ps, LANES - IDX_DIM, 1) * idx_scale, 0.0)
    iwt_ref[...] = iws.T


def _prep(p, ps, tabs, qg, kg, l):
    t = p.shape[0]
    tp = KC
    half_iq = IDX_HEADS * IDX_DIM // 2
    tab_spec = pl.BlockSpec((tp, LANES), lambda i: (i, 0))
    vec_spec = _lp(l, (1, HEAD_DIM), lambda i: (0, 0))

    def out(width, dtype):
        return (pl.BlockSpec((tp, width), lambda i: (i, 0)), jax.ShapeDtypeStruct((t, width), dtype))

    outs = [out(ATTN_DIM, BF16), out(ATTN_DIM, BF16), out(KV_DIM, BF16),
            (pl.BlockSpec((1, KV_DIM, tp), lambda i: (i, 0, 0)),
             jax.ShapeDtypeStruct((t // tp, KV_DIM, tp), BF16)),
            out(LANES, BF16), out(LANES, BF16),
            (pl.BlockSpec((LANES, tp), lambda i: (0, i)), jax.ShapeDtypeStruct((LANES, t), F32))]
    return pl.pallas_call(
        _prep_kernel,
        grid=(t // tp,),
        in_specs=[
            pl.BlockSpec((tp, ATTN_DIM), lambda i: (i, P_Q // ATTN_DIM)),
            pl.BlockSpec((tp, half_iq), lambda i: (i, P_IQ // half_iq)),
            pl.BlockSpec((tp, half_iq), lambda i: (i, P_IQ // half_iq + 1)),
            pl.BlockSpec((tp, 2 * KV_DIM), lambda i: (i, P_K // (2 * KV_DIM))),
            tab_spec,
        ] + [tab_spec] * 6 + [vec_spec, vec_spec],
        out_specs=[o[0] for o in outs],
        out_shape=[o[1] for o in outs],
        compiler_params=_cparams(("parallel",)),
        name="prep",
    )(p, p, p, p, ps, *tabs, qg, kg)


def _sortable(x):
    bits = lax.bitcast_convert_type(x, I32)
    return bits ^ ((bits >> 31) & 0x7FFFFFFF)


N_PAIRS = N_HEADS // 2
N_PART = 4


def _dsa_kernel(qn_ref, iq_ref, iwt_ref, kn_ref, vt_ref, ika_ref, ikb_ref, o_ref,
                key_ref, thr_ref, nge_ref, bias_ref, m_ref, l_ref, acc_ref, sa_ref, sb_ref, p_ref):
    j = pl.program_id(1)
    nck = j // (KC // QB) + 1
    seq = kn_ref.shape[0]

    iwt = iwt_ref[...]
    q_pos = j * QB + lax.broadcasted_iota(I32, (SUB, QB), 1)
    n_grp = IDX_HEADS // 2

    def score_chunk(c, carry):
        for hf in range(KC // SUB):
            start = pl.multiple_of(c * KC + hf * SUB, SUB)
            ka = ika_ref[pl.ds(start, SUB), :]
            kb = ikb_ref[pl.ds(start, SUB), :]
            acc = jnp.zeros((SUB, QB), F32)
            for a in range(n_grp // 2):
                b = a + n_grp // 2
                iq2 = jnp.concatenate([iq_ref[:, a * LANES:(a + 1) * LANES],
                                       iq_ref[:, b * LANES:(b + 1) * LANES]], axis=0)
                sa = _dot_nt(ka, iq2)
                sb = _dot_nt(kb, iq2)
                acc = acc + jnp.maximum(sa[:, :QB], 0.0) * iwt[2 * a:2 * a + 1, :]
                acc = acc + jnp.maximum(sb[:, :QB], 0.0) * iwt[2 * a + 1:2 * a + 2, :]
                acc = acc + jnp.maximum(sa[:, QB:], 0.0) * iwt[2 * b:2 * b + 1, :]
                acc = acc + jnp.maximum(sb[:, QB:], 0.0) * iwt[2 * b + 1:2 * b + 2, :]
            k_pos = start + lax.broadcasted_iota(I32, (SUB, QB), 0)
            acc = jnp.where(k_pos <= q_pos, acc, -jnp.inf)
            key_ref[c, hf * SUB:(hf + 1) * SUB, :] = _sortable(acc)
        return carry

    lax.fori_loop(0, nck, score_chunk, 0)

    n_loaded = jnp.broadcast_to(nck * KC, (SUBLANES, QB)).astype(I32)

    def search(nc):
        def search_pass(p, carry):
            tu, n_at = carry
            cand_u = tu | lax.shift_left(jnp.int32(1), 31 - p)
            cand = cand_u ^ INT_MIN
            parts = [jnp.zeros((SUBLANES, QB), I32)] * N_PART
            for c in range(nc):
                for r in range(KC // SUBLANES):
                    k = key_ref[c, r * SUBLANES:(r + 1) * SUBLANES, :]
                    parts[r % N_PART] = parts[r % N_PART] + jnp.where(k >= cand, 1, 0)
            tot = (parts[0] + parts[1]) + (parts[2] + parts[3])
            n = jnp.broadcast_to(jnp.sum(tot, axis=0, keepdims=True), (SUBLANES, QB))
            ok = n >= TOPK
            return jnp.where(ok, cand_u, tu), jnp.where(ok, n, n_at)

        tu, n_ge = lax.fori_loop(0, 32, search_pass, (jnp.zeros((SUBLANES, QB), I32), n_loaded))
        thr_ref[...] = tu ^ INT_MIN
        nge_ref[...] = n_ge

    for nc in range(1, key_ref.shape[0] + 1):
        pl.when(nck == nc)(functools.partial(search, nc))
    thr = thr_ref[...]
    n_ge = nge_ref[...]

    row_iota = lax.broadcasted_iota(I32, (SUBLANES, QB), 0)
    tied = (n_ge > TOPK) & (thr > KEY_NEG_INF)
    any_tied = jnp.max(jnp.where(tied, 1, 0)) > 0

    def count32(pred):
        def chunk(c, parts):
            parts = list(parts)
            for r in range(KC // SUBLANES):
                k = key_ref[c, r * SUBLANES:(r + 1) * SUBLANES, :]
                hit = pred(k, c * KC + r * SUBLANES)
                parts[r % N_PART] = parts[r % N_PART] + jnp.where(hit, 1, 0)
            return tuple(parts)

        zero = jnp.zeros((SUBLANES, QB), I32)
        parts = lax.fori_loop(0, nck, chunk, (zero,) * N_PART)
        tot = (parts[0] + parts[1]) + (parts[2] + parts[3])
        return jnp.broadcast_to(jnp.sum(tot, axis=0, keepdims=True), (SUBLANES, QB))

    thr_fin = jnp.maximum(thr, KEY_NEG_INF + 1)

    @pl.when(jnp.logical_not(any_tied))
    def _():
        def bias_chunk(c, carry):
            for r in range(KC // SUBLANES):
                sl = slice(r * SUBLANES, (r + 1) * SUBLANES)
                k = key_ref[c, sl, :]
                sel = (k >= thr_fin) & (k < KEY_POS_INF)
                bias_ref[c, sl, :] = jnp.where(sel, 0.0, NEG_BIG)
            return carry

        lax.fori_loop(0, nck, bias_chunk, 0)

    @pl.when(any_tied)
    def _():
        need = TOPK - count32(lambda k, _: k > thr)

        def pos_pass(p, x):
            cand = x | lax.shift_left(jnp.int32(1), (seq.bit_length() - 2) - p)
            n = count32(lambda k, r0: (k == thr) & (row_iota + r0 < cand))
            return jnp.where(n < need, cand, x)

        xlim = lax.fori_loop(0, seq.bit_length() - 1, pos_pass, jnp.zeros((SUBLANES, QB), I32))

        def bias_chunk(c, carry):
            for r in range(KC // SUBLANES):
                sl = slice(r * SUBLANES, (r + 1) * SUBLANES)
                k = key_ref[c, sl, :]
                pos = row_iota + (c * KC + r * SUBLANES)
                sel = (k > thr) | ((k == thr) & (pos <= xlim))
                sel = sel & (k >= thr_fin) & (k < KEY_POS_INF)
                bias_ref[c, sl, :] = jnp.where(sel, 0.0, NEG_BIG)
            return carry

        lax.fori_loop(0, nck, bias_chunk, 0)

    m_ref[...] = jnp.full(m_ref.shape, NEG_BIG, F32)
    l_ref[...] = jnp.zeros(l_ref.shape, F32)
    acc_ref[...] = jnp.zeros(acc_ref.shape, F32)
    group = N_HEADS // N_KV_HEADS

    def logits(c, s_ref):
        col_max = []
        for pr in range(N_PAIRS):
            g = (2 * pr) // group
            q2 = jnp.concatenate([qn_ref[:, (2 * pr) * HEAD_DIM:(2 * pr + 1) * HEAD_DIM],
                                  qn_ref[:, (2 * pr + 1) * HEAD_DIM:(2 * pr + 2) * HEAD_DIM]],
                                 axis=0)
            mx = jnp.full((SUBLANES, 2 * QB), NEG_BIG, F32)
            for hf in range(KC // SUB):
                start = pl.multiple_of(c * KC + hf * SUB, SUB)
                kc = kn_ref[pl.ds(start, SUB), g * HEAD_DIM:(g + 1) * HEAD_DIM]
                bias = bias_ref[c, hf * SUB:(hf + 1) * SUB, :]
                s = _dot_nt(kc, q2) + jnp.concatenate([bias, bias], axis=1)
                s_ref[pr, hf * SUB:(hf + 1) * SUB, :] = s
                for r in range(SUB // SUBLANES):
                    mx = jnp.maximum(mx, s[r * SUBLANES:(r + 1) * SUBLANES, :])
            col_max.append(jnp.broadcast_to(jnp.max(mx, axis=0, keepdims=True), (SUBLANES, 2 * QB)))
        return tuple(col_max)

    def finish(c, s_ref, col_max):
        alphas = []
        for pr in range(N_PAIRS):
            m_prev = m_ref[pr]
            m_new = jnp.maximum(m_prev, col_max[pr])
            alpha = jnp.exp2(m_prev - m_new)
            lsum = jnp.zeros((SUBLANES, 2 * QB), F32)
            for t in range(KC // SLAB):
                rows = slice(t * SLAB, (t + 1) * SLAB)
                p = jnp.exp2(s_ref[pr, rows, :] - m_new[0:1, :])
                for r in range(SLAB // SUBLANES):
                    lsum = lsum + p[r * SUBLANES:(r + 1) * SUBLANES, :]
                p_ref[pr, rows, :] = p.astype(BF16)
            m_ref[pr] = m_new
            l_ref[pr] = alpha * l_ref[pr] + jnp.sum(lsum, axis=0, keepdims=True)
            alphas.append(alpha)
        for pr in range(N_PAIRS):
            g = (2 * pr) // group
            vt = vt_ref[c, g * HEAD_DIM:(g + 1) * HEAD_DIM, :]
            acc_ref[pr] = acc_ref[pr] * alphas[pr][0:1, :] + _dot(vt, p_ref[pr])

    def attn_step(i, col_max):
        c = 2 * i
        mid = logits(c + 1, sb_ref)
        finish(c, sa_ref, col_max)
        nxt = logits(c + 2, sa_ref)
        finish(c + 1, sb_ref, mid)
        return nxt

    n_steps = (nck - 1) // 2
    tail = 2 * n_steps
    col_max = lax.fori_loop(0, n_steps, attn_step, logits(0, sa_ref))

    @pl.when(tail == nck - 1)
    def _():
        finish(tail, sa_ref, col_max)

    @pl.when(tail != nck - 1)
    def _():
        mid = logits(tail + 1, sb_ref)
        finish(tail, sa_ref, col_max)
        finish(tail + 1, sb_ref, mid)

    for pr in range(N_PAIRS):
        o_t = acc_ref[pr] / l_ref[pr][0:1, :]
        for i in range(2):
            h = 2 * pr + i
            o_ref[:, h * HEAD_DIM:(h + 1) * HEAD_DIM] = o_t[:, i * QB:(i + 1) * QB].T.astype(BF16)


def _dsa(qn, iqr, iwt, kn, vt, ika, ikb, batch, seq):
    t = qn.shape[0]
    nb = seq // QB
    nchunk = seq // KC
    qmap = lambda b, j: (b * nb + j, 0)
    bmap = lambda b, j: (b, 0)
    return pl.pallas_call(
        _dsa_kernel,
        grid=(batch, nb),
        in_specs=[
            pl.BlockSpec((QB, ATTN_DIM), qmap),
            pl.BlockSpec((QB, ATTN_DIM), qmap),
            pl.BlockSpec((IDX_HEADS, QB), lambda b, j: (0, b * nb + j)),
            pl.BlockSpec((seq, KV_DIM), bmap),
            pl.BlockSpec((nchunk, KV_DIM, KC), lambda b, j: (b, 0, 0)),
            pl.BlockSpec((seq, LANES), bmap),
            pl.BlockSpec((seq, LANES), bmap),
        ],
        out_specs=pl.BlockSpec((QB, ATTN_DIM), qmap),
        out_shape=jax.ShapeDtypeStruct((t, ATTN_DIM), BF16),
        scratch_shapes=[
            pltpu.VMEM((nchunk, KC, QB), I32),
            pltpu.VMEM((SUBLANES, QB), I32),
            pltpu.VMEM((SUBLANES, QB), I32),
            pltpu.VMEM((nchunk, KC, QB), F32),
            pltpu.VMEM((N_PAIRS, SUBLANES, 2 * QB), F32),
            pltpu.VMEM((N_PAIRS, SUBLANES, 2 * QB), F32),
            pltpu.VMEM((N_PAIRS, HEAD_DIM, 2 * QB), F32),
            pltpu.VMEM((N_PAIRS, KC, 2 * QB), F32),
            pltpu.VMEM((N_PAIRS, KC, 2 * QB), F32),
            pltpu.VMEM((N_PAIRS, KC, 2 * QB), BF16),
        ],
        compiler_params=_cparams(("parallel", "arbitrary")),
        name="dsa",
    )(qn, iqr, iwt, kn, vt, ika, ikb)


def _mem_kv_kernel(mem_ref, g_ref, w_ref, kg_ref, mk_ref, mv_ref):
    m = _rms(mem_ref[...], g_ref[...]).astype(BF16)
    kv = _dot(m, w_ref[...])
    for h in range(MEM_HEADS):
        sl = slice(h * HEAD_DIM, (h + 1) * HEAD_DIM)
        mk_ref[:, sl] = _rms(kv[:, sl], kg_ref[...]).astype(BF16)
    mv_ref[...] = kv[:, MEM_DIM:].astype(BF16)


def _mem_kv(mem2d, g, w, kg, batch, l):
    n = mem2d.shape[0]
    m = n // batch
    out = jax.ShapeDtypeStruct((n, MEM_DIM), BF16)
    ospec = pl.BlockSpec((m, MEM_DIM), lambda b: (b, 0))
    return pl.pallas_call(
        _mem_kv_kernel,
        grid=(batch,),
        in_specs=[
            pl.BlockSpec((m, D_MODEL), lambda b: (b, 0)),
            _lp(l, (1, D_MODEL), lambda b: (0, 0)),
            _lp(l, (D_MODEL, 2 * MEM_DIM), lambda b: (0, 0)),
            _lp(l, (1, HEAD_DIM), lambda b: (0, 0)),
        ],
        out_specs=[ospec, ospec],
        out_shape=[out, out],
        compiler_params=_cparams(("parallel",)),
        name="mem_kv",
    )(mem2d, g, w, kg)


def _mem_attn_kernel(q_ref, mk_ref, mv_ref, qg_ref, o_ref):
    scale = HEAD_DIM ** -0.5
    for h in range(MEM_HEADS):
        sl = slice(h * HEAD_DIM, (h + 1) * HEAD_DIM)
        qh = (_rms(q_ref[:, sl].astype(F32), qg_ref[...]) * scale).astype(BF16)
        s = _dot_nt(qh, mk_ref[:, sl])
        p = jnp.exp(s - jnp.max(s, axis=1, keepdims=True))
        l = jnp.sum(p, axis=1, keepdims=True)
        o = _dot((p / l).astype(BF16), mv_ref[:, sl])
        o_ref[:, sl] = o.astype(BF16)


def _mem_attn(p, mk, mv, qg, batch, seq, l):
    t = p.shape[0]
    tq = 512
    nq = seq // tq
    m = mk.shape[0] // batch
    return pl.pallas_call(
        _mem_attn_kernel,
        grid=(batch, nq),
        in_specs=[
            pl.BlockSpec((tq, MEM_DIM), lambda b, i: (b * nq + i, P_MQ // MEM_DIM)),
            pl.BlockSpec((m, MEM_DIM), lambda b, i: (b, 0)),
            pl.BlockSpec((m, MEM_DIM), lambda b, i: (b, 0)),
            _lp(l, (1, HEAD_DIM), lambda b, i: (0, 0)),
        ],
        out_specs=pl.BlockSpec((tq, MEM_DIM), lambda b, i: (b * nq + i, 0)),
        out_shape=jax.ShapeDtypeStruct((t, MEM_DIM), BF16),
        compiler_params=_cparams(("parallel", "parallel")),
        name="mem_attn",
    )(p, mk, mv, qg)


CONV_HALO = 32
CONV_ROWS = 64


def _conv_kernel(cur_a_ref, cur_g_ref, prev_a_ref, prev_g_ref, inb_ref, cw_ref, cb_ref, lg_ref,
                 lb_ref, o_ref, u_ref, sh_ref):
    ts = cur_a_ref.shape[0]
    n = CONV_HALO + ts

    def glu(a, g):
        a = a.astype(F32) + inb_ref[:, :CONV_CH]
        g = g.astype(F32) + inb_ref[:, CONV_CH:]
        return a * jax.nn.sigmoid(g)

    u_prev = glu(prev_a_ref[ts - CONV_HALO:, :], prev_g_ref[ts - CONV_HALO:, :])
    u_ref[:CONV_HALO, :] = jnp.where(pl.program_id(1) == 0, 0.0, u_prev)
    u_ref[CONV_HALO:n, :] = glu(cur_a_ref[...], cur_g_ref[...])
    u_ref[n:, :] = jnp.zeros((SUBLANES, CONV_CH), F32)
    for r in range(1, SUBLANES):
        sh_ref[r - 1] = u_ref[r:r + n, :]

    first_tap = CONV_HALO - (CONV_WIDTH - 1)
    for t in range(ts // CONV_ROWS):
        base = t * CONV_ROWS
        y = jnp.broadcast_to(cb_ref[...], (CONV_ROWS, CONV_CH))
        for w in range(CONV_WIDTH):
            r = (first_tap + w) % SUBLANES
            a = base + first_tap + w - r
            rows = u_ref[a:a + CONV_ROWS, :] if r == 0 else sh_ref[r - 1, a:a + CONV_ROWS, :]
            y = y + rows * cw_ref[w:w + 1, :]
        mu = jnp.mean(y, axis=-1, keepdims=True)
        d = y - mu
        var = jnp.mean(d * d, axis=-1, keepdims=True)
        z = d * lax.rsqrt(var + EPS) * lg_ref[...] + lb_ref[...]
        o_ref[base:base + CONV_ROWS, :] = (z * jax.nn.sigmoid(z)).astype(BF16)


def _conv(p, inb, cw, cb, lg, lb, batch, seq, l):
    t = p.shape[0]
    ts = 512
    ns = seq // ts
    a_blk = P_GLU // CONV_CH
    vec = lambda w: _lp(l, (1, w), lambda b, i: (0, 0))
    cur = lambda k: pl.BlockSpec((ts, CONV_CH), lambda b, i: (b * ns + i, a_blk + k))
    prev = lambda k: pl.BlockSpec((ts, CONV_CH), lambda b, i: (b * ns + jnp.maximum(i - 1, 0), a_blk + k))
    return pl.pallas_call(
        _conv_kernel,
        grid=(batch, ns),
        in_specs=[
            cur(0), cur(1), prev(0), prev(1),
            vec(2 * CONV_CH),
            _lp(l, (CONV_WIDTH, CONV_CH), lambda b, i: (0, 0)),
            vec(CONV_CH), vec(CONV_CH), vec(CONV_CH),
        ],
        out_specs=pl.BlockSpec((ts, CONV_CH), lambda b, i: (b * ns + i, 0)),
        out_shape=jax.ShapeDtypeStruct((t, CONV_CH), BF16),
        scratch_shapes=[pltpu.VMEM((CONV_HALO + ts + SUBLANES, CONV_CH), F32),
                        pltpu.VMEM((SUBLANES - 1, CONV_HALO + ts, CONV_CH), F32)],
        compiler_params=_cparams(("parallel", "parallel")),
        name="conv",
    )(p, p, p, p, inb, cw, cb, lg, lb)


MERGE_TN = 512


def _merge_kernel(a_ref, m_ref, c_ref, g0_ref, g1_ref, g2_ref, gb0_ref, gb1_ref, gb2_ref,
                  wa_ref, wm_ref, wc_ref, cob_ref, o_ref):
    a, m, c = a_ref[...], m_ref[...], c_ref[...]
    for n in range(D_MODEL // MERGE_TN):
        sl = slice(n * MERGE_TN, (n + 1) * MERGE_TN)

        def gate(g_ref, gb_ref):
            return jax.nn.sigmoid(g_ref[:, sl].astype(F32) + gb_ref[:, sl])

        y = gate(g0_ref, gb0_ref) * _dot(a, wa_ref[:, sl])
        y = y + gate(g1_ref, gb1_ref) * _dot(m, wm_ref[:, sl])
        y = y + gate(g2_ref, gb2_ref) * (_dot(c, wc_ref[:, sl]) + cob_ref[:, sl])
        o_ref[:, sl] = y.astype(BF16)


def _merge(attn, memo, convo, p, gate_b, wa, wm, wc, cob, l):
    t = attn.shape[0]
    tm = 512
    gblk = P_GATES // D_MODEL
    row = lambda w: pl.BlockSpec((tm, w), lambda i: (i, 0))
    full = lambda r, c: _lp(l, (r, c), lambda i: (0, 0))
    gspec = lambda k: pl.BlockSpec((tm, D_MODEL), lambda i: (i, gblk + k))
    gbspec = lambda k: _lp(l, (1, D_MODEL), lambda i: (0, k))
    return pl.pallas_call(
        _merge_kernel,
        grid=(t // tm,),
        in_specs=[row(ATTN_DIM), row(MEM_DIM), row(CONV_CH),
                  gspec(0), gspec(1), gspec(2), gbspec(0), gbspec(1), gbspec(2),
                  full(ATTN_DIM, D_MODEL), full(MEM_DIM, D_MODEL), full(CONV_CH, D_MODEL),
                  full(1, D_MODEL)],
        out_specs=row(D_MODEL),
        out_shape=jax.ShapeDtypeStruct((t, D_MODEL), BF16),
        compiler_params=_cparams(("parallel",)),
        name="merge",
    )(attn, memo, convo, p, p, p, gate_b, gate_b, gate_b, wa, wm, wc, cob)


def _outproj_kernel(x_ref, mg_ref, w_ref, g_ref, x1_ref, h2_ref):
    x1 = x_ref[...] + _dot(mg_ref[...], w_ref[...])
    x1_ref[...] = x1
    h2_ref[...] = _rms(x1, g_ref[...]).astype(BF16)


def _outproj(x, merged, w, g, l):
    t = x.shape[0]
    tm = 512
    row = pl.BlockSpec((tm, D_MODEL), lambda i: (i, 0))
    return pl.pallas_call(
        _outproj_kernel,
        grid=(t // tm,),
        in_specs=[row, row,
                  _lp(l, (D_MODEL, D_MODEL), lambda i: (0, 0)),
                  _lp(l, (1, D_MODEL), lambda i: (0, 0))],
        out_specs=[row, row],
        out_shape=[jax.ShapeDtypeStruct((t, D_MODEL), F32), jax.ShapeDtypeStruct((t, D_MODEL), BF16)],
        compiler_params=_cparams(("parallel",)),
        name="outproj",
    )(x, merged, w, g)


def _mlp_kernel(h_ref, x_ref, wu_ref, wd_ref, o_ref):
    @pl.when(pl.program_id(1) == 0)
    def _():
        o_ref[...] = x_ref[...]

    u = jnp.maximum(_dot(h_ref[...], wu_ref[...]), 0.0)
    o_ref[...] += _dot((u * u).astype(BF16), wd_ref[...])


def _mlp(h2, x1, wu, wd, l):
    t = x1.shape[0]
    tm, tf = 512, 1024
    return pl.pallas_call(
        _mlp_kernel,
        grid=(t // tm, FFN_DIM // tf),
        in_specs=[
            pl.BlockSpec((tm, D_MODEL), lambda i, f: (i, 0)),
            pl.BlockSpec((tm, D_MODEL), lambda i, f: (i, 0)),
            _lp(l, (D_MODEL, tf), lambda i, f: (0, f)),
            _lp(l, (tf, D_MODEL), lambda i, f: (f, 0)),
        ],
        out_specs=pl.BlockSpec((tm, D_MODEL), lambda i, f: (i, 0)),
        out_shape=jax.ShapeDtypeStruct((t, D_MODEL), F32),
        compiler_params=_cparams(("parallel", "arbitrary")),
        name="mlp",
    )(h2, x1, wu, wd)


def _split_w_in(w):
    narrow = IDX_DIM + IDX_HEADS
    head = w[:, :, :P_HEAD].astype(BF16)
    tail = w[:, :, P_HEAD + narrow:].astype(BF16)
    small = jnp.pad(w[:, :, P_HEAD:P_HEAD + narrow],
                    ((0, 0), (0, 0), (0, PS_WIDTH - narrow))).astype(BF16)
    return head, tail, small


def kernel(x, mem, positions, norm1_g, w_in, q_norm_g, k_norm_g, mem_norm_g, w_mem_kv, mq_norm_g,
           mk_norm_g, conv_in_b, conv_w, conv_b, conv_ln_g, conv_ln_b, gate_b, w_attn_o, w_mem_o,
           w_conv_o, conv_o_b, w_out, norm2_g, w_up, w_down):
    batch, seq, _ = x.shape
    depth = w_in.shape[0]
    t = batch * seq
    xf = x.reshape(t, D_MODEL)
    mem2d = mem.reshape(batch * mem.shape[1], D_MODEL)
    tabs = _rope_tables(positions)

    vec = lambda v: v.reshape(depth, 1, -1)
    norm1_g, q_norm_g, k_norm_g, mem_norm_g, mq_norm_g, mk_norm_g = map(
        vec, (norm1_g, q_norm_g, k_norm_g, mem_norm_g, mq_norm_g, mk_norm_g))
    conv_in_b, conv_b, conv_ln_g, conv_ln_b, gate_b, conv_o_b, norm2_g = map(
        vec, (conv_in_b, conv_b, conv_ln_g, conv_ln_b, gate_b, conv_o_b, norm2_g))
    w_head, w_tail, w_small = _split_w_in(w_in)
    w_mem_kv, w_attn_o, w_mem_o, w_conv_o, w_out, w_up, w_down = (
        w.astype(BF16) for w in (w_mem_kv, w_attn_o, w_mem_o, w_conv_o, w_out, w_up, w_down))

    for l in range(depth):
        p, ps = _proj(xf, norm1_g, w_head, w_tail, w_small, l)
        qn, iqr, kn, vt, ika, ikb, iwt = _prep(p, ps, tabs, q_norm_g, k_norm_g, l)
        attn = _dsa(qn, iqr, iwt, kn, vt, ika, ikb, batch, seq)
        mk, mv = _mem_kv(mem2d, mem_norm_g, w_mem_kv, mk_norm_g, batch, l)
        memo = _mem_attn(p, mk, mv, mq_norm_g, batch, seq, l)
        convo = _conv(p, conv_in_b, conv_w, conv_b, conv_ln_g, conv_ln_b, batch, seq, l)
        merged = _merge(attn, memo, convo, p, gate_b, w_attn_o, w_mem_o, w_conv_o, conv_o_b, l)
        x1, h2 = _outproj(xf, merged, w_out, norm2_g, l)
        xf = _mlp(h2, x1, w_up, w_down, l)
    return xf.reshape(batch, seq, D_MODEL)
```

```python
import functools
import math

import jax
import jax.numpy as jnp
from jax import lax
from jax.experimental import pallas as pl
from jax.experimental.pallas import tpu as pltpu

F32 = jnp.float32
BF16 = jnp.bfloat16
I32 = jnp.int32

D_MODEL = 2048
HEAD_DIM = 128
N_HEADS = 8
N_KV_HEADS = 2
IDX_HEADS = 16
IDX_DIM = 64
TOPK = 256
CONV_CH = 512
CONV_WIDTH = 31
MEM_HEADS = 4
FFN_DIM = 4 * D_MODEL
ROPE_THETA = 500000.0
N_BRANCH = 3
EPS = 1e-6
ATTN_DIM = N_HEADS * HEAD_DIM
KV_DIM = N_KV_HEADS * HEAD_DIM
MEM_DIM = MEM_HEADS * HEAD_DIM
HEAD_ROT_HALF = HEAD_DIM // 8
IDX_ROT_HALF = IDX_DIM // 8

LANES = 128
VMEM_LIMIT = 56 * 1024 * 1024

P_Q = 0
P_K = P_Q + ATTN_DIM
P_V = P_K + KV_DIM
P_IQ = P_V + KV_DIM
P_HEAD = P_IQ + IDX_HEADS * IDX_DIM
P_GLU = P_HEAD
P_MQ = P_GLU + 2 * CONV_CH
P_GATES = P_MQ + MEM_DIM
P_WIDTH = P_GATES + N_BRANCH * D_MODEL
P_TAIL = P_WIDTH - P_HEAD
PS_WIDTH = LANES

QB = 128
KC = 512
SUB = 256
SLAB = 32
SUBLANES = 8
LOG2E = 1.4426950408889634
NEG_BIG = -1e30
INT_MIN = -2 ** 31
KEY_NEG_INF = -2139095041
KEY_POS_INF = 2139095040


def _cparams(sem, vmem=VMEM_LIMIT):
    return pltpu.CompilerParams(dimension_semantics=sem, vmem_limit_bytes=vmem)


def _lp(l, shape, imap):
    return pl.BlockSpec((pl.Squeezed(),) + shape, lambda *g: (l,) + imap(*g))


def _dot(a, b):
    return jnp.dot(a, b, preferred_element_type=F32)


def _dot_nt(a, b):
    return lax.dot_general(a, b, (((1,), (1,)), ((), ())), preferred_element_type=F32)


def _sigmoid(x):
    return 0.5 * jnp.tanh(0.5 * x) + 0.5


def _rms(xf, g):
    return xf * lax.rsqrt(jnp.mean(xf * xf, axis=-1, keepdims=True) + EPS) * g


def _rope_table_kernel(pos_ref, ch_ref, sh1_ref, sh2_ref, ci_ref, si1_ref, si2_ref):
    pos = pos_ref[...].astype(F32)
    lane = lax.broadcasted_iota(I32, (1, LANES), 1)

    def tables(period, half, c_ref, s1_ref, s2_ref):
        r = lane & (period - 1)
        fi = (r & (half - 1)).astype(F32)
        inv = jnp.exp(fi * (-math.log(ROPE_THETA) / half))
        ang = pos * inv
        c = jnp.cos(ang)
        s = jnp.sin(ang)
        first = r < half
        second = (r >= half) & (r < 2 * half)
        c_ref[...] = jnp.where(first | second, c, 1.0)
        s1_ref[...] = jnp.where(second, s, 0.0)
        s2_ref[...] = jnp.where(first, -s, 0.0)

    tables(HEAD_DIM, HEAD_ROT_HALF, ch_ref, sh1_ref, sh2_ref)
    tables(IDX_DIM, IDX_ROT_HALF, ci_ref, si1_ref, si2_ref)


def _rope_tables(positions):
    t = positions.size
    tp = 512
    tab = jax.ShapeDtypeStruct((t, LANES), F32)
    spec = pl.BlockSpec((tp, LANES), lambda i: (i, 0))
    return pl.pallas_call(
        _rope_table_kernel,
        grid=(t // tp,),
        in_specs=[pl.BlockSpec((tp, 1), lambda i: (i, 0))],
        out_specs=[spec] * 6,
        out_shape=[tab] * 6,
        compiler_params=_cparams(("parallel",)),
        name="rope_tables",
    )(positions.reshape(t, 1))


def _rope(t, c, s1, s2, half):
    return t * c + pltpu.roll(t, half, 1) * s1 + pltpu.roll(t, LANES - half, 1) * s2


PROJ_TN = 1280


def _proj_kernel(x_ref, g_ref, wh_ref, wt_ref, ws_ref, o_ref, os_ref, h_ref):
    j = pl.program_id(1)

    @pl.when(j == 0)
    def _():
        h = _rms(x_ref[...], g_ref[...]).astype(BF16)
        h_ref[...] = h
        os_ref[...] = _dot(h, ws_ref[...])

    @pl.when(j < P_HEAD // PROJ_TN)
    def _():
        o_ref[...] = _dot(h_ref[...], wh_ref[...]).astype(o_ref.dtype)

    @pl.when(j >= P_HEAD // PROJ_TN)
    def _():
        o_ref[...] = _dot(h_ref[...], wt_ref[...]).astype(o_ref.dtype)


def _proj(x, g, wh, wt, ws, l):
    t = x.shape[0]
    tm, tn = 1024, PROJ_TN
    n_head = P_HEAD // tn
    return pl.pallas_call(
        _proj_kernel,
        grid=(t // tm, P_WIDTH // tn),
        in_specs=[
            pl.BlockSpec((tm, D_MODEL), lambda i, j: (i, 0)),
            _lp(l, (1, D_MODEL), lambda i, j: (0, 0)),
            _lp(l, (D_MODEL, tn), lambda i, j: (0, jnp.minimum(j, n_head - 1))),
            _lp(l, (D_MODEL, tn), lambda i, j: (0, jnp.maximum(j - n_head, 0))),
            _lp(l, (D_MODEL, PS_WIDTH), lambda i, j: (0, 0)),
        ],
        out_specs=[
            pl.BlockSpec((tm, tn), lambda i, j: (i, j)),
            pl.BlockSpec((tm, PS_WIDTH), lambda i, j: (i, 0)),
        ],
        out_shape=[
            jax.ShapeDtypeStruct((t, P_WIDTH), BF16),
            jax.ShapeDtypeStruct((t, PS_WIDTH), F32),
        ],
        scratch_shapes=[pltpu.VMEM((tm, D_MODEL), BF16)],
        compiler_params=_cparams(("parallel", "arbitrary")),
        name="proj",
    )(x, g, wh, wt, ws)


def _prep_kernel(q_ref, iql_ref, iqh_ref, kv_ref, ps_ref, ch_ref, sh1_ref, sh2_ref, ci_ref, si1_ref,
                 si2_ref, qg_ref, kg_ref, qn_ref, iqr_ref, kn_ref, vt_ref, ika_ref, ikb_ref,
                 iwt_ref):
    ch, sh1, sh2 = ch_ref[...], sh1_ref[...], sh2_ref[...]
    ci, si1, si2 = ci_ref[...], si1_ref[...], si2_ref[...]
    scale = (HEAD_DIM ** -0.5) * LOG2E
    half_groups = N_HEADS // 2
    for h in range(N_HEADS):
        sl = slice(h * HEAD_DIM, (h + 1) * HEAD_DIM)
        qh = _rms(q_ref[:, sl].astype(F32), qg_ref[...])
        qn_ref[:, sl] = (_rope(qh, ch, sh1, sh2, HEAD_ROT_HALF) * scale).astype(BF16)
        src = iql_ref if h < half_groups else iqh_ref
        hs = h % half_groups
        iqh = src[:, hs * LANES:(hs + 1) * LANES].astype(F32)
        iqr_ref[:, sl] = _rope(iqh, ci, si1, si2, IDX_ROT_HALF).astype(BF16)
    for g in range(N_KV_HEADS):
        sl = slice(g * HEAD_DIM, (g + 1) * HEAD_DIM)
        kh = _rms(kv_ref[:, sl].astype(F32), kg_ref[...])
        kn_ref[:, sl] = _rope(kh, ch, sh1, sh2, HEAD_ROT_HALF).astype(BF16)
        vh = kv_ref[:, KV_DIM + g * HEAD_DIM:KV_DIM + (g + 1) * HEAD_DIM].astype(F32)
        vt_ref[0, sl, :] = vh.T.astype(BF16)
    ps = ps_ref[...]
    lane = lax.broadcasted_iota(I32, (1, LANES), 1)
    ikr = jnp.where(lane < IDX_DIM, _rope(ps, ci, si1, si2, IDX_ROT_HALF), 0.0)
    ika_ref[...] = ikr.astype(BF16)
    ikb_ref[...] = pltpu.roll(ikr, IDX_DIM, 1).astype(BF16)
    idx_scale = (IDX_DIM ** -0.5) * (IDX_HEADS ** -0.5)
    iws = jnp.where(lane < IDX_HEADS, pltpu.roll(ps, LANES - IDX_DIM, 1) * idx_scale, 0.0)
    iwt_ref[...] = iws.T


def _prep(p, ps, tabs, qg, kg, l):
    t = p.shape[0]
    tp = KC
    half_iq = IDX_HEADS * IDX_DIM // 2
    tab_spec = pl.BlockSpec((tp, LANES), lambda i: (i, 0))
    vec_spec = _lp(l, (1, HEAD_DIM), lambda i: (0, 0))

    def out(width, dtype):
        return (pl.BlockSpec((tp, width), lambda i: (i, 0)), jax.ShapeDtypeStruct((t, width), dtype))

    outs = [out(ATTN_DIM, BF16), out(ATTN_DIM, BF16), out(KV_DIM, BF16),
            (pl.BlockSpec((1, KV_DIM, tp), lambda i: (i, 0, 0)),
             jax.ShapeDtypeStruct((t // tp, KV_DIM, tp), BF16)),
            out(LANES, BF16), out(LANES, BF16),
            (pl.BlockSpec((LANES, tp), lambda i: (0, i)), jax.ShapeDtypeStruct((LANES, t), F32))]
    return pl.pallas_call(
        _prep_kernel,
        grid=(t // tp,),
        in_specs=[
            pl.BlockSpec((tp, ATTN_DIM), lambda i: (i, P_Q // ATTN_DIM)),
            pl.BlockSpec((tp, half_iq), lambda i: (i, P_IQ // half_iq)),
            pl.BlockSpec((tp, half_iq), lambda i: (i, P_IQ // half_iq + 1)),
            pl.BlockSpec((tp, 2 * KV_DIM), lambda i: (i, P_K // (2 * KV_DIM))),
            tab_spec,
        ] + [tab_spec] * 6 + [vec_spec, vec_spec],
        out_specs=[o[0] for o in outs],
        out_shape=[o[1] for o in outs],
        compiler_params=_cparams(("parallel",)),
        name="prep",
    )(p, p, p, p, ps, *tabs, qg, kg)


def _sortable(x):
    bits = lax.bitcast_convert_type(x, I32)
    return bits ^ ((bits >> 31) & 0x7FFFFFFF)


N_PAIRS = N_HEADS // 2
N_PART = 4


def _dsa_kernel(qn_ref, iq_ref, iwt_ref, kn_ref, vt_ref, ika_ref, ikb_ref, o_ref,
                key_ref, thr_ref, nge_ref, bias_ref, m_ref, l_ref, acc_ref, sa_ref, sb_ref, p_ref):
    j = pl.program_id(1)
    nck = j // (KC // QB) + 1
    seq = kn_ref.shape[0]

    iwt = iwt_ref[...]
    q_pos = j * QB + lax.broadcasted_iota(I32, (SUB, QB), 1)
    n_grp = IDX_HEADS // 2

    def score_chunk(c, carry):
        for hf in range(KC // SUB):
            start = pl.multiple_of(c * KC + hf * SUB, SUB)
            ka = ika_ref[pl.ds(start, SUB), :]
            kb = ikb_ref[pl.ds(start, SUB), :]
            acc = jnp.zeros((SUB, QB), F32)
            for a in range(n_grp // 2):
                b = a + n_grp // 2
                iq2 = jnp.concatenate([iq_ref[:, a * LANES:(a + 1) * LANES],
                                       iq_ref[:, b * LANES:(b + 1) * LANES]], axis=0)
                sa = _dot_nt(ka, iq2)
                sb = _dot_nt(kb, iq2)
                acc = acc + jnp.maximum(sa[:, :QB], 0.0) * iwt[2 * a:2 * a + 1, :]
                acc = acc + jnp.maximum(sb[:, :QB], 0.0) * iwt[2 * a + 1:2 * a + 2, :]
                acc = acc + jnp.maximum(sa[:, QB:], 0.0) * iwt[2 * b:2 * b + 1, :]
                acc = acc + jnp.maximum(sb[:, QB:], 0.0) * iwt[2 * b + 1:2 * b + 2, :]
            k_pos = start + lax.broadcasted_iota(I32, (SUB, QB), 0)
            acc = jnp.where(k_pos <= q_pos, acc, -jnp.inf)
            key_ref[c, hf * SUB:(hf + 1) * SUB, :] = _sortable(acc)
        return carry

    lax.fori_loop(0, nck, score_chunk, 0)

    n_loaded = jnp.broadcast_to(nck * KC, (SUBLANES, QB)).astype(I32)

    def search(nc):
        def search_pass(p, carry):
            tu, n_at = carry
            cand_u = tu | lax.shift_left(jnp.int32(1), 31 - p)
            cand = cand_u ^ INT_MIN
            parts = [jnp.zeros((SUBLANES, QB), I32)] * N_PART
            for c in range(nc):
                for r in range(KC // SUBLANES):
                    k = key_ref[c, r * SUBLANES:(r + 1) * SUBLANES, :]
                    parts[r % N_PART] = parts[r % N_PART] + jnp.where(k >= cand, 1, 0)
            tot = (parts[0] + parts[1]) + (parts[2] + parts[3])
            n = jnp.broadcast_to(jnp.sum(tot, axis=0, keepdims=True), (SUBLANES, QB))
            ok = n >= TOPK
            return jnp.where(ok, cand_u, tu), jnp.where(ok, n, n_at)

        tu, n_ge = lax.fori_loop(0, 32, search_pass, (jnp.zeros((SUBLANES, QB), I32), n_loaded))
        thr_ref[...] = tu ^ INT_MIN
        nge_ref[...] = n_ge

    for nc in range(1, key_ref.shape[0] + 1):
        pl.when(nck == nc)(functools.partial(search, nc))
    thr = thr_ref[...]
    n_ge = nge_ref[...]

    row_iota = lax.broadcasted_iota(I32, (SUBLANES, QB), 0)
    tied = (n_ge > TOPK) & (thr > KEY_NEG_INF)
    any_tied = jnp.max(jnp.where(tied, 1, 0)) > 0

    def count32(pred):
        def chunk(c, parts):
            parts = list(parts)
            for r in range(KC // SUBLANES):
                k = key_ref[c, r * SUBLANES:(r + 1) * SUBLANES, :]
                hit = pred(k, c * KC + r * SUBLANES)
                parts[r % N_PART] = parts[r % N_PART] + jnp.where(hit, 1, 0)
            return tuple(parts)

        zero = jnp.zeros((SUBLANES, QB), I32)
        parts = lax.fori_loop(0, nck, chunk, (zero,) * N_PART)
        tot = (parts[0] + parts[1]) + (parts[2] + parts[3])
        return jnp.broadcast_to(jnp.sum(tot, axis=0, keepdims=True), (SUBLANES, QB))

    thr_fin = jnp.maximum(thr, KEY_NEG_INF + 1)

    @pl.when(jnp.logical_not(any_tied))
    def _():
        def bias_chunk(c, carry):
            for r in range(KC // SUBLANES):
                sl = slice(r * SUBLANES, (r + 1) * SUBLANES)
                k = key_ref[c, sl, :]
                sel = (k >= thr_fin) & (k < KEY_POS_INF)
                bias_ref[c, sl, :] = jnp.where(sel, 0.0, NEG_BIG)
            return carry

        lax.fori_loop(0, nck, bias_chunk, 0)

    @pl.when(any_tied)
    def _():
        need = TOPK - count32(lambda k, _: k > thr)

        def pos_pass(p, x):
            cand = x | lax.shift_left(jnp.int32(1), (seq.bit_length() - 2) - p)
            n = count32(lambda k, r0: (k == thr) & (row_iota + r0 < cand))
            return jnp.where(n < need, cand, x)

        xlim = lax.fori_loop(0, seq.bit_length() - 1, pos_pass, jnp.zeros((SUBLANES, QB), I32))

        def bias_chunk(c, carry):
            for r in range(KC // SUBLANES):
                sl = slice(r * SUBLANES, (r + 1) * SUBLANES)
                k = key_ref[c, sl, :]
                pos = row_iota + (c * KC + r * SUBLANES)
                sel = (k > thr) | ((k == thr) & (pos <= xlim))
                sel = sel & (k >= thr_fin) & (k < KEY_POS_INF)
                bias_ref[c, sl, :] = jnp.where(sel, 0.0, NEG_BIG)
            return carry

        lax.fori_loop(0, nck, bias_chunk, 0)

    m_ref[...] = jnp.full(m_ref.shape, NEG_BIG, F32)
    l_ref[...] = jnp.zeros(l_ref.shape, F32)
    acc_ref[...] = jnp.zeros(acc_ref.shape, F32)
    group = N_HEADS // N_KV_HEADS

    def logits(c, s_ref):
        col_max = []
        for pr in range(N_PAIRS):
            g = (2 * pr) // group
            q2 = jnp.concatenate([qn_ref[:, (2 * pr) * HEAD_DIM:(2 * pr + 1) * HEAD_DIM],
                                  qn_ref[:, (2 * pr + 1) * HEAD_DIM:(2 * pr + 2) * HEAD_DIM]],
                                 axis=0)
            mx = jnp.full((SUBLANES, 2 * QB), NEG_BIG, F32)
            for hf in range(KC // SUB):
                start = pl.multiple_of(c * KC + hf * SUB, SUB)
                kc = kn_ref[pl.ds(start, SUB), g * HEAD_DIM:(g + 1) * HEAD_DIM]
                bias = bias_ref[c, hf * SUB:(hf + 1) * SUB, :]
                s = _dot_nt(kc, q2) + jnp.concatenate([bias, bias], axis=1)
                s_ref[pr, hf * SUB:(hf + 1) * SUB, :] = s
                for r in range(SUB // SUBLANES):
                    mx = jnp.maximum(mx, s[r * SUBLANES:(r + 1) * SUBLANES, :])
            col_max.append(jnp.broadcast_to(jnp.max(mx, axis=0, keepdims=True), (SUBLANES, 2 * QB)))
        return tuple(col_max)

    def finish(c, s_ref, col_max):
        alphas = []
        for pr in range(N_PAIRS):
            m_prev = m_ref[pr]
            m_new = jnp.maximum(m_prev, col_max[pr])
            alpha = jnp.exp2(m_prev - m_new)
            lsum = jnp.zeros((SUBLANES, 2 * QB), F32)
            for t in range(KC // SLAB):
                rows = slice(t * SLAB, (t + 1) * SLAB)
                p = jnp.exp2(s_ref[pr, rows, :] - m_new[0:1, :])
                for r in range(SLAB // SUBLANES):
                    lsum = lsum + p[r * SUBLANES:(r + 1) * SUBLANES, :]
                p_ref[pr, rows, :] = p.astype(BF16)
            m_ref[pr] = m_new
            l_ref[pr] = alpha * l_ref[pr] + jnp.sum(lsum, axis=0, keepdims=True)
            alphas.append(alpha)
        for pr in range(N_PAIRS):
            g = (2 * pr) // group
            vt = vt_ref[c, g * HEAD_DIM:(g + 1) * HEAD_DIM, :]
            acc_ref[pr] = acc_ref[pr] * alphas[pr][0:1, :] + _dot(vt, p_ref[pr])

    def attn_step(i, col_max):
        c = 2 * i
        mid = logits(c + 1, sb_ref)
        finish(c, sa_ref, col_max)
        nxt = logits(c + 2, sa_ref)
        finish(c + 1, sb_ref, mid)
        return nxt

    n_steps = (nck - 1) // 2
    tail = 2 * n_steps
    col_max = lax.fori_loop(0, n_steps, attn_step, logits(0, sa_ref))

    @pl.when(tail == nck - 1)
    def _():
        finish(tail, sa_ref, col_max)

    @pl.when(tail != nck - 1)
    def _():
        mid = logits(tail + 1, sb_ref)
        finish(tail, sa_ref, col_max)
        finish(tail + 1, sb_ref, mid)

    for pr in range(N_PAIRS):
        o_t = acc_ref[pr] / l_ref[pr][0:1, :]
        for i in range(2):
            h = 2 * pr + i
            o_ref[:, h * HEAD_DIM:(h + 1) * HEAD_DIM] = o_t[:, i * QB:(i + 1) * QB].T.astype(BF16)


def _dsa(qn, iqr, iwt, kn, vt, ika, ikb, batch, seq):
    t = qn.shape[0]
    nb = seq // QB
    nchunk = seq // KC
    qmap = lambda b, j: (b * nb + j, 0)
    bmap = lambda b, j: (b, 0)
    return pl.pallas_call(
        _dsa_kernel,
        grid=(batch, nb),
        in_specs=[
            pl.BlockSpec((QB, ATTN_DIM), qmap),
            pl.BlockSpec((QB, ATTN_DIM), qmap),
            pl.BlockSpec((IDX_HEADS, QB), lambda b, j: (0, b * nb + j)),
            pl.BlockSpec((seq, KV_DIM), bmap),
            pl.BlockSpec((nchunk, KV_DIM, KC), lambda b, j: (b, 0, 0)),
            pl.BlockSpec((seq, LANES), bmap),
            pl.BlockSpec((seq, LANES), bmap),
        ],
        out_specs=pl.BlockSpec((QB, ATTN_DIM), qmap),
        out_shape=jax.ShapeDtypeStruct((t, ATTN_DIM), BF16),
        scratch_shapes=[
            pltpu.VMEM((nchunk, KC, QB), I32),
            pltpu.VMEM((SUBLANES, QB), I32),
            pltpu.VMEM((SUBLANES, QB), I32),
            pltpu.VMEM((nchunk, KC, QB), F32),
            pltpu.VMEM((N_PAIRS, SUBLANES, 2 * QB), F32),
            pltpu.VMEM((N_PAIRS, SUBLANES, 2 * QB), F32),
            pltpu.VMEM((N_PAIRS, HEAD_DIM, 2 * QB), F32),
            pltpu.VMEM((N_PAIRS, KC, 2 * QB), F32),
            pltpu.VMEM((N_PAIRS, KC, 2 * QB), F32),
            pltpu.VMEM((N_PAIRS, KC, 2 * QB), BF16),
        ],
        compiler_params=_cparams(("parallel", "arbitrary")),
        name="dsa",
    )(qn, iqr, iwt, kn, vt, ika, ikb)


def _mem_kv_kernel(mem_ref, g_ref, w_ref, kg_ref, mk_ref, mv_ref):
    m = _rms(mem_ref[...], g_ref[...]).astype(BF16)
    kv = _dot(m, w_ref[...])
    for h in range(MEM_HEADS):
        sl = slice(h * HEAD_DIM, (h + 1) * HEAD_DIM)
        mk_ref[:, sl] = _rms(kv[:, sl], kg_ref[...]).astype(BF16)
    mv_ref[...] = kv[:, MEM_DIM:].astype(BF16)


def _mem_kv(mem2d, g, w, kg, batch, l):
    n = mem2d.shape[0]
    m = n // batch
    out = jax.ShapeDtypeStruct((n, MEM_DIM), BF16)
    ospec = pl.BlockSpec((m, MEM_DIM), lambda b: (b, 0))
    return pl.pallas_call(
        _mem_kv_kernel,
        grid=(batch,),
        in_specs=[
            pl.BlockSpec((m, D_MODEL), lambda b: (b, 0)),
            _lp(l, (1, D_MODEL), lambda b: (0, 0)),
            _lp(l, (D_MODEL, 2 * MEM_DIM), lambda b: (0, 0)),
            _lp(l, (1, HEAD_DIM), lambda b: (0, 0)),
        ],
        out_specs=[ospec, ospec],
        out_shape=[out, out],
        compiler_params=_cparams(("parallel",)),
        name="mem_kv",
    )(mem2d, g, w, kg)


def _mem_attn_kernel(q_ref, mk_ref, mv_ref, qg_ref, o_ref):
    scale = HEAD_DIM ** -0.5
    for h in range(MEM_HEADS):
        sl = slice(h * HEAD_DIM, (h + 1) * HEAD_DIM)
        qh = (_rms(q_ref[:, sl].astype(F32), qg_ref[...]) * scale).astype(BF16)
        s = _dot_nt(qh, mk_ref[:, sl])
        p = jnp.exp(s - jnp.max(s, axis=1, keepdims=True))
        l = jnp.sum(p, axis=1, keepdims=True)
        o = _dot((p / l).astype(BF16), mv_ref[:, sl])
        o_ref[:, sl] = o.astype(BF16)


def _mem_attn(p, mk, mv, qg, batch, seq, l):
    t = p.shape[0]
    tq = 512
    nq = seq // tq
    m = mk.shape[0] // batch
    return pl.pallas_call(
        _mem_attn_kernel,
        grid=(batch, nq),
        in_specs=[
            pl.BlockSpec((tq, MEM_DIM), lambda b, i: (b * nq + i, P_MQ // MEM_DIM)),
            pl.BlockSpec((m, MEM_DIM), lambda b, i: (b, 0)),
            pl.BlockSpec((m, MEM_DIM), lambda b, i: (b, 0)),
            _lp(l, (1, HEAD_DIM), lambda b, i: (0, 0)),
        ],
        out_specs=pl.BlockSpec((tq, MEM_DIM), lambda b, i: (b * nq + i, 0)),
        out_shape=jax.ShapeDtypeStruct((t, MEM_DIM), BF16),
        compiler_params=_cparams(("parallel", "parallel")),
        name="mem_attn",
    )(p, mk, mv, qg)


CONV_HALO = 32
CONV_ROWS = 64


def _conv_kernel(cur_a_ref, cur_g_ref, prev_a_ref, prev_g_ref, inb_ref, cw_ref, cb_ref, lg_ref,
                 lb_ref, o_ref, u_ref, sh_ref):
    ts = cur_a_ref.shape[0]
    n = CONV_HALO + ts

    def glu(a, g):
        a = a.astype(F32) + inb_ref[:, :CONV_CH]
        g = g.astype(F32) + inb_ref[:, CONV_CH:]
        return a * _sigmoid(g)

    u_prev = glu(prev_a_ref[ts - CONV_HALO:, :], prev_g_ref[ts - CONV_HALO:, :])
    u_ref[:CONV_HALO, :] = jnp.where(pl.program_id(1) == 0, 0.0, u_prev)
    u_ref[CONV_HALO:n, :] = glu(cur_a_ref[...], cur_g_ref[...])
    u_ref[n:, :] = jnp.zeros((SUBLANES, CONV_CH), F32)
    for r in range(1, SUBLANES):
        sh_ref[r - 1] = u_ref[r:r + n, :]

    first_tap = CONV_HALO - (CONV_WIDTH - 1)
    for t in range(ts // CONV_ROWS):
        base = t * CONV_ROWS
        y = jnp.broadcast_to(cb_ref[...], (CONV_ROWS, CONV_CH))
        for w in range(CONV_WIDTH):
            r = (first_tap + w) % SUBLANES
            a = base + first_tap + w - r
            rows = u_ref[a:a + CONV_ROWS, :] if r == 0 else sh_ref[r - 1, a:a + CONV_ROWS, :]
            y = y + rows * cw_ref[w:w + 1, :]
        mu = jnp.mean(y, axis=-1, keepdims=True)
        d = y - mu
        var = jnp.mean(d * d, axis=-1, keepdims=True)
        z = d * lax.rsqrt(var + EPS) * lg_ref[...] + lb_ref[...]
        o_ref[base:base + CONV_ROWS, :] = (z * _sigmoid(z)).astype(BF16)


def _conv(p, inb, cw, cb, lg, lb, batch, seq, l):
    t = p.shape[0]
    ts = 512
    ns = seq // ts
    a_blk = P_GLU // CONV_CH
    vec = lambda w: _lp(l, (1, w), lambda b, i: (0, 0))
    cur = lambda k: pl.BlockSpec((ts, CONV_CH), lambda b, i: (b * ns + i, a_blk + k))
    prev = lambda k: pl.BlockSpec((ts, CONV_CH), lambda b, i: (b * ns + jnp.maximum(i - 1, 0), a_blk + k))
    return pl.pallas_call(
        _conv_kernel,
        grid=(batch, ns),
        in_specs=[
            cur(0), cur(1), prev(0), prev(1),
            vec(2 * CONV_CH),
            _lp(l, (CONV_WIDTH, CONV_CH), lambda b, i: (0, 0)),
            vec(CONV_CH), vec(CONV_CH), vec(CONV_CH),
        ],
        out_specs=pl.BlockSpec((ts, CONV_CH), lambda b, i: (b * ns + i, 0)),
        out_shape=jax.ShapeDtypeStruct((t, CONV_CH), BF16),
        scratch_shapes=[pltpu.VMEM((CONV_HALO + ts + SUBLANES, CONV_CH), F32),
                        pltpu.VMEM((SUBLANES - 1, CONV_HALO + ts, CONV_CH), F32)],
        compiler_params=_cparams(("parallel", "parallel")),
        name="conv",
    )(p, p, p, p, inb, cw, cb, lg, lb)


MERGE_TN = 512


def _merge_kernel(a_ref, m_ref, c_ref, g0_ref, g1_ref, g2_ref, gb0_ref, gb1_ref, gb2_ref,
                  wa_ref, wm_ref, wc_ref, cob_ref, o_ref):
    a, m, c = a_ref[...], m_ref[...], c_ref[...]
    for n in range(D_MODEL // MERGE_TN):
        sl = slice(n * MERGE_TN, (n + 1) * MERGE_TN)

        def gate(g_ref, gb_ref):
            return _sigmoid(g_ref[:, sl].astype(F32) + gb_ref[:, sl])

        y = gate(g0_ref, gb0_ref) * _dot(a, wa_ref[:, sl])
        y = y + gate(g1_ref, gb1_ref) * _dot(m, wm_ref[:, sl])
        y = y + gate(g2_ref, gb2_ref) * (_dot(c, wc_ref[:, sl]) + cob_ref[:, sl])
        o_ref[:, sl] = y.astype(BF16)


def _merge(attn, memo, convo, p, gate_b, wa, wm, wc, cob, l):
    t = attn.shape[0]
    tm = 512
    gblk = P_GATES // D_MODEL
    row = lambda w: pl.BlockSpec((tm, w), lambda i: (i, 0))
    full = lambda r, c: _lp(l, (r, c), lambda i: (0, 0))
    gspec = lambda k: pl.BlockSpec((tm, D_MODEL), lambda i: (i, gblk + k))
    gbspec = lambda k: _lp(l, (1, D_MODEL), lambda i: (0, k))
    return pl.pallas_call(
        _merge_kernel,
        grid=(t // tm,),
        in_specs=[row(ATTN_DIM), row(MEM_DIM), row(CONV_CH),
                  gspec(0), gspec(1), gspec(2), gbspec(0), gbspec(1), gbspec(2),
                  full(ATTN_DIM, D_MODEL), full(MEM_DIM, D_MODEL), full(CONV_CH, D_MODEL),
                  full(1, D_MODEL)],
        out_specs=row(D_MODEL),
        out_shape=jax.ShapeDtypeStruct((t, D_MODEL), BF16),
        compiler_params=_cparams(("parallel",)),
        name="merge",
    )(attn, memo, convo, p, p, p, gate_b, gate_b, gate_b, wa, wm, wc, cob)


def _outproj_kernel(x_ref, mg_ref, w_ref, g_ref, x1_ref, h2_ref):
    x1 = x_ref[...] + _dot(mg_ref[...], w_ref[...])
    x1_ref[...] = x1
    h2_ref[...] = _rms(x1, g_ref[...]).astype(BF16)


def _outproj(x, merged, w, g, l):
    t = x.shape[0]
    tm = 512
    row = pl.BlockSpec((tm, D_MODEL), lambda i: (i, 0))
    return pl.pallas_call(
        _outproj_kernel,
        grid=(t // tm,),
        in_specs=[row, row,
                  _lp(l, (D_MODEL, D_MODEL), lambda i: (0, 0)),
                  _lp(l, (1, D_MODEL), lambda i: (0, 0))],
        out_specs=[row, row],
        out_shape=[jax.ShapeDtypeStruct((t, D_MODEL), F32), jax.ShapeDtypeStruct((t, D_MODEL), BF16)],
        compiler_params=_cparams(("parallel",)),
        name="outproj",
    )(x, merged, w, g)


def _mlp_kernel(h_ref, x_ref, wu_ref, wd_ref, o_ref):
    @pl.when(pl.program_id(1) == 0)
    def _():
        o_ref[...] = x_ref[...]

    u = jnp.maximum(_dot(h_ref[...], wu_ref[...]), 0.0)
    o_ref[...] += _dot((u * u).astype(BF16), wd_ref[...])


def _mlp(h2, x1, wu, wd, l):
    t = x1.shape[0]
    tm, tf = 512, 1024
    return pl.pallas_call(
        _mlp_kernel,
        grid=(t // tm, FFN_DIM // tf),
        in_specs=[
            pl.BlockSpec((tm, D_MODEL), lambda i, f: (i, 0)),
            pl.BlockSpec((tm, D_MODEL), lambda i, f: (i, 0)),
            _lp(l, (D_MODEL, tf), lambda i, f: (0, f)),
            _lp(l, (tf, D_MODEL), lambda i, f: (f, 0)),
        ],
        out_specs=pl.BlockSpec((tm, D_MODEL), lambda i, f: (i, 0)),
        out_shape=jax.ShapeDtypeStruct((t, D_MODEL), F32),
        compiler_params=_cparams(("parallel", "arbitrary")),
        name="mlp",
    )(h2, x1, wu, wd)


def _split_w_in_kernel(w_ref, head_ref, tail_ref, small_ref):
    narrow = IDX_DIM + IDX_HEADS
    head_ref[...] = w_ref[:, :P_HEAD].astype(BF16)
    tail_ref[...] = w_ref[:, P_HEAD + narrow:].astype(BF16)
    lane = lax.broadcasted_iota(I32, (1, PS_WIDTH), 1)
    small_ref[...] = jnp.where(lane < narrow, w_ref[:, P_HEAD:P_HEAD + PS_WIDTH], 0.0).astype(BF16)


def _split_w_in(w):
    depth, rows, n_in = w.shape
    tr = 256

    def spec(width):
        return pl.BlockSpec((pl.Squeezed(), tr, width), lambda l, i: (l, i, 0))

    return pl.pallas_call(
        _split_w_in_kernel,
        grid=(depth, rows // tr),
        in_specs=[spec(n_in)],
        out_specs=[spec(P_HEAD), spec(P_TAIL), spec(PS_WIDTH)],
        out_shape=[jax.ShapeDtypeStruct((depth, rows, P_HEAD), BF16),
                   jax.ShapeDtypeStruct((depth, rows, P_TAIL), BF16),
                   jax.ShapeDtypeStruct((depth, rows, PS_WIDTH), BF16)],
        compiler_params=_cparams(("parallel", "parallel")),
        name="split_w_in",
    )(w)


def kernel(x, mem, positions, norm1_g, w_in, q_norm_g, k_norm_g, mem_norm_g, w_mem_kv, mq_norm_g,
           mk_norm_g, conv_in_b, conv_w, conv_b, conv_ln_g, conv_ln_b, gate_b, w_attn_o, w_mem_o,
           w_conv_o, conv_o_b, w_out, norm2_g, w_up, w_down):
    batch, seq, _ = x.shape
    depth = w_in.shape[0]
    t = batch * seq
    xf = x.reshape(t, D_MODEL)
    mem2d = mem.reshape(batch * mem.shape[1], D_MODEL)
    tabs = _rope_tables(positions)

    vec = lambda v: v.reshape(depth, 1, -1)
    norm1_g, q_norm_g, k_norm_g, mem_norm_g, mq_norm_g, mk_norm_g = map(
        vec, (norm1_g, q_norm_g, k_norm_g, mem_norm_g, mq_norm_g, mk_norm_g))
    conv_in_b, conv_b, conv_ln_g, conv_ln_b, gate_b, conv_o_b, norm2_g = map(
        vec, (conv_in_b, conv_b, conv_ln_g, conv_ln_b, gate_b, conv_o_b, norm2_g))
    w_head, w_tail, w_small = _split_w_in(w_in)
    w_mem_kv, w_attn_o, w_mem_o, w_conv_o, w_out, w_up, w_down = (
        w.astype(BF16) for w in (w_mem_kv, w_attn_o, w_mem_o, w_conv_o, w_out, w_up, w_down))

    for l in range(depth):
        p, ps = _proj(xf, norm1_g, w_head, w_tail, w_small, l)
        qn, iqr, kn, vt, ika, ikb, iwt = _prep(p, ps, tabs, q_norm_g, k_norm_g, l)
        attn = _dsa(qn, iqr, iwt, kn, vt, ika, ikb, batch, seq)
        mk, mv = _mem_kv(mem2d, mem_norm_g, w_mem_kv, mk_norm_g, batch, l)
        memo = _mem_attn(p, mk, mv, mq_norm_g, batch, seq, l)
        convo = _conv(p, conv_in_b, conv_w, conv_b, conv_ln_g, conv_ln_b, batch, seq, l)
        merged = _merge(attn, memo, convo, p, gate_b, w_attn_o, w_mem_o, w_conv_o, conv_o_b, l)
        x1, h2 = _outproj(xf, merged, w_out, norm2_g, l)
        xf = _mlp(h2, x1, w_up, w_down, l)
    return xf.reshape(batch, seq, D_MODEL)
```

```python
import functools
import math

import jax
import jax.numpy as jnp
from jax import lax
from jax.experimental import pallas as pl
from jax.experimental.pallas import tpu as pltpu

F32 = jnp.float32
BF16 = jnp.bfloat16
I32 = jnp.int32

D_MODEL = 2048
HEAD_DIM = 128
N_HEADS = 8
N_KV_HEADS = 2
IDX_HEADS = 16
IDX_DIM = 64
TOPK = 256
CONV_CH = 512
CONV_WIDTH = 31
MEM_HEADS = 4
FFN_DIM = 4 * D_MODEL
ROPE_THETA = 500000.0
N_BRANCH = 3
EPS = 1e-6
ATTN_DIM = N_HEADS * HEAD_DIM
KV_DIM = N_KV_HEADS * HEAD_DIM
MEM_DIM = MEM_HEADS * HEAD_DIM
HEAD_ROT_HALF = HEAD_DIM // 8
IDX_ROT_HALF = IDX_DIM // 8

LANES = 128
VMEM_LIMIT = 56 * 1024 * 1024

P_Q = 0
P_K = P_Q + ATTN_DIM
P_V = P_K + KV_DIM
P_IQ = P_V + KV_DIM
P_HEAD = P_IQ + IDX_HEADS * IDX_DIM
P_GLU = P_HEAD
P_MQ = P_GLU + 2 * CONV_CH
P_GATES = P_MQ + MEM_DIM
P_WIDTH = P_GATES + N_BRANCH * D_MODEL
P_TAIL = P_WIDTH - P_HEAD
PS_WIDTH = LANES

QB = 128
KC = 512
SUB = 256
SLAB = 32
SUBLANES = 8
LOG2E = 1.4426950408889634
NEG_BIG = -1e30
INT_MIN = -2 ** 31
KEY_NEG_INF = -2139095041
KEY_POS_INF = 2139095040


def _cparams(sem, vmem=VMEM_LIMIT):
    return pltpu.CompilerParams(dimension_semantics=sem, vmem_limit_bytes=vmem)


def _lp(l, shape, imap):
    return pl.BlockSpec((pl.Squeezed(),) + shape, lambda *g: (l,) + imap(*g))


def _dot(a, b):
    return jnp.dot(a, b, preferred_element_type=F32)


def _dot_nt(a, b):
    return lax.dot_general(a, b, (((1,), (1,)), ((), ())), preferred_element_type=F32)


def _rms(xf, g):
    return xf * lax.rsqrt(jnp.mean(xf * xf, axis=-1, keepdims=True) + EPS) * g


def _rope_table_kernel(pos_ref, ch_ref, sh_ref, ci_ref, si_ref):
    pos = pos_ref[...].astype(F32)
    lane = lax.broadcasted_iota(I32, (1, LANES), 1)

    def tables(period, half, c_ref, s_ref):
        r = lane & (period - 1)
        fi = (r & (half - 1)).astype(F32)
        inv = jnp.exp(fi * (-math.log(ROPE_THETA) / half))
        ang = pos * inv
        c = jnp.cos(ang)
        s = jnp.sin(ang)
        first = r < half
        second = (r >= half) & (r < 2 * half)
        c_ref[...] = jnp.where(first | second, c, 1.0)
        s_ref[...] = jnp.where(second, s, jnp.where(first, -s, 0.0))

    tables(HEAD_DIM, HEAD_ROT_HALF, ch_ref, sh_ref)
    tables(IDX_DIM, IDX_ROT_HALF, ci_ref, si_ref)


def _swap_matrix(period, half):
    r = lax.broadcasted_iota(I32, (LANES, LANES), 0)
    c = lax.broadcasted_iota(I32, (LANES, LANES), 1)
    pos = c & (period - 1)
    src = jnp.where(pos < half, c + half, c - half)
    return jnp.where((pos < 2 * half) & (r == src), 1.0, 0.0).astype(BF16)


def _rope_tables(positions):
    t = positions.size
    tp = 512
    tab = jax.ShapeDtypeStruct((t, LANES), F32)
    spec = pl.BlockSpec((tp, LANES), lambda i: (i, 0))
    return pl.pallas_call(
        _rope_table_kernel,
        grid=(t // tp,),
        in_specs=[pl.BlockSpec((tp, 1), lambda i: (i, 0))],
        out_specs=[spec] * 4,
        out_shape=[tab] * 4,
        compiler_params=_cparams(("parallel",)),
        name="rope_tables",
    )(positions.reshape(t, 1))


def _rope(t, c, s, swap):
    hi = t.astype(BF16)
    lo = (t - hi.astype(F32)).astype(BF16)
    return t * c + (_dot(hi, swap) + _dot(lo, swap)) * s


PROJ_TN = 1280


def _proj_kernel(x_ref, g_ref, wh_ref, wt_ref, ws_ref, o_ref, os_ref, h_ref):
    j = pl.program_id(1)

    @pl.when(j == 0)
    def _():
        h = _rms(x_ref[...], g_ref[...]).astype(BF16)
        h_ref[...] = h
        os_ref[...] = _dot(h, ws_ref[...])

    @pl.when(j < P_HEAD // PROJ_TN)
    def _():
        o_ref[...] = _dot(h_ref[...], wh_ref[...]).astype(o_ref.dtype)

    @pl.when(j >= P_HEAD // PROJ_TN)
    def _():
        o_ref[...] = _dot(h_ref[...], wt_ref[...]).astype(o_ref.dtype)


def _proj(x, g, wh, wt, ws, l):
    t = x.shape[0]
    tm, tn = 1024, PROJ_TN
    n_head = P_HEAD // tn
    return pl.pallas_call(
        _proj_kernel,
        grid=(t // tm, P_WIDTH // tn),
        in_specs=[
            pl.BlockSpec((tm, D_MODEL), lambda i, j: (i, 0)),
            _lp(l, (1, D_MODEL), lambda i, j: (0, 0)),
            _lp(l, (D_MODEL, tn), lambda i, j: (0, jnp.minimum(j, n_head - 1))),
            _lp(l, (D_MODEL, tn), lambda i, j: (0, jnp.maximum(j - n_head, 0))),
            _lp(l, (D_MODEL, PS_WIDTH), lambda i, j: (0, 0)),
        ],
        out_specs=[
            pl.BlockSpec((tm, tn), lambda i, j: (i, j)),
            pl.BlockSpec((tm, PS_WIDTH), lambda i, j: (i, 0)),
        ],
        out_shape=[
            jax.ShapeDtypeStruct((t, P_WIDTH), BF16),
            jax.ShapeDtypeStruct((t, PS_WIDTH), F32),
        ],
        scratch_shapes=[pltpu.VMEM((tm, D_MODEL), BF16)],
        compiler_params=_cparams(("parallel", "arbitrary")),
        name="proj",
    )(x, g, wh, wt, ws)


def _prep_kernel(q_ref, iql_ref, iqh_ref, kv_ref, ps_ref, ch_ref, sh_ref, ci_ref, si_ref,
                 qg_ref, kg_ref, qn_ref, iqr_ref, kn_ref, vt_ref, ika_ref, ikb_ref, iwt_ref):
    ch, sh = ch_ref[...], sh_ref[...]
    ci, si = ci_ref[...], si_ref[...]
    swap_h = _swap_matrix(HEAD_DIM, HEAD_ROT_HALF)
    swap_i = _swap_matrix(IDX_DIM, IDX_ROT_HALF)
    scale = (HEAD_DIM ** -0.5) * LOG2E
    half_groups = N_HEADS // 2
    for h in range(N_HEADS):
        sl = slice(h * HEAD_DIM, (h + 1) * HEAD_DIM)
        qh = _rms(q_ref[:, sl].astype(F32), qg_ref[...])
        qn_ref[:, sl] = (_rope(qh, ch, sh, swap_h) * scale).astype(BF16)
        src = iql_ref if h < half_groups else iqh_ref
        hs = h % half_groups
        iqh = src[:, hs * LANES:(hs + 1) * LANES].astype(F32)
        iqr_ref[:, sl] = _rope(iqh, ci, si, swap_i).astype(BF16)
    for g in range(N_KV_HEADS):
        sl = slice(g * HEAD_DIM, (g + 1) * HEAD_DIM)
        kh = _rms(kv_ref[:, sl].astype(F32), kg_ref[...])
        kn_ref[:, sl] = _rope(kh, ch, sh, swap_h).astype(BF16)
        vh = kv_ref[:, KV_DIM + g * HEAD_DIM:KV_DIM + (g + 1) * HEAD_DIM].astype(F32)
        vt_ref[0, sl, :] = vh.T.astype(BF16)
    ps = ps_ref[...]
    lane = lax.broadcasted_iota(I32, (1, LANES), 1)
    ikr = jnp.where(lane < IDX_DIM, _rope(ps, ci, si, swap_i), 0.0)
    ika_ref[...] = ikr.astype(BF16)
    ikb_ref[...] = pltpu.roll(ikr, IDX_DIM, 1).astype(BF16)
    idx_scale = (IDX_DIM ** -0.5) * (IDX_HEADS ** -0.5)
    iws = jnp.where(lane < IDX_HEADS, pltpu.roll(ps, LANES - IDX_DIM, 1) * idx_scale, 0.0)
    iwt_ref[...] = iws.T


def _prep(p, ps, tabs, qg, kg, l):
    t = p.shape[0]
    tp = KC
    half_iq = IDX_HEADS * IDX_DIM // 2
    tab_spec = pl.BlockSpec((tp, LANES), lambda i: (i, 0))
    vec_spec = _lp(l, (1, HEAD_DIM), lambda i: (0, 0))

    def out(width, dtype):
        return (pl.BlockSpec((tp, width), lambda i: (i, 0)), jax.ShapeDtypeStruct((t, width), dtype))

    outs = [out(ATTN_DIM, BF16), out(ATTN_DIM, BF16), out(KV_DIM, BF16),
            (pl.BlockSpec((1, KV_DIM, tp), lambda i: (i, 0, 0)),
             jax.ShapeDtypeStruct((t // tp, KV_DIM, tp), BF16)),
            out(LANES, BF16), out(LANES, BF16),
            (pl.BlockSpec((LANES, tp), lambda i: (0, i)), jax.ShapeDtypeStruct((LANES, t), F32))]
    return pl.pallas_call(
        _prep_kernel,
        grid=(t // tp,),
        in_specs=[
            pl.BlockSpec((tp, ATTN_DIM), lambda i: (i, P_Q // ATTN_DIM)),
            pl.BlockSpec((tp, half_iq), lambda i: (i, P_IQ // half_iq)),
            pl.BlockSpec((tp, half_iq), lambda i: (i, P_IQ // half_iq + 1)),
            pl.BlockSpec((tp, 2 * KV_DIM), lambda i: (i, P_K // (2 * KV_DIM))),
            tab_spec,
        ] + [tab_spec] * 4 + [vec_spec, vec_spec],
        out_specs=[o[0] for o in outs],
        out_shape=[o[1] for o in outs],
        compiler_params=_cparams(("parallel",)),
        name="prep",
    )(p, p, p, p, ps, *tabs, qg, kg)


def _sortable(x):
    bits = lax.bitcast_convert_type(x, I32)
    return bits ^ ((bits >> 31) & 0x7FFFFFFF)


N_PAIRS = N_HEADS // 2
N_PART = 4


def _dsa_kernel(qn_ref, iq_ref, iwt_ref, kn_ref, vt_ref, ika_ref, ikb_ref, o_ref,
                key_ref, thr_ref, nge_ref, bias_ref, m_ref, l_ref, acc_ref, sa_ref, sb_ref, p_ref):
    j = pl.program_id(1)
    nck = j // (KC // QB) + 1
    seq = kn_ref.shape[0]

    iwt = iwt_ref[...]
    q_pos = j * QB + lax.broadcasted_iota(I32, (SUB, QB), 1)
    n_grp = IDX_HEADS // 2

    def score_chunk(c, carry):
        for hf in range(KC // SUB):
            start = pl.multiple_of(c * KC + hf * SUB, SUB)
            ka = ika_ref[pl.ds(start, SUB), :]
            kb = ikb_ref[pl.ds(start, SUB), :]
            acc = jnp.zeros((SUB, QB), F32)
            for a in range(n_grp // 2):
                b = a + n_grp // 2
                iq2 = jnp.concatenate([iq_ref[:, a * LANES:(a + 1) * LANES],
                                       iq_ref[:, b * LANES:(b + 1) * LANES]], axis=0)
                sa = _dot_nt(ka, iq2)
                sb = _dot_nt(kb, iq2)
                acc = acc + jnp.maximum(sa[:, :QB], 0.0) * iwt[2 * a:2 * a + 1, :]
                acc = acc + jnp.maximum(sb[:, :QB], 0.0) * iwt[2 * a + 1:2 * a + 2, :]
                acc = acc + jnp.maximum(sa[:, QB:], 0.0) * iwt[2 * b:2 * b + 1, :]
                acc = acc + jnp.maximum(sb[:, QB:], 0.0) * iwt[2 * b + 1:2 * b + 2, :]
            k_pos = start + lax.broadcasted_iota(I32, (SUB, QB), 0)
            acc = jnp.where(k_pos <= q_pos, acc, -jnp.inf)
            key_ref[c, hf * SUB:(hf + 1) * SUB, :] = _sortable(acc)
        return carry

    lax.fori_loop(0, nck, score_chunk, 0)

    n_loaded = jnp.broadcast_to(nck * KC, (SUBLANES, QB)).astype(I32)

    def search(nc):
        def search_pass(p, carry):
            tu, n_at = carry
            cand_u = tu | lax.shift_left(jnp.int32(1), 31 - p)
            cand = cand_u ^ INT_MIN
            parts = [jnp.zeros((SUBLANES, QB), I32)] * N_PART
            for c in range(nc):
                for r in range(KC // SUBLANES):
                    k = key_ref[c, r * SUBLANES:(r + 1) * SUBLANES, :]
                    parts[r % N_PART] = parts[r % N_PART] + jnp.where(k >= cand, 1, 0)
            tot = (parts[0] + parts[1]) + (parts[2] + parts[3])
            n = jnp.broadcast_to(jnp.sum(tot, axis=0, keepdims=True), (SUBLANES, QB))
            ok = n >= TOPK
            return jnp.where(ok, cand_u, tu), jnp.where(ok, n, n_at)

        tu, n_ge = lax.fori_loop(0, 32, search_pass, (jnp.zeros((SUBLANES, QB), I32), n_loaded))
        thr_ref[...] = tu ^ INT_MIN
        nge_ref[...] = n_ge

    for nc in range(1, key_ref.shape[0] + 1):
        pl.when(nck == nc)(functools.partial(search, nc))
    thr = thr_ref[...]
    n_ge = nge_ref[...]

    row_iota = lax.broadcasted_iota(I32, (SUBLANES, QB), 0)
    tied = (n_ge > TOPK) & (thr > KEY_NEG_INF)
    any_tied = jnp.max(jnp.where(tied, 1, 0)) > 0

    def count32(pred):
        def chunk(c, parts):
            parts = list(parts)
            for r in range(KC // SUBLANES):
                k = key_ref[c, r * SUBLANES:(r + 1) * SUBLANES, :]
                hit = pred(k, c * KC + r * SUBLANES)
                parts[r % N_PART] = parts[r % N_PART] + jnp.where(hit, 1, 0)
            return tuple(parts)

        zero = jnp.zeros((SUBLANES, QB), I32)
        parts = lax.fori_loop(0, nck, chunk, (zero,) * N_PART)
        tot = (parts[0] + parts[1]) + (parts[2] + parts[3])
        return jnp.broadcast_to(jnp.sum(tot, axis=0, keepdims=True), (SUBLANES, QB))

    thr_fin = jnp.maximum(thr, KEY_NEG_INF + 1)

    @pl.when(jnp.logical_not(any_tied))
    def _():
        def bias_chunk(c, carry):
            for r in range(KC // SUBLANES):
                sl = slice(r * SUBLANES, (r + 1) * SUBLANES)
                k = key_ref[c, sl, :]
                sel = (k >= thr_fin) & (k < KEY_POS_INF)
                bias_ref[c, sl, :] = jnp.where(sel, 0.0, NEG_BIG)
            return carry

        lax.fori_loop(0, nck, bias_chunk, 0)

    @pl.when(any_tied)
    def _():
        need = TOPK - count32(lambda k, _: k > thr)

        def pos_pass(p, x):
            cand = x | lax.shift_left(jnp.int32(1), (seq.bit_length() - 2) - p)
            n = count32(lambda k, r0: (k == thr) & (row_iota + r0 < cand))
            return jnp.where(n < need, cand, x)

        xlim = lax.fori_loop(0, seq.bit_length() - 1, pos_pass, jnp.zeros((SUBLANES, QB), I32))

        def bias_chunk(c, carry):
            for r in range(KC // SUBLANES):
                sl = slice(r * SUBLANES, (r + 1) * SUBLANES)
                k = key_ref[c, sl, :]
                pos = row_iota + (c * KC + r * SUBLANES)
                sel = (k > thr) | ((k == thr) & (pos <= xlim))
                sel = sel & (k >= thr_fin) & (k < KEY_POS_INF)
                bias_ref[c, sl, :] = jnp.where(sel, 0.0, NEG_BIG)
            return carry

        lax.fori_loop(0, nck, bias_chunk, 0)

    m_ref[...] = jnp.full(m_ref.shape, NEG_BIG, F32)
    l_ref[...] = jnp.zeros(l_ref.shape, F32)
    acc_ref[...] = jnp.zeros(acc_ref.shape, F32)
    group = N_HEADS // N_KV_HEADS

    def logits(c, s_ref):
        col_max = []
        for pr in range(N_PAIRS):
            g = (2 * pr) // group
            q2 = jnp.concatenate([qn_ref[:, (2 * pr) * HEAD_DIM:(2 * pr + 1) * HEAD_DIM],
                                  qn_ref[:, (2 * pr + 1) * HEAD_DIM:(2 * pr + 2) * HEAD_DIM]],
                                 axis=0)
            mx = jnp.full((SUBLANES, 2 * QB), NEG_BIG, F32)
            for hf in range(KC // SUB):
                start = pl.multiple_of(c * KC + hf * SUB, SUB)
                kc = kn_ref[pl.ds(start, SUB), g * HEAD_DIM:(g + 1) * HEAD_DIM]
                bias = bias_ref[c, hf * SUB:(hf + 1) * SUB, :]
                s = _dot_nt(kc, q2) + jnp.concatenate([bias, bias], axis=1)
                s_ref[pr, hf * SUB:(hf + 1) * SUB, :] = s
                for r in range(SUB // SUBLANES):
                    mx = jnp.maximum(mx, s[r * SUBLANES:(r + 1) * SUBLANES, :])
            col_max.append(jnp.broadcast_to(jnp.max(mx, axis=0, keepdims=True), (SUBLANES, 2 * QB)))
        return tuple(col_max)

    def finish(c, s_ref, col_max):
        alphas = []
        for pr in range(N_PAIRS):
            m_prev = m_ref[pr]
            m_new = jnp.maximum(m_prev, col_max[pr])
            alpha = jnp.exp2(m_prev - m_new)
            lsum = jnp.zeros((SUBLANES, 2 * QB), F32)
            for t in range(KC // SLAB):
                rows = slice(t * SLAB, (t + 1) * SLAB)
                p = jnp.exp2(s_ref[pr, rows, :] - m_new[0:1, :])
                for r in range(SLAB // SUBLANES):
                    lsum = lsum + p[r * SUBLANES:(r + 1) * SUBLANES, :]
                p_ref[pr, rows, :] = p.astype(BF16)
            m_ref[pr] = m_new
            l_ref[pr] = alpha * l_ref[pr] + jnp.sum(lsum, axis=0, keepdims=True)
            alphas.append(alpha)
        for pr in range(N_PAIRS):
            g = (2 * pr) // group
            vt = vt_ref[c, g * HEAD_DIM:(g + 1) * HEAD_DIM, :]
            acc_ref[pr] = acc_ref[pr] * alphas[pr][0:1, :] + _dot(vt, p_ref[pr])

    def attn_step(i, col_max):
        c = 2 * i
        mid = logits(c + 1, sb_ref)
        finish(c, sa_ref, col_max)
        nxt = logits(c + 2, sa_ref)
        finish(c + 1, sb_ref, mid)
        return nxt

    n_steps = (nck - 1) // 2
    tail = 2 * n_steps
    col_max = lax.fori_loop(0, n_steps, attn_step, logits(0, sa_ref))

    @pl.when(tail == nck - 1)
    def _():
        finish(tail, sa_ref, col_max)

    @pl.when(tail != nck - 1)
    def _():
        mid = logits(tail + 1, sb_ref)
        finish(tail, sa_ref, col_max)
        finish(tail + 1, sb_ref, mid)

    for pr in range(N_PAIRS):
        o_t = acc_ref[pr] / l_ref[pr][0:1, :]
        for i in range(2):
            h = 2 * pr + i
            o_ref[:, h * HEAD_DIM:(h + 1) * HEAD_DIM] = o_t[:, i * QB:(i + 1) * QB].T.astype(BF16)


def _dsa(qn, iqr, iwt, kn, vt, ika, ikb, batch, seq):
    t = qn.shape[0]
    nb = seq // QB
    nchunk = seq // KC
    qmap = lambda b, j: (b * nb + j, 0)
    bmap = lambda b, j: (b, 0)
    return pl.pallas_call(
        _dsa_kernel,
        grid=(batch, nb),
        in_specs=[
            pl.BlockSpec((QB, ATTN_DIM), qmap),
            pl.BlockSpec((QB, ATTN_DIM), qmap),
            pl.BlockSpec((IDX_HEADS, QB), lambda b, j: (0, b * nb + j)),
            pl.BlockSpec((seq, KV_DIM), bmap),
            pl.BlockSpec((nchunk, KV_DIM, KC), lambda b, j: (b, 0, 0)),
            pl.BlockSpec((seq, LANES), bmap),
            pl.BlockSpec((seq, LANES), bmap),
        ],
        out_specs=pl.BlockSpec((QB, ATTN_DIM), qmap),
        out_shape=jax.ShapeDtypeStruct((t, ATTN_DIM), BF16),
        scratch_shapes=[
            pltpu.VMEM((nchunk, KC, QB), I32),
            pltpu.VMEM((SUBLANES, QB), I32),
            pltpu.VMEM((SUBLANES, QB), I32),
            pltpu.VMEM((nchunk, KC, QB), F32),
            pltpu.VMEM((N_PAIRS, SUBLANES, 2 * QB), F32),
            pltpu.VMEM((N_PAIRS, SUBLANES, 2 * QB), F32),
            pltpu.VMEM((N_PAIRS, HEAD_DIM, 2 * QB), F32),
            pltpu.VMEM((N_PAIRS, KC, 2 * QB), F32),
            pltpu.VMEM((N_PAIRS, KC, 2 * QB), F32),
            pltpu.VMEM((N_PAIRS, KC, 2 * QB), BF16),
        ],
        compiler_params=_cparams(("parallel", "arbitrary")),
        name="dsa",
    )(qn, iqr, iwt, kn, vt, ika, ikb)


def _mem_kv_kernel(mem_ref, g_ref, w_ref, kg_ref, mk_ref, mv_ref):
    m = _rms(mem_ref[...], g_ref[...]).astype(BF16)
    kv = _dot(m, w_ref[...])
    for h in range(MEM_HEADS):
        sl = slice(h * HEAD_DIM, (h + 1) * HEAD_DIM)
        mk_ref[:, sl] = _rms(kv[:, sl], kg_ref[...]).astype(BF16)
    mv_ref[...] = kv[:, MEM_DIM:].astype(BF16)


def _mem_kv(mem2d, g, w, kg, batch, l):
    n = mem2d.shape[0]
    m = n // batch
    out = jax.ShapeDtypeStruct((n, MEM_DIM), BF16)
    ospec = pl.BlockSpec((m, MEM_DIM), lambda b: (b, 0))
    return pl.pallas_call(
        _mem_kv_kernel,
        grid=(batch,),
        in_specs=[
            pl.BlockSpec((m, D_MODEL), lambda b: (b, 0)),
            _lp(l, (1, D_MODEL), lambda b: (0, 0)),
            _lp(l, (D_MODEL, 2 * MEM_DIM), lambda b: (0, 0)),
            _lp(l, (1, HEAD_DIM), lambda b: (0, 0)),
        ],
        out_specs=[ospec, ospec],
        out_shape=[out, out],
        compiler_params=_cparams(("parallel",)),
        name="mem_kv",
    )(mem2d, g, w, kg)


def _mem_attn_kernel(q_ref, mk_ref, mv_ref, qg_ref, o_ref):
    scale = HEAD_DIM ** -0.5
    for h in range(MEM_HEADS):
        sl = slice(h * HEAD_DIM, (h + 1) * HEAD_DIM)
        qh = (_rms(q_ref[:, sl].astype(F32), qg_ref[...]) * scale).astype(BF16)
        s = _dot_nt(qh, mk_ref[:, sl])
        p = jnp.exp(s - jnp.max(s, axis=1, keepdims=True))
        l = jnp.sum(p, axis=1, keepdims=True)
        o = _dot((p / l).astype(BF16), mv_ref[:, sl])
        o_ref[:, sl] = o.astype(BF16)


def _mem_attn(p, mk, mv, qg, batch, seq, l):
    t = p.shape[0]
    tq = 512
    nq = seq // tq
    m = mk.shape[0] // batch
    return pl.pallas_call(
        _mem_attn_kernel,
        grid=(batch, nq),
        in_specs=[
            pl.BlockSpec((tq, MEM_DIM), lambda b, i: (b * nq + i, P_MQ // MEM_DIM)),
            pl.BlockSpec((m, MEM_DIM), lambda b, i: (b, 0)),
            pl.BlockSpec((m, MEM_DIM), lambda b, i: (b, 0)),
            _lp(l, (1, HEAD_DIM), lambda b, i: (0, 0)),
        ],
        out_specs=pl.BlockSpec((tq, MEM_DIM), lambda b, i: (b * nq + i, 0)),
        out_shape=jax.ShapeDtypeStruct((t, MEM_DIM), BF16),
        compiler_params=_cparams(("parallel", "parallel")),
        name="mem_attn",
    )(p, mk, mv, qg)


CONV_HALO = 32
CONV_ROWS = 64


def _conv_kernel(cur_a_ref, cur_g_ref, prev_a_ref, prev_g_ref, inb_ref, cw_ref, cb_ref, lg_ref,
                 lb_ref, o_ref, u_ref, sh_ref):
    ts = cur_a_ref.shape[0]
    n = CONV_HALO + ts

    def glu(a, g):
        a = a.astype(F32) + inb_ref[:, :CONV_CH]
        g = g.astype(F32) + inb_ref[:, CONV_CH:]
        return a * jax.nn.sigmoid(g)

    u_prev = glu(prev_a_ref[ts - CONV_HALO:, :], prev_g_ref[ts - CONV_HALO:, :])
    u_ref[:CONV_HALO, :] = jnp.where(pl.program_id(1) == 0, 0.0, u_prev)
    u_ref[CONV_HALO:n, :] = glu(cur_a_ref[...], cur_g_ref[...])
    u_ref[n:, :] = jnp.zeros((SUBLANES, CONV_CH), F32)
    for r in range(1, SUBLANES):
        sh_ref[r - 1] = u_ref[r:r + n, :]

    first_tap = CONV_HALO - (CONV_WIDTH - 1)
    for t in range(ts // CONV_ROWS):
        base = t * CONV_ROWS
        y = jnp.broadcast_to(cb_ref[...], (CONV_ROWS, CONV_CH))
        for w in range(CONV_WIDTH):
            r = (first_tap + w) % SUBLANES
            a = base + first_tap + w - r
            rows = u_ref[a:a + CONV_ROWS, :] if r == 0 else sh_ref[r - 1, a:a + CONV_ROWS, :]
            y = y + rows * cw_ref[w:w + 1, :]
        mu = jnp.mean(y, axis=-1, keepdims=True)
        d = y - mu
        var = jnp.mean(d * d, axis=-1, keepdims=True)
        z = d * lax.rsqrt(var + EPS) * lg_ref[...] + lb_ref[...]
        o_ref[base:base + CONV_ROWS, :] = (z * jax.nn.sigmoid(z)).astype(BF16)


def _conv(p, inb, cw, cb, lg, lb, batch, seq, l):
    t = p.shape[0]
    ts = 512
    ns = seq // ts
    a_blk = P_GLU // CONV_CH
    vec = lambda w: _lp(l, (1, w), lambda b, i: (0, 0))
    cur = lambda k: pl.BlockSpec((ts, CONV_CH), lambda b, i: (b * ns + i, a_blk + k))
    prev = lambda k: pl.BlockSpec((ts, CONV_CH), lambda b, i: (b * ns + jnp.maximum(i - 1, 0), a_blk + k))
    return pl.pallas_call(
        _conv_kernel,
        grid=(batch, ns),
        in_specs=[
            cur(0), cur(1), prev(0), prev(1),
            vec(2 * CONV_CH),
            _lp(l, (CONV_WIDTH, CONV_CH), lambda b, i: (0, 0)),
            vec(CONV_CH), vec(CONV_CH), vec(CONV_CH),
        ],
        out_specs=pl.BlockSpec((ts, CONV_CH), lambda b, i: (b * ns + i, 0)),
        out_shape=jax.ShapeDtypeStruct((t, CONV_CH), BF16),
        scratch_shapes=[pltpu.VMEM((CONV_HALO + ts + SUBLANES, CONV_CH), F32),
                        pltpu.VMEM((SUBLANES - 1, CONV_HALO + ts, CONV_CH), F32)],
        compiler_params=_cparams(("parallel", "parallel")),
        name="conv",
    )(p, p, p, p, inb, cw, cb, lg, lb)


MERGE_TN = 512


def _merge_kernel(a_ref, m_ref, c_ref, g0_ref, g1_ref, g2_ref, gb0_ref, gb1_ref, gb2_ref,
                  wa_ref, wm_ref, wc_ref, cob_ref, o_ref):
    a, m, c = a_ref[...], m_ref[...], c_ref[...]
    for n in range(D_MODEL // MERGE_TN):
        sl = slice(n * MERGE_TN, (n + 1) * MERGE_TN)

        def gate(g_ref, gb_ref):
            return jax.nn.sigmoid(g_ref[:, sl].astype(F32) + gb_ref[:, sl])

        y = gate(g0_ref, gb0_ref) * _dot(a, wa_ref[:, sl])
        y = y + gate(g1_ref, gb1_ref) * _dot(m, wm_ref[:, sl])
        y = y + gate(g2_ref, gb2_ref) * (_dot(c, wc_ref[:, sl]) + cob_ref[:, sl])
        o_ref[:, sl] = y.astype(BF16)


def _merge(attn, memo, convo, p, gate_b, wa, wm, wc, cob, l):
    t = attn.shape[0]
    tm = 512
    gblk = P_GATES // D_MODEL
    row = lambda w: pl.BlockSpec((tm, w), lambda i: (i, 0))
    full = lambda r, c: _lp(l, (r, c), lambda i: (0, 0))
    gspec = lambda k: pl.BlockSpec((tm, D_MODEL), lambda i: (i, gblk + k))
    gbspec = lambda k: _lp(l, (1, D_MODEL), lambda i: (0, k))
    return pl.pallas_call(
        _merge_kernel,
        grid=(t // tm,),
        in_specs=[row(ATTN_DIM), row(MEM_DIM), row(CONV_CH),
                  gspec(0), gspec(1), gspec(2), gbspec(0), gbspec(1), gbspec(2),
                  full(ATTN_DIM, D_MODEL), full(MEM_DIM, D_MODEL), full(CONV_CH, D_MODEL),
                  full(1, D_MODEL)],
        out_specs=row(D_MODEL),
        out_shape=jax.ShapeDtypeStruct((t, D_MODEL), BF16),
        compiler_params=_cparams(("parallel",)),
        name="merge",
    )(attn, memo, convo, p, p, p, gate_b, gate_b, gate_b, wa, wm, wc, cob)


def _outproj_kernel(x_ref, mg_ref, w_ref, g_ref, x1_ref, h2_ref):
    x1 = x_ref[...] + _dot(mg_ref[...], w_ref[...])
    x1_ref[...] = x1
    h2_ref[...] = _rms(x1, g_ref[...]).astype(BF16)


def _outproj(x, merged, w, g, l):
    t = x.shape[0]
    tm = 512
    row = pl.BlockSpec((tm, D_MODEL), lambda i: (i, 0))
    return pl.pallas_call(
        _outproj_kernel,
        grid=(t // tm,),
        in_specs=[row, row,
                  _lp(l, (D_MODEL, D_MODEL), lambda i: (0, 0)),
                  _lp(l, (1, D_MODEL), lambda i: (0, 0))],
        out_specs=[row, row],
        out_shape=[jax.ShapeDtypeStruct((t, D_MODEL), F32), jax.ShapeDtypeStruct((t, D_MODEL), BF16)],
        compiler_params=_cparams(("parallel",)),
        name="outproj",
    )(x, merged, w, g)


def _mlp_kernel(h_ref, x_ref, wu_ref, wd_ref, o_ref):
    @pl.when(pl.program_id(1) == 0)
    def _():
        o_ref[...] = x_ref[...]

    u = jnp.maximum(_dot(h_ref[...], wu_ref[...]), 0.0)
    o_ref[...] += _dot((u * u).astype(BF16), wd_ref[...])


def _mlp(h2, x1, wu, wd, l):
    t = x1.shape[0]
    tm, tf = 512, 1024
    return pl.pallas_call(
        _mlp_kernel,
        grid=(t // tm, FFN_DIM // tf),
        in_specs=[
            pl.BlockSpec((tm, D_MODEL), lambda i, f: (i, 0)),
            pl.BlockSpec((tm, D_MODEL), lambda i, f: (i, 0)),
            _lp(l, (D_MODEL, tf), lambda i, f: (0, f)),
            _lp(l, (tf, D_MODEL), lambda i, f: (f, 0)),
        ],
        out_specs=pl.BlockSpec((tm, D_MODEL), lambda i, f: (i, 0)),
        out_shape=jax.ShapeDtypeStruct((t, D_MODEL), F32),
        compiler_params=_cparams(("parallel", "arbitrary")),
        name="mlp",
    )(h2, x1, wu, wd)


def _split_w_in(w):
    narrow = IDX_DIM + IDX_HEADS
    wb = w.astype(BF16)
    head = wb[:, :, :P_HEAD]
    tail = wb[:, :, P_HEAD + narrow:]
    small = jnp.pad(wb[:, :, P_HEAD:P_HEAD + narrow], ((0, 0), (0, 0), (0, PS_WIDTH - narrow)))
    return head, tail, small


def kernel(x, mem, positions, norm1_g, w_in, q_norm_g, k_norm_g, mem_norm_g, w_mem_kv, mq_norm_g,
           mk_norm_g, conv_in_b, conv_w, conv_b, conv_ln_g, conv_ln_b, gate_b, w_attn_o, w_mem_o,
           w_conv_o, conv_o_b, w_out, norm2_g, w_up, w_down):
    batch, seq, _ = x.shape
    depth = w_in.shape[0]
    t = batch * seq
    xf = x.reshape(t, D_MODEL)
    mem2d = mem.reshape(batch * mem.shape[1], D_MODEL)
    tabs = _rope_tables(positions)

    vec = lambda v: v.reshape(depth, 1, -1)
    norm1_g, q_norm_g, k_norm_g, mem_norm_g, mq_norm_g, mk_norm_g = map(
        vec, (norm1_g, q_norm_g, k_norm_g, mem_norm_g, mq_norm_g, mk_norm_g))
    conv_in_b, conv_b, conv_ln_g, conv_ln_b, gate_b, conv_o_b, norm2_g = map(
        vec, (conv_in_b, conv_b, conv_ln_g, conv_ln_b, gate_b, conv_o_b, norm2_g))
    w_head, w_tail, w_small = _split_w_in(w_in)
    w_mem_kv, w_attn_o, w_mem_o, w_conv_o, w_out, w_up, w_down = (
        w.astype(BF16) for w in (w_mem_kv, w_attn_o, w_mem_o, w_conv_o, w_out, w_up, w_down))

    for l in range(depth):
        p, ps = _proj(xf, norm1_g, w_head, w_tail, w_small, l)
        qn, iqr, kn, vt, ika, ikb, iwt = _prep(p, ps, tabs, q_norm_g, k_norm_g, l)
        attn = _dsa(qn, iqr, iwt, kn, vt, ika, ikb, batch, seq)
        mk, mv = _mem_kv(mem2d, mem_norm_g, w_mem_kv, mk_norm_g, batch, l)
        memo = _mem_attn(p, mk, mv, mq_norm_g, batch, seq, l)
        convo = _conv(p, conv_in_b, conv_w, conv_b, conv_ln_g, conv_ln_b, batch, seq, l)
        merged = _merge(attn, memo, convo, p, gate_b, w_attn_o, w_mem_o, w_conv_o, conv_o_b, l)
        x1, h2 = _outproj(xf, merged, w_out, norm2_g, l)
        xf = _mlp(h2, x1, w_up, w_down, l)
    return xf.reshape(batch, seq, D_MODEL)
```

```python
import functools
import math

import jax
import jax.numpy as jnp
from jax import lax
from jax.experimental import pallas as pl
from jax.experimental.pallas import tpu as pltpu

F32 = jnp.float32
BF16 = jnp.bfloat16
I32 = jnp.int32

D_MODEL = 2048
HEAD_DIM = 128
N_HEADS = 8
N_KV_HEADS = 2
IDX_HEADS = 16
IDX_DIM = 64
TOPK = 256
CONV_CH = 512
CONV_WIDTH = 31
MEM_HEADS = 4
FFN_DIM = 4 * D_MODEL
ROPE_THETA = 500000.0
N_BRANCH = 3
EPS = 1e-6
ATTN_DIM = N_HEADS * HEAD_DIM
KV_DIM = N_KV_HEADS * HEAD_DIM
MEM_DIM = MEM_HEADS * HEAD_DIM
HEAD_ROT_HALF = HEAD_DIM // 8
IDX_ROT_HALF = IDX_DIM // 8

LANES = 128
VMEM_LIMIT = 56 * 1024 * 1024

P_Q = 0
P_K = P_Q + ATTN_DIM
P_V = P_K + KV_DIM
P_IQ = P_V + KV_DIM
P_HEAD = P_IQ + IDX_HEADS * IDX_DIM
P_GLU = P_HEAD
P_MQ = P_GLU + 2 * CONV_CH
P_GATES = P_MQ + MEM_DIM
P_WIDTH = P_GATES + N_BRANCH * D_MODEL
P_TAIL = P_WIDTH - P_HEAD
PS_WIDTH = LANES

QB = 128
KC = 512
SUB = 256
SLAB = 32
SUBLANES = 8
LOG2E = 1.4426950408889634
NEG_BIG = -1e30
INT_MIN = -2 ** 31
KEY_NEG_INF = -2139095041
KEY_POS_INF = 2139095040


def _cparams(sem, vmem=VMEM_LIMIT):
    return pltpu.CompilerParams(dimension_semantics=sem, vmem_limit_bytes=vmem)


def _lp(l, shape, imap):
    return pl.BlockSpec((pl.Squeezed(),) + shape, lambda *g: (l,) + imap(*g))


def _dot(a, b):
    return jnp.dot(a, b, preferred_element_type=F32)


def _dot_nt(a, b):
    return lax.dot_general(a, b, (((1,), (1,)), ((), ())), preferred_element_type=F32)


def _rms(xf, g):
    return xf * lax.rsqrt(jnp.mean(xf * xf, axis=-1, keepdims=True) + EPS) * g


def _rope_table_kernel(pos_ref, ch_ref, sh_ref, ci_ref, si_ref):
    pos = pos_ref[...].astype(F32)
    lane = lax.broadcasted_iota(I32, (1, LANES), 1)

    def tables(period, half, c_ref, s_ref):
        r = lane & (period - 1)
        fi = (r & (half - 1)).astype(F32)
        inv = jnp.exp(fi * (-math.log(ROPE_THETA) / half))
        ang = pos * inv
        c = jnp.cos(ang)
        s = jnp.sin(ang)
        first = r < half
        second = (r >= half) & (r < 2 * half)
        c_ref[...] = jnp.where(first | second, c, 1.0)
        s_ref[...] = jnp.where(second, s, jnp.where(first, -s, 0.0))

    tables(HEAD_DIM, HEAD_ROT_HALF, ch_ref, sh_ref)
    tables(IDX_DIM, IDX_ROT_HALF, ci_ref, si_ref)


def _swap_matrix(period, half):
    r = lax.broadcasted_iota(I32, (LANES, LANES), 0)
    c = lax.broadcasted_iota(I32, (LANES, LANES), 1)
    pos = c & (period - 1)
    src = jnp.where(pos < half, c + half, c - half)
    return jnp.where((pos < 2 * half) & (r == src), 1.0, 0.0).astype(BF16)


def _rope_tables(positions):
    t = positions.size
    tp = 512
    tab = jax.ShapeDtypeStruct((t, LANES), F32)
    spec = pl.BlockSpec((tp, LANES), lambda i: (i, 0))
    return pl.pallas_call(
        _rope_table_kernel,
        grid=(t // tp,),
        in_specs=[pl.BlockSpec((tp, 1), lambda i: (i, 0))],
        out_specs=[spec] * 4,
        out_shape=[tab] * 4,
        compiler_params=_cparams(("parallel",)),
        name="rope_tables",
    )(positions.reshape(t, 1))


def _rope(t, c, s, swap):
    hi = t.astype(BF16)
    lo = (t - hi.astype(F32)).astype(BF16)
    return t * c + (_dot(hi, swap) + _dot(lo, swap)) * s


PROJ_TN = 1280


def _proj_kernel(x_ref, g_ref, wh_ref, wt_ref, ws_ref, o_ref, os_ref, h_ref):
    j = pl.program_id(1)

    @pl.when(j == 0)
    def _():
        h = _rms(x_ref[...], g_ref[...]).astype(BF16)
        h_ref[...] = h
        os_ref[...] = _dot(h, ws_ref[...])

    @pl.when(j < P_HEAD // PROJ_TN)
    def _():
        o_ref[...] = _dot(h_ref[...], wh_ref[...]).astype(o_ref.dtype)

    @pl.when(j >= P_HEAD // PROJ_TN)
    def _():
        o_ref[...] = _dot(h_ref[...], wt_ref[...]).astype(o_ref.dtype)


def _proj(x, g, wh, wt, ws, l):
    t = x.shape[0]
    tm, tn = 1024, PROJ_TN
    n_head = P_HEAD // tn
    return pl.pallas_call(
        _proj_kernel,
        grid=(t // tm, P_WIDTH // tn),
        in_specs=[
            pl.BlockSpec((tm, D_MODEL), lambda i, j: (i, 0)),
            _lp(l, (1, D_MODEL), lambda i, j: (0, 0)),
            _lp(l, (D_MODEL, tn), lambda i, j: (0, jnp.minimum(j, n_head - 1))),
            _lp(l, (D_MODEL, tn), lambda i, j: (0, jnp.maximum(j - n_head, 0))),
            _lp(l, (D_MODEL, PS_WIDTH), lambda i, j: (0, 0)),
        ],
        out_specs=[
            pl.BlockSpec((tm, tn), lambda i, j: (i, j)),
            pl.BlockSpec((tm, PS_WIDTH), lambda i, j: (i, 0)),
        ],
        out_shape=[
            jax.ShapeDtypeStruct((t, P_WIDTH), BF16),
            jax.ShapeDtypeStruct((t, PS_WIDTH), F32),
        ],
        scratch_shapes=[pltpu.VMEM((tm, D_MODEL), BF16)],
        compiler_params=_cparams(("parallel", "arbitrary")),
        name="proj",
    )(x, g, wh, wt, ws)


def _prep_kernel(q_ref, iql_ref, iqh_ref, kv_ref, ps_ref, ch_ref, sh_ref, ci_ref, si_ref,
                 qg_ref, kg_ref, qn_ref, iqr_ref, kn_ref, vt_ref, ika_ref, ikb_ref, iwt_ref):
    ch, sh = ch_ref[...], sh_ref[...]
    ci, si = ci_ref[...], si_ref[...]
    swap_h = _swap_matrix(HEAD_DIM, HEAD_ROT_HALF)
    swap_i = _swap_matrix(IDX_DIM, IDX_ROT_HALF)
    scale = (HEAD_DIM ** -0.5) * LOG2E
    half_groups = N_HEADS // 2
    for h in range(N_HEADS):
        sl = slice(h * HEAD_DIM, (h + 1) * HEAD_DIM)
        qh = _rms(q_ref[:, sl].astype(F32), qg_ref[...])
        qn_ref[:, sl] = (_rope(qh, ch, sh, swap_h) * scale).astype(BF16)
        src = iql_ref if h < half_groups else iqh_ref
        hs = h % half_groups
        iqh = src[:, hs * LANES:(hs + 1) * LANES].astype(F32)
        iqr_ref[:, sl] = _rope(iqh, ci, si, swap_i).astype(BF16)
    for g in range(N_KV_HEADS):
        sl = slice(g * HEAD_DIM, (g + 1) * HEAD_DIM)
        kh = _rms(kv_ref[:, sl].astype(F32), kg_ref[...])
        kn_ref[:, sl] = _rope(kh, ch, sh, swap_h).astype(BF16)
        vh = kv_ref[:, KV_DIM + g * HEAD_DIM:KV_DIM + (g + 1) * HEAD_DIM].astype(F32)
        vt_ref[0, sl, :] = vh.T.astype(BF16)
    ps = ps_ref[...]
    lane = lax.broadcasted_iota(I32, (1, LANES), 1)
    ikr = jnp.where(lane < IDX_DIM, _rope(ps, ci, si, swap_i), 0.0)
    ika_ref[...] = ikr.astype(BF16)
    ikb_ref[...] = pltpu.roll(ikr, IDX_DIM, 1).astype(BF16)
    idx_scale = (IDX_DIM ** -0.5) * (IDX_HEADS ** -0.5)
    iws = jnp.where(lane < IDX_HEADS, pltpu.roll(ps, LANES - IDX_DIM, 1) * idx_scale, 0.0)
    iwt_ref[...] = iws.T


def _prep(p, ps, tabs, qg, kg, l):
    t = p.shape[0]
    tp = KC
    half_iq = IDX_HEADS * IDX_DIM // 2
    tab_spec = pl.BlockSpec((tp, LANES), lambda i: (i, 0))
    vec_spec = _lp(l, (1, HEAD_DIM), lambda i: (0, 0))

    def out(width, dtype):
        return (pl.BlockSpec((tp, width), lambda i: (i, 0)), jax.ShapeDtypeStruct((t, width), dtype))

    outs = [out(ATTN_DIM, BF16), out(ATTN_DIM, BF16), out(KV_DIM, BF16),
            (pl.BlockSpec((1, KV_DIM, tp), lambda i: (i, 0, 0)),
             jax.ShapeDtypeStruct((t // tp, KV_DIM, tp), BF16)),
            out(LANES, BF16), out(LANES, BF16),
            (pl.BlockSpec((LANES, tp), lambda i: (0, i)), jax.ShapeDtypeStruct((LANES, t), F32))]
    return pl.pallas_call(
        _prep_kernel,
        grid=(t // tp,),
        in_specs=[
            pl.BlockSpec((tp, ATTN_DIM), lambda i: (i, P_Q // ATTN_DIM)),
            pl.BlockSpec((tp, half_iq), lambda i: (i, P_IQ // half_iq)),
            pl.BlockSpec((tp, half_iq), lambda i: (i, P_IQ // half_iq + 1)),
            pl.BlockSpec((tp, 2 * KV_DIM), lambda i: (i, P_K // (2 * KV_DIM))),
            tab_spec,
        ] + [tab_spec] * 4 + [vec_spec, vec_spec],
        out_specs=[o[0] for o in outs],
        out_shape=[o[1] for o in outs],
        compiler_params=_cparams(("parallel",)),
        name="prep",
    )(p, p, p, p, ps, *tabs, qg, kg)


def _sortable(x):
    bits = lax.bitcast_convert_type(x, I32)
    return bits ^ ((bits >> 31) & 0x7FFFFFFF)


N_PAIRS = N_HEADS // 2
N_PART = 4


def _dsa_kernel(qn_ref, iq_ref, iwt_ref, kn_ref, vt_ref, ika_ref, ikb_ref, o_ref,
                key_ref, thr_ref, nge_ref, bias_ref, m_ref, l_ref, acc_ref, sa_ref, sb_ref, p_ref):
    j = pl.program_id(1)
    nck = j // (KC // QB) + 1
    seq = kn_ref.shape[0]

    iwt = iwt_ref[...]
    q_pos = j * QB + lax.broadcasted_iota(I32, (SUB, QB), 1)
    n_grp = IDX_HEADS // 2

    halves = KC // SUB
    nhalf = j // (SUB // QB) + 1

    def score_half(c, hf):
        start = pl.multiple_of(c * KC + hf * SUB, SUB)
        ka = ika_ref[pl.ds(start, SUB), :]
        kb = ikb_ref[pl.ds(start, SUB), :]
        acc = jnp.zeros((SUB, QB), F32)
        for a in range(n_grp // 2):
            b = a + n_grp // 2
            iq2 = jnp.concatenate([iq_ref[:, a * LANES:(a + 1) * LANES],
                                   iq_ref[:, b * LANES:(b + 1) * LANES]], axis=0)
            sa = _dot_nt(ka, iq2)
            sb = _dot_nt(kb, iq2)
            acc = acc + jnp.maximum(sa[:, :QB], 0.0) * iwt[2 * a:2 * a + 1, :]
            acc = acc + jnp.maximum(sb[:, :QB], 0.0) * iwt[2 * a + 1:2 * a + 2, :]
            acc = acc + jnp.maximum(sa[:, QB:], 0.0) * iwt[2 * b:2 * b + 1, :]
            acc = acc + jnp.maximum(sb[:, QB:], 0.0) * iwt[2 * b + 1:2 * b + 2, :]
        k_pos = start + lax.broadcasted_iota(I32, (SUB, QB), 0)
        acc = jnp.where(k_pos <= q_pos, acc, -jnp.inf)
        key_ref[c, hf * SUB:(hf + 1) * SUB, :] = _sortable(acc)

    def score_chunk(c, carry):
        for hf in range(halves):
            score_half(c, hf)
        return carry

    lax.fori_loop(0, nhalf // halves, score_chunk, 0)

    @pl.when(nhalf % halves == 1)
    def _():
        score_half(nck - 1, 0)
        key_ref[nck - 1, SUB:, :] = jnp.full((KC - SUB, QB), INT_MIN, I32)

    n_loaded = jnp.broadcast_to(nhalf * SUB, (SUBLANES, QB)).astype(I32)

    def search(nh):
        def search_pass(p, carry):
            tu, n_at = carry
            cand_u = tu | lax.shift_left(jnp.int32(1), 31 - p)
            cand = cand_u ^ INT_MIN
            parts = [jnp.zeros((SUBLANES, QB), I32)] * N_PART
            for r in range(nh * SUB // SUBLANES):
                row = (r * SUBLANES) % KC
                k = key_ref[(r * SUBLANES) // KC, row:row + SUBLANES, :]
                parts[r % N_PART] = parts[r % N_PART] + jnp.where(k >= cand, 1, 0)
            tot = (parts[0] + parts[1]) + (parts[2] + parts[3])
            n = jnp.broadcast_to(jnp.sum(tot, axis=0, keepdims=True), (SUBLANES, QB))
            ok = n >= TOPK
            return jnp.where(ok, cand_u, tu), jnp.where(ok, n, n_at)

        tu, n_ge = lax.fori_loop(0, 32, search_pass, (jnp.zeros((SUBLANES, QB), I32), n_loaded))
        thr_ref[...] = tu ^ INT_MIN
        nge_ref[...] = n_ge

    for nh in range(1, key_ref.shape[0] * halves + 1):
        pl.when(nhalf == nh)(functools.partial(search, nh))
    thr = thr_ref[...]
    n_ge = nge_ref[...]

    row_iota = lax.broadcasted_iota(I32, (SUBLANES, QB), 0)
    tied = (n_ge > TOPK) & (thr > KEY_NEG_INF)
    any_tied = jnp.max(jnp.where(tied, 1, 0)) > 0

    def count32(pred):
        def chunk(c, parts):
            parts = list(parts)
            for r in range(KC // SUBLANES):
                k = key_ref[c, r * SUBLANES:(r + 1) * SUBLANES, :]
                hit = pred(k, c * KC + r * SUBLANES)
                parts[r % N_PART] = parts[r % N_PART] + jnp.where(hit, 1, 0)
            return tuple(parts)

        zero = jnp.zeros((SUBLANES, QB), I32)
        parts = lax.fori_loop(0, nck, chunk, (zero,) * N_PART)
        tot = (parts[0] + parts[1]) + (parts[2] + parts[3])
        return jnp.broadcast_to(jnp.sum(tot, axis=0, keepdims=True), (SUBLANES, QB))

    thr_fin = jnp.maximum(thr, KEY_NEG_INF + 1)

    @pl.when(jnp.logical_not(any_tied))
    def _():
        def bias_chunk(c, carry):
            for r in range(KC // SUBLANES):
                sl = slice(r * SUBLANES, (r + 1) * SUBLANES)
                k = key_ref[c, sl, :]
                sel = (k >= thr_fin) & (k < KEY_POS_INF)
                bias_ref[c, sl, :] = jnp.where(sel, 0.0, NEG_BIG)
            return carry

        lax.fori_loop(0, nck, bias_chunk, 0)

    @pl.when(any_tied)
    def _():
        need = TOPK - count32(lambda k, _: k > thr)

        def pos_pass(p, x):
            cand = x | lax.shift_left(jnp.int32(1), (seq.bit_length() - 2) - p)
            n = count32(lambda k, r0: (k == thr) & (row_iota + r0 < cand))
            return jnp.where(n < need, cand, x)

        xlim = lax.fori_loop(0, seq.bit_length() - 1, pos_pass, jnp.zeros((SUBLANES, QB), I32))

        def bias_chunk(c, carry):
            for r in range(KC // SUBLANES):
                sl = slice(r * SUBLANES, (r + 1) * SUBLANES)
                k = key_ref[c, sl, :]
                pos = row_iota + (c * KC + r * SUBLANES)
                sel = (k > thr) | ((k == thr) & (pos <= xlim))
                sel = sel & (k >= thr_fin) & (k < KEY_POS_INF)
                bias_ref[c, sl, :] = jnp.where(sel, 0.0, NEG_BIG)
            return carry

        lax.fori_loop(0, nck, bias_chunk, 0)

    m_ref[...] = jnp.full(m_ref.shape, NEG_BIG, F32)
    l_ref[...] = jnp.zeros(l_ref.shape, F32)
    acc_ref[...] = jnp.zeros(acc_ref.shape, F32)
    group = N_HEADS // N_KV_HEADS

    def logits(c, s_ref):
        col_max = []
        for pr in range(N_PAIRS):
            g = (2 * pr) // group
            q2 = jnp.concatenate([qn_ref[:, (2 * pr) * HEAD_DIM:(2 * pr + 1) * HEAD_DIM],
                                  qn_ref[:, (2 * pr + 1) * HEAD_DIM:(2 * pr + 2) * HEAD_DIM]],
                                 axis=0)
            mx = jnp.full((SUBLANES, 2 * QB), NEG_BIG, F32)
            for hf in range(KC // SUB):
                start = pl.multiple_of(c * KC + hf * SUB, SUB)
                kc = kn_ref[pl.ds(start, SUB), g * HEAD_DIM:(g + 1) * HEAD_DIM]
                bias = bias_ref[c, hf * SUB:(hf + 1) * SUB, :]
                s = _dot_nt(kc, q2) + jnp.concatenate([bias, bias], axis=1)
                s_ref[pr, hf * SUB:(hf + 1) * SUB, :] = s
                for r in range(SUB // SUBLANES):
                    mx = jnp.maximum(mx, s[r * SUBLANES:(r + 1) * SUBLANES, :])
            col_max.append(jnp.broadcast_to(jnp.max(mx, axis=0, keepdims=True), (SUBLANES, 2 * QB)))
        return tuple(col_max)

    def finish(c, s_ref, col_max):
        alphas = []
        for pr in range(N_PAIRS):
            m_prev = m_ref[pr]
            m_new = jnp.maximum(m_prev, col_max[pr])
            alpha = jnp.exp2(m_prev - m_new)
            lsum = jnp.zeros((SUBLANES, 2 * QB), F32)
            for t in range(KC // SLAB):
                rows = slice(t * SLAB, (t + 1) * SLAB)
                p = jnp.exp2(s_ref[pr, rows, :] - m_new[0:1, :])
                for r in range(SLAB // SUBLANES):
                    lsum = lsum + p[r * SUBLANES:(r + 1) * SUBLANES, :]
                p_ref[pr, rows, :] = p.astype(BF16)
            m_ref[pr] = m_new
            l_ref[pr] = alpha * l_ref[pr] + jnp.sum(lsum, axis=0, keepdims=True)
            alphas.append(alpha)
        for pr in range(N_PAIRS):
            g = (2 * pr) // group
            vt = vt_ref[c, g * HEAD_DIM:(g + 1) * HEAD_DIM, :]
            acc_ref[pr] = acc_ref[pr] * alphas[pr][0:1, :] + _dot(vt, p_ref[pr])

    def attn_step(i, col_max):
        c = 2 * i
        mid = logits(c + 1, sb_ref)
        finish(c, sa_ref, col_max)
        nxt = logits(c + 2, sa_ref)
        finish(c + 1, sb_ref, mid)
        return nxt

    n_steps = (nck - 1) // 2
    tail = 2 * n_steps
    col_max = lax.fori_loop(0, n_steps, attn_step, logits(0, sa_ref))

    @pl.when(tail == nck - 1)
    def _():
        finish(tail, sa_ref, col_max)

    @pl.when(tail != nck - 1)
    def _():
        mid = logits(tail + 1, sb_ref)
        finish(tail, sa_ref, col_max)
        finish(tail + 1, sb_ref, mid)

    for pr in range(N_PAIRS):
        o_t = acc_ref[pr] / l_ref[pr][0:1, :]
        for i in range(2):
            h = 2 * pr + i
            o_ref[:, h * HEAD_DIM:(h + 1) * HEAD_DIM] = o_t[:, i * QB:(i + 1) * QB].T.astype(BF16)


def _dsa(qn, iqr, iwt, kn, vt, ika, ikb, batch, seq):
    t = qn.shape[0]
    nb = seq // QB
    nchunk = seq // KC
    qmap = lambda b, j: (b * nb + j, 0)
    bmap = lambda b, j: (b, 0)
    return pl.pallas_call(
        _dsa_kernel,
        grid=(batch, nb),
        in_specs=[
            pl.BlockSpec((QB, ATTN_DIM), qmap),
            pl.BlockSpec((QB, ATTN_DIM), qmap),
            pl.BlockSpec((IDX_HEADS, QB), lambda b, j: (0, b * nb + j)),
            pl.BlockSpec((seq, KV_DIM), bmap),
            pl.BlockSpec((nchunk, KV_DIM, KC), lambda b, j: (b, 0, 0)),
            pl.BlockSpec((seq, LANES), bmap),
            pl.BlockSpec((seq, LANES), bmap),
        ],
        out_specs=pl.BlockSpec((QB, ATTN_DIM), qmap),
        out_shape=jax.ShapeDtypeStruct((t, ATTN_DIM), BF16),
        scratch_shapes=[
            pltpu.VMEM((nchunk, KC, QB), I32),
            pltpu.VMEM((SUBLANES, QB), I32),
            pltpu.VMEM((SUBLANES, QB), I32),
            pltpu.VMEM((nchunk, KC, QB), F32),
            pltpu.VMEM((N_PAIRS, SUBLANES, 2 * QB), F32),
            pltpu.VMEM((N_PAIRS, SUBLANES, 2 * QB), F32),
            pltpu.VMEM((N_PAIRS, HEAD_DIM, 2 * QB), F32),
            pltpu.VMEM((N_PAIRS, KC, 2 * QB), F32),
            pltpu.VMEM((N_PAIRS, KC, 2 * QB), F32),
            pltpu.VMEM((N_PAIRS, KC, 2 * QB), BF16),
        ],
        compiler_params=_cparams(("parallel", "arbitrary")),
        name="dsa",
    )(qn, iqr, iwt, kn, vt, ika, ikb)


def _mem_kv_kernel(mem_ref, g_ref, w_ref, kg_ref, mk_ref, mv_ref):
    m = _rms(mem_ref[...], g_ref[...]).astype(BF16)
    kv = _dot(m, w_ref[...])
    for h in range(MEM_HEADS):
        sl = slice(h * HEAD_DIM, (h + 1) * HEAD_DIM)
        mk_ref[:, sl] = _rms(kv[:, sl], kg_ref[...]).astype(BF16)
    mv_ref[...] = kv[:, MEM_DIM:].astype(BF16)


def _mem_kv(mem2d, g, w, kg, batch, l):
    n = mem2d.shape[0]
    m = n // batch
    out = jax.ShapeDtypeStruct((n, MEM_DIM), BF16)
    ospec = pl.BlockSpec((m, MEM_DIM), lambda b: (b, 0))
    return pl.pallas_call(
        _mem_kv_kernel,
        grid=(batch,),
        in_specs=[
            pl.BlockSpec((m, D_MODEL), lambda b: (b, 0)),
            _lp(l, (1, D_MODEL), lambda b: (0, 0)),
            _lp(l, (D_MODEL, 2 * MEM_DIM), lambda b: (0, 0)),
            _lp(l, (1, HEAD_DIM), lambda b: (0, 0)),
        ],
        out_specs=[ospec, ospec],
        out_shape=[out, out],
        compiler_params=_cparams(("parallel",)),
        name="mem_kv",
    )(mem2d, g, w, kg)


def _mem_attn_kernel(q_ref, mk_ref, mv_ref, qg_ref, o_ref):
    scale = HEAD_DIM ** -0.5
    for h in range(MEM_HEADS):
        sl = slice(h * HEAD_DIM, (h + 1) * HEAD_DIM)
        qh = (_rms(q_ref[:, sl].astype(F32), qg_ref[...]) * scale).astype(BF16)
        s = _dot_nt(qh, mk_ref[:, sl])
        p = jnp.exp(s - jnp.max(s, axis=1, keepdims=True))
        l = jnp.sum(p, axis=1, keepdims=True)
        o = _dot((p / l).astype(BF16), mv_ref[:, sl])
        o_ref[:, sl] = o.astype(BF16)


def _mem_attn(p, mk, mv, qg, batch, seq, l):
    t = p.shape[0]
    tq = 512
    nq = seq // tq
    m = mk.shape[0] // batch
    return pl.pallas_call(
        _mem_attn_kernel,
        grid=(batch, nq),
        in_specs=[
            pl.BlockSpec((tq, MEM_DIM), lambda b, i: (b * nq + i, P_MQ // MEM_DIM)),
            pl.BlockSpec((m, MEM_DIM), lambda b, i: (b, 0)),
            pl.BlockSpec((m, MEM_DIM), lambda b, i: (b, 0)),
            _lp(l, (1, HEAD_DIM), lambda b, i: (0, 0)),
        ],
        out_specs=pl.BlockSpec((tq, MEM_DIM), lambda b, i: (b * nq + i, 0)),
        out_shape=jax.ShapeDtypeStruct((t, MEM_DIM), BF16),
        compiler_params=_cparams(("parallel", "parallel")),
        name="mem_attn",
    )(p, mk, mv, qg)


CONV_HALO = 32
CONV_ROWS = 64


def _conv_kernel(cur_a_ref, cur_g_ref, prev_a_ref, prev_g_ref, inb_ref, cw_ref, cb_ref, lg_ref,
                 lb_ref, o_ref, u_ref, sh_ref):
    ts = cur_a_ref.shape[0]
    n = CONV_HALO + ts

    def glu(a, g):
        a = a.astype(F32) + inb_ref[:, :CONV_CH]
        g = g.astype(F32) + inb_ref[:, CONV_CH:]
        return a * jax.nn.sigmoid(g)

    u_prev = glu(prev_a_ref[ts - CONV_HALO:, :], prev_g_ref[ts - CONV_HALO:, :])
    u_ref[:CONV_HALO, :] = jnp.where(pl.program_id(1) == 0, 0.0, u_prev)
    u_ref[CONV_HALO:n, :] = glu(cur_a_ref[...], cur_g_ref[...])
    u_ref[n:, :] = jnp.zeros((SUBLANES, CONV_CH), F32)
    for r in range(1, SUBLANES):
        sh_ref[r - 1] = u_ref[r:r + n, :]

    first_tap = CONV_HALO - (CONV_WIDTH - 1)
    for t in range(ts // CONV_ROWS):
        base = t * CONV_ROWS
        y = jnp.broadcast_to(cb_ref[...], (CONV_ROWS, CONV_CH))
        for w in range(CONV_WIDTH):
            r = (first_tap + w) % SUBLANES
            a = base + first_tap + w - r
            rows = u_ref[a:a + CONV_ROWS, :] if r == 0 else sh_ref[r - 1, a:a + CONV_ROWS, :]
            y = y + rows * cw_ref[w:w + 1, :]
        mu = jnp.mean(y, axis=-1, keepdims=True)
        d = y - mu
        var = jnp.mean(d * d, axis=-1, keepdims=True)
        z = d * lax.rsqrt(var + EPS) * lg_ref[...] + lb_ref[...]
        o_ref[base:base + CONV_ROWS, :] = (z * jax.nn.sigmoid(z)).astype(BF16)


def _conv(p, inb, cw, cb, lg, lb, batch, seq, l):
    t = p.shape[0]
    ts = 512
    ns = seq // ts
    a_blk = P_GLU // CONV_CH
    vec = lambda w: _lp(l, (1, w), lambda b, i: (0, 0))
    cur = lambda k: pl.BlockSpec((ts, CONV_CH), lambda b, i: (b * ns + i, a_blk + k))
    prev = lambda k: pl.BlockSpec((ts, CONV_CH), lambda b, i: (b * ns + jnp.maximum(i - 1, 0), a_blk + k))
    return pl.pallas_call(
        _conv_kernel,
        grid=(batch, ns),
        in_specs=[
            cur(0), cur(1), prev(0), prev(1),
            vec(2 * CONV_CH),
            _lp(l, (CONV_WIDTH, CONV_CH), lambda b, i: (0, 0)),
            vec(CONV_CH), vec(CONV_CH), vec(CONV_CH),
        ],
        out_specs=pl.BlockSpec((ts, CONV_CH), lambda b, i: (b * ns + i, 0)),
        out_shape=jax.ShapeDtypeStruct((t, CONV_CH), BF16),
        scratch_shapes=[pltpu.VMEM((CONV_HALO + ts + SUBLANES, CONV_CH), F32),
                        pltpu.VMEM((SUBLANES - 1, CONV_HALO + ts, CONV_CH), F32)],
        compiler_params=_cparams(("parallel", "parallel")),
        name="conv",
    )(p, p, p, p, inb, cw, cb, lg, lb)


MERGE_TN = 512


def _merge_kernel(a_ref, m_ref, c_ref, g0_ref, g1_ref, g2_ref, gb0_ref, gb1_ref, gb2_ref,
                  wa_ref, wm_ref, wc_ref, cob_ref, o_ref):
    a, m, c = a_ref[...], m_ref[...], c_ref[...]
    for n in range(D_MODEL // MERGE_TN):
        sl = slice(n * MERGE_TN, (n + 1) * MERGE_TN)

        def gate(g_ref, gb_ref):
            return jax.nn.sigmoid(g_ref[:, sl].astype(F32) + gb_ref[:, sl])

        y = gate(g0_ref, gb0_ref) * _dot(a, wa_ref[:, sl])
        y = y + gate(g1_ref, gb1_ref) * _dot(m, wm_ref[:, sl])
        y = y + gate(g2_ref, gb2_ref) * (_dot(c, wc_ref[:, sl]) + cob_ref[:, sl])
        o_ref[:, sl] = y.astype(BF16)


def _merge(attn, memo, convo, p, gate_b, wa, wm, wc, cob, l):
    t = attn.shape[0]
    tm = 512
    gblk = P_GATES // D_MODEL
    row = lambda w: pl.BlockSpec((tm, w), lambda i: (i, 0))
    full = lambda r, c: _lp(l, (r, c), lambda i: (0, 0))
    gspec = lambda k: pl.BlockSpec((tm, D_MODEL), lambda i: (i, gblk + k))
    gbspec = lambda k: _lp(l, (1, D_MODEL), lambda i: (0, k))
    return pl.pallas_call(
        _merge_kernel,
        grid=(t // tm,),
        in_specs=[row(ATTN_DIM), row(MEM_DIM), row(CONV_CH),
                  gspec(0), gspec(1), gspec(2), gbspec(0), gbspec(1), gbspec(2),
                  full(ATTN_DIM, D_MODEL), full(MEM_DIM, D_MODEL), full(CONV_CH, D_MODEL),
                  full(1, D_MODEL)],
        out_specs=row(D_MODEL),
        out_shape=jax.ShapeDtypeStruct((t, D_MODEL), BF16),
        compiler_params=_cparams(("parallel",)),
        name="merge",
    )(attn, memo, convo, p, p, p, gate_b, gate_b, gate_b, wa, wm, wc, cob)


def _outproj_kernel(x_ref, mg_ref, w_ref, g_ref, x1_ref, h2_ref):
    x1 = x_ref[...] + _dot(mg_ref[...], w_ref[...])
    x1_ref[...] = x1
    h2_ref[...] = _rms(x1, g_ref[...]).astype(BF16)


def _outproj(x, merged, w, g, l):
    t = x.shape[0]
    tm = 512
    row = pl.BlockSpec((tm, D_MODEL), lambda i: (i, 0))
    return pl.pallas_call(
        _outproj_kernel,
        grid=(t // tm,),
        in_specs=[row, row,
                  _lp(l, (D_MODEL, D_MODEL), lambda i: (0, 0)),
                  _lp(l, (1, D_MODEL), lambda i: (0, 0))],
        out_specs=[row, row],
        out_shape=[jax.ShapeDtypeStruct((t, D_MODEL), F32), jax.ShapeDtypeStruct((t, D_MODEL), BF16)],
        compiler_params=_cparams(("parallel",)),
        name="outproj",
    )(x, merged, w, g)


def _mlp_kernel(h_ref, x_ref, wu_ref, wd_ref, o_ref):
    @pl.when(pl.program_id(1) == 0)
    def _():
        o_ref[...] = x_ref[...]

    u = jnp.maximum(_dot(h_ref[...], wu_ref[...]), 0.0)
    o_ref[...] += _dot((u * u).astype(BF16), wd_ref[...])


def _mlp(h2, x1, wu, wd, l):
    t = x1.shape[0]
    tm, tf = 512, 1024
    return pl.pallas_call(
        _mlp_kernel,
        grid=(t // tm, FFN_DIM // tf),
        in_specs=[
            pl.BlockSpec((tm, D_MODEL), lambda i, f: (i, 0)),
            pl.BlockSpec((tm, D_MODEL), lambda i, f: (i, 0)),
            _lp(l, (D_MODEL, tf), lambda i, f: (0, f)),
            _lp(l, (tf, D_MODEL), lambda i, f: (f, 0)),
        ],
        out_specs=pl.BlockSpec((tm, D_MODEL), lambda i, f: (i, 0)),
        out_shape=jax.ShapeDtypeStruct((t, D_MODEL), F32),
        compiler_params=_cparams(("parallel", "arbitrary")),
        name="mlp",
    )(h2, x1, wu, wd)


def _split_w_in(w):
    narrow = IDX_DIM + IDX_HEADS
    wb = w.astype(BF16)
    head = wb[:, :, :P_HEAD]
    tail = wb[:, :, P_HEAD + narrow:]
    small = jnp.pad(wb[:, :, P_HEAD:P_HEAD + narrow], ((0, 0), (0, 0), (0, PS_WIDTH - narrow)))
    return head, tail, small


def kernel(x, mem, positions, norm1_g, w_in, q_norm_g, k_norm_g, mem_norm_g, w_mem_kv, mq_norm_g,
           mk_norm_g, conv_in_b, conv_w, conv_b, conv_ln_g, conv_ln_b, gate_b, w_attn_o, w_mem_o,
           w_conv_o, conv_o_b, w_out, norm2_g, w_up, w_down):
    batch, seq, _ = x.shape
    depth = w_in.shape[0]
    t = batch * seq
    xf = x.reshape(t, D_MODEL)
    mem2d = mem.reshape(batch * mem.shape[1], D_MODEL)
    tabs = _rope_tables(positions)

    vec = lambda v: v.reshape(depth, 1, -1)
    norm1_g, q_norm_g, k_norm_g, mem_norm_g, mq_norm_g, mk_norm_g = map(
        vec, (norm1_g, q_norm_g, k_norm_g, mem_norm_g, mq_norm_g, mk_norm_g))
    conv_in_b, conv_b, conv_ln_g, conv_ln_b, gate_b, conv_o_b, norm2_g = map(
        vec, (conv_in_b, conv_b, conv_ln_g, conv_ln_b, gate_b, conv_o_b, norm2_g))
    w_head, w_tail, w_small = _split_w_in(w_in)
    w_mem_kv, w_attn_o, w_mem_o, w_conv_o, w_out, w_up, w_down = (
        w.astype(BF16) for w in (w_mem_kv, w_attn_o, w_mem_o, w_conv_o, w_out, w_up, w_down))

    for l in range(depth):
        p, ps = _proj(xf, norm1_g, w_head, w_tail, w_small, l)
        qn, iqr, kn, vt, ika, ikb, iwt = _prep(p, ps, tabs, q_norm_g, k_norm_g, l)
        attn = _dsa(qn, iqr, iwt, kn, vt, ika, ikb, batch, seq)
        mk, mv = _mem_kv(mem2d, mem_norm_g, w_mem_kv, mk_norm_g, batch, l)
        memo = _mem_attn(p, mk, mv, mq_norm_g, batch, seq, l)
        convo = _conv(p, conv_in_b, conv_w, conv_b, conv_ln_g, conv_ln_b, batch, seq, l)
        merged = _merge(attn, memo, convo, p, gate_b, w_attn_o, w_mem_o, w_conv_o, conv_o_b, l)
        x1, h2 = _outproj(xf, merged, w_out, norm2_g, l)
        xf = _mlp(h2, x1, w_up, w_down, l)
    return xf.reshape(batch, seq, D_MODEL)
```

```python
import functools
import math

import jax
import jax.numpy as jnp
from jax import lax
from jax.experimental import pallas as pl
from jax.experimental.pallas import tpu as pltpu

F32 = jnp.float32
BF16 = jnp.bfloat16
I32 = jnp.int32

D_MODEL = 2048
HEAD_DIM = 128
N_HEADS = 8
N_KV_HEADS = 2
IDX_HEADS = 16
IDX_DIM = 64
TOPK = 256
CONV_CH = 512
CONV_WIDTH = 31
MEM_HEADS = 4
FFN_DIM = 4 * D_MODEL
ROPE_THETA = 500000.0
N_BRANCH = 3
EPS = 1e-6
ATTN_DIM = N_HEADS * HEAD_DIM
KV_DIM = N_KV_HEADS * HEAD_DIM
MEM_DIM = MEM_HEADS * HEAD_DIM
HEAD_ROT_HALF = HEAD_DIM // 8
IDX_ROT_HALF = IDX_DIM // 8

LANES = 128
VMEM_LIMIT = 56 * 1024 * 1024

P_Q = 0
P_K = P_Q + ATTN_DIM
P_V = P_K + KV_DIM
P_IQ = P_V + KV_DIM
P_HEAD = P_IQ + IDX_HEADS * IDX_DIM
P_GLU = P_HEAD
P_MQ = P_GLU + 2 * CONV_CH
P_GATES = P_MQ + MEM_DIM
P_WIDTH = P_GATES + N_BRANCH * D_MODEL
P_TAIL = P_WIDTH - P_HEAD
PS_WIDTH = LANES

QB = 128
KC = 512
SUB = 256
SLAB = 32
SUBLANES = 8
LOG2E = 1.4426950408889634
NEG_BIG = -1e30
INT_MIN = -2 ** 31
KEY_NEG_INF = -2139095041
KEY_POS_INF = 2139095040


def _cparams(sem, vmem=VMEM_LIMIT):
    return pltpu.CompilerParams(dimension_semantics=sem, vmem_limit_bytes=vmem)


def _lp(l, shape, imap):
    return pl.BlockSpec((pl.Squeezed(),) + shape, lambda *g: (l,) + imap(*g))


def _dot(a, b):
    return jnp.dot(a, b, preferred_element_type=F32)


def _dot_nt(a, b):
    return lax.dot_general(a, b, (((1,), (1,)), ((), ())), preferred_element_type=F32)


def _rms(xf, g):
    return xf * lax.rsqrt(jnp.mean(xf * xf, axis=-1, keepdims=True) + EPS) * g


def _rope_table_kernel(pos_ref, ch_ref, sh_ref, ci_ref, si_ref):
    pos = pos_ref[...].astype(F32)
    lane = lax.broadcasted_iota(I32, (1, LANES), 1)
    hh, ih = HEAD_ROT_HALF, IDX_ROT_HALF
    is_head = lane < hh
    fi = jnp.where(is_head, lane, lane - hh).astype(F32)
    inv = jnp.exp(fi * jnp.where(is_head, -math.log(ROPE_THETA) / hh, -math.log(ROPE_THETA) / ih))
    ang = pos * inv
    c = jnp.cos(ang)
    s = jnp.sin(ang)

    def place(x, shift):
        return pltpu.roll(x, shift % LANES, 1)

    second = (lane >= hh) & (lane < 2 * hh)
    ch_ref[...] = jnp.where(is_head, c, jnp.where(second, place(c, hh), 1.0))
    sh_ref[...] = jnp.where(is_head, -s, jnp.where(second, place(s, hh), 0.0))
    r = lane & (IDX_DIM - 1)
    upper = lane >= IDX_DIM

    def idx_table(x, sign_first, fill):
        first = jnp.where(upper, place(x, IDX_DIM - hh), place(x, -hh))
        second_half = jnp.where(upper, place(x, IDX_DIM - hh + ih), place(x, ih - hh))
        return jnp.where(r < ih, sign_first * first, jnp.where(r < 2 * ih, second_half, fill))

    ci_ref[...] = idx_table(c, 1.0, 1.0)
    si_ref[...] = idx_table(s, -1.0, 0.0)


def _swap_matrix(period, half):
    r = lax.broadcasted_iota(I32, (LANES, LANES), 0)
    c = lax.broadcasted_iota(I32, (LANES, LANES), 1)
    pos = c & (period - 1)
    src = jnp.where(pos < half, c + half, c - half)
    return jnp.where((pos < 2 * half) & (r == src), 1.0, 0.0).astype(BF16)


def _rope_tables(positions):
    t = positions.size
    tp = 512
    tab = jax.ShapeDtypeStruct((t, LANES), F32)
    spec = pl.BlockSpec((tp, LANES), lambda i: (i, 0))
    return pl.pallas_call(
        _rope_table_kernel,
        grid=(t // tp,),
        in_specs=[pl.BlockSpec((tp, 1), lambda i: (i, 0))],
        out_specs=[spec] * 4,
        out_shape=[tab] * 4,
        compiler_params=_cparams(("parallel",)),
        name="rope_tables",
    )(positions.reshape(t, 1))


def _rope(t, c, s, swap):
    hi = t.astype(BF16)
    lo = (t - hi.astype(F32)).astype(BF16)
    return t * c + (_dot(hi, swap) + _dot(lo, swap)) * s


PROJ_TN = 1280


def _proj_kernel(x_ref, g_ref, wh_ref, wt_ref, ws_ref, o_ref, os_ref, h_ref):
    j = pl.program_id(1)

    @pl.when(j == 0)
    def _():
        h = _rms(x_ref[...], g_ref[...]).astype(BF16)
        h_ref[...] = h
        os_ref[...] = _dot(h, ws_ref[...])

    @pl.when(j < P_HEAD // PROJ_TN)
    def _():
        o_ref[...] = _dot(h_ref[...], wh_ref[...]).astype(o_ref.dtype)

    @pl.when(j >= P_HEAD // PROJ_TN)
    def _():
        o_ref[...] = _dot(h_ref[...], wt_ref[...]).astype(o_ref.dtype)


def _proj(x, g, wh, wt, ws, l):
    t = x.shape[0]
    tm, tn = 1024, PROJ_TN
    n_head = P_HEAD // tn
    return pl.pallas_call(
        _proj_kernel,
        grid=(t // tm, P_WIDTH // tn),
        in_specs=[
            pl.BlockSpec((tm, D_MODEL), lambda i, j: (i, 0)),
            _lp(l, (1, D_MODEL), lambda i, j: (0, 0)),
            _lp(l, (D_MODEL, tn), lambda i, j: (0, jnp.minimum(j, n_head - 1))),
            _lp(l, (D_MODEL, tn), lambda i, j: (0, jnp.maximum(j - n_head, 0))),
            _lp(l, (D_MODEL, PS_WIDTH), lambda i, j: (0, 0)),
        ],
        out_specs=[
            pl.BlockSpec((tm, tn), lambda i, j: (i, j)),
            pl.BlockSpec((tm, PS_WIDTH), lambda i, j: (i, 0)),
        ],
        out_shape=[
            jax.ShapeDtypeStruct((t, P_WIDTH), BF16),
            jax.ShapeDtypeStruct((t, PS_WIDTH), F32),
        ],
        scratch_shapes=[pltpu.VMEM((tm, D_MODEL), BF16)],
        compiler_params=_cparams(("parallel", "arbitrary")),
        name="proj",
    )(x, g, wh, wt, ws)


def _prep_kernel(q_ref, iql_ref, iqh_ref, kv_ref, ps_ref, ch_ref, sh_ref, ci_ref, si_ref,
                 qg_ref, kg_ref, qn_ref, iqr_ref, kn_ref, vt_ref, ika_ref, ikb_ref, iwt_ref):
    ch, sh = ch_ref[...], sh_ref[...]
    ci, si = ci_ref[...], si_ref[...]
    swap_h = _swap_matrix(HEAD_DIM, HEAD_ROT_HALF)
    swap_i = _swap_matrix(IDX_DIM, IDX_ROT_HALF)
    scale = (HEAD_DIM ** -0.5) * LOG2E
    half_groups = N_HEADS // 2
    for h in range(N_HEADS):
        sl = slice(h * HEAD_DIM, (h + 1) * HEAD_DIM)
        qh = _rms(q_ref[:, sl].astype(F32), qg_ref[...])
        qn_ref[:, sl] = (_rope(qh, ch, sh, swap_h) * scale).astype(BF16)
        src = iql_ref if h < half_groups else iqh_ref
        hs = h % half_groups
        iqh = src[:, hs * LANES:(hs + 1) * LANES].astype(F32)
        iqr_ref[:, sl] = _rope(iqh, ci, si, swap_i).astype(BF16)
    for g in range(N_KV_HEADS):
        sl = slice(g * HEAD_DIM, (g + 1) * HEAD_DIM)
        kh = _rms(kv_ref[:, sl].astype(F32), kg_ref[...])
        kn_ref[:, sl] = _rope(kh, ch, sh, swap_h).astype(BF16)
        vh = kv_ref[:, KV_DIM + g * HEAD_DIM:KV_DIM + (g + 1) * HEAD_DIM].astype(F32)
        vt_ref[0, sl, :] = vh.T.astype(BF16)
    ps = ps_ref[...]
    lane = lax.broadcasted_iota(I32, (1, LANES), 1)
    ikr = jnp.where(lane < IDX_DIM, _rope(ps, ci, si, swap_i), 0.0)
    ika_ref[...] = ikr.astype(BF16)
    ikb_ref[...] = pltpu.roll(ikr, IDX_DIM, 1).astype(BF16)
    idx_scale = (IDX_DIM ** -0.5) * (IDX_HEADS ** -0.5)
    iws = jnp.where(lane < IDX_HEADS, pltpu.roll(ps, LANES - IDX_DIM, 1) * idx_scale, 0.0)
    iwt_ref[...] = iws.T


def _prep(p, ps, tabs, qg, kg, l):
    t = p.shape[0]
    tp = KC
    half_iq = IDX_HEADS * IDX_DIM // 2
    tab_spec = pl.BlockSpec((tp, LANES), lambda i: (i, 0))
    vec_spec = _lp(l, (1, HEAD_DIM), lambda i: (0, 0))

    def out(width, dtype):
        return (pl.BlockSpec((tp, width), lambda i: (i, 0)), jax.ShapeDtypeStruct((t, width), dtype))

    outs = [out(ATTN_DIM, BF16), out(ATTN_DIM, BF16), out(KV_DIM, BF16),
            (pl.BlockSpec((1, KV_DIM, tp), lambda i: (i, 0, 0)),
             jax.ShapeDtypeStruct((t // tp, KV_DIM, tp), BF16)),
            out(LANES, BF16), out(LANES, BF16),
            (pl.BlockSpec((LANES, tp), lambda i: (0, i)), jax.ShapeDtypeStruct((LANES, t), F32))]
    return pl.pallas_call(
        _prep_kernel,
        grid=(t // tp,),
        in_specs=[
            pl.BlockSpec((tp, ATTN_DIM), lambda i: (i, P_Q // ATTN_DIM)),
            pl.BlockSpec((tp, half_iq), lambda i: (i, P_IQ // half_iq)),
            pl.BlockSpec((tp, half_iq), lambda i: (i, P_IQ // half_iq + 1)),
            pl.BlockSpec((tp, 2 * KV_DIM), lambda i: (i, P_K // (2 * KV_DIM))),
            tab_spec,
        ] + [tab_spec] * 4 + [vec_spec, vec_spec],
        out_specs=[o[0] for o in outs],
        out_shape=[o[1] for o in outs],
        compiler_params=_cparams(("parallel",)),
        name="prep",
    )(p, p, p, p, ps, *tabs, qg, kg)


def _sortable(x):
    bits = lax.bitcast_convert_type(x, I32)
    return bits ^ ((bits >> 31) & 0x7FFFFFFF)


N_PAIRS = N_HEADS // 2
N_PART = 4


def _dsa_kernel(qn_ref, iq_ref, iwt_ref, kn_ref, vt_ref, ika_ref, ikb_ref, o_ref,
                key_ref, thr_ref, nge_ref, bias_ref, m_ref, l_ref, acc_ref, sa_ref, sb_ref, p_ref):
    j = pl.program_id(1)
    nck = j // (KC // QB) + 1
    seq = kn_ref.shape[0]

    iwt = iwt_ref[...]
    q_pos = j * QB + lax.broadcasted_iota(I32, (SUB, QB), 1)
    n_grp = IDX_HEADS // 2

    halves = KC // SUB
    nhalf = j // (SUB // QB) + 1

    def score_half(c, hf):
        start = pl.multiple_of(c * KC + hf * SUB, SUB)
        ka = ika_ref[pl.ds(start, SUB), :]
        kb = ikb_ref[pl.ds(start, SUB), :]
        acc = jnp.zeros((SUB, QB), F32)
        for a in range(n_grp // 2):
            b = a + n_grp // 2
            iq2 = jnp.concatenate([iq_ref[:, a * LANES:(a + 1) * LANES],
                                   iq_ref[:, b * LANES:(b + 1) * LANES]], axis=0)
            sa = _dot_nt(ka, iq2)
            sb = _dot_nt(kb, iq2)
            acc = acc + jnp.maximum(sa[:, :QB], 0.0) * iwt[2 * a:2 * a + 1, :]
            acc = acc + jnp.maximum(sb[:, :QB], 0.0) * iwt[2 * a + 1:2 * a + 2, :]
            acc = acc + jnp.maximum(sa[:, QB:], 0.0) * iwt[2 * b:2 * b + 1, :]
            acc = acc + jnp.maximum(sb[:, QB:], 0.0) * iwt[2 * b + 1:2 * b + 2, :]
        k_pos = start + lax.broadcasted_iota(I32, (SUB, QB), 0)
        acc = jnp.where(k_pos <= q_pos, acc, -jnp.inf)
        key_ref[c, hf * SUB:(hf + 1) * SUB, :] = _sortable(acc)

    def score_chunk(c, carry):
        for hf in range(halves):
            score_half(c, hf)
        return carry

    lax.fori_loop(0, nhalf // halves, score_chunk, 0)

    @pl.when(nhalf % halves == 1)
    def _():
        score_half(nck - 1, 0)
        key_ref[nck - 1, SUB:, :] = jnp.full((KC - SUB, QB), INT_MIN, I32)

    n_loaded = jnp.broadcast_to(nhalf * SUB, (SUBLANES, QB)).astype(I32)

    def search(nh):
        def search_pass(p, carry):
            tu, n_at = carry
            cand_u = tu | lax.shift_left(jnp.int32(1), 31 - p)
            cand = cand_u ^ INT_MIN
            parts = [jnp.zeros((SUBLANES, QB), I32)] * N_PART
            for r in range(nh * SUB // SUBLANES):
                row = (r * SUBLANES) % KC
                k = key_ref[(r * SUBLANES) // KC, row:row + SUBLANES, :]
                parts[r % N_PART] = parts[r % N_PART] + jnp.where(k >= cand, 1, 0)
            tot = (parts[0] + parts[1]) + (parts[2] + parts[3])
            n = jnp.broadcast_to(jnp.sum(tot, axis=0, keepdims=True), (SUBLANES, QB))
            ok = n >= TOPK
            return jnp.where(ok, cand_u, tu), jnp.where(ok, n, n_at)

        tu, n_ge = lax.fori_loop(0, 32, search_pass, (jnp.zeros((SUBLANES, QB), I32), n_loaded))
        thr_ref[...] = tu ^ INT_MIN
        nge_ref[...] = n_ge

    for nh in range(1, key_ref.shape[0] * halves + 1):
        pl.when(nhalf == nh)(functools.partial(search, nh))
    thr = thr_ref[...]
    n_ge = nge_ref[...]

    row_iota = lax.broadcasted_iota(I32, (SUBLANES, QB), 0)
    tied = (n_ge > TOPK) & (thr > KEY_NEG_INF)
    any_tied = jnp.max(jnp.where(tied, 1, 0)) > 0

    def count32(pred):
        def chunk(c, parts):
            parts = list(parts)
            for r in range(KC // SUBLANES):
                k = key_ref[c, r * SUBLANES:(r + 1) * SUBLANES, :]
                hit = pred(k, c * KC + r * SUBLANES)
                parts[r % N_PART] = parts[r % N_PART] + jnp.where(hit, 1, 0)
            return tuple(parts)

        zero = jnp.zeros((SUBLANES, QB), I32)
        parts = lax.fori_loop(0, nck, chunk, (zero,) * N_PART)
        tot = (parts[0] + parts[1]) + (parts[2] + parts[3])
        return jnp.broadcast_to(jnp.sum(tot, axis=0, keepdims=True), (SUBLANES, QB))

    thr_fin = jnp.maximum(thr, KEY_NEG_INF + 1)

    @pl.when(jnp.logical_not(any_tied))
    def _():
        def bias_chunk(c, carry):
            for r in range(KC // SUBLANES):
                sl = slice(r * SUBLANES, (r + 1) * SUBLANES)
                k = key_ref[c, sl, :]
                sel = (k >= thr_fin) & (k < KEY_POS_INF)
                bias_ref[c, sl, :] = jnp.where(sel, 0.0, NEG_BIG)
            return carry

        lax.fori_loop(0, nck, bias_chunk, 0)

    @pl.when(any_tied)
    def _():
        need = TOPK - count32(lambda k, _: k > thr)

        def pos_pass(p, x):
            cand = x | lax.shift_left(jnp.int32(1), (seq.bit_length() - 2) - p)
            n = count32(lambda k, r0: (k == thr) & (row_iota + r0 < cand))
            return jnp.where(n < need, cand, x)

        xlim = lax.fori_loop(0, seq.bit_length() - 1, pos_pass, jnp.zeros((SUBLANES, QB), I32))

        def bias_chunk(c, carry):
            for r in range(KC // SUBLANES):
                sl = slice(r * SUBLANES, (r + 1) * SUBLANES)
                k = key_ref[c, sl, :]
                pos = row_iota + (c * KC + r * SUBLANES)
                sel = (k > thr) | ((k == thr) & (pos <= xlim))
                sel = sel & (k >= thr_fin) & (k < KEY_POS_INF)
                bias_ref[c, sl, :] = jnp.where(sel, 0.0, NEG_BIG)
            return carry

        lax.fori_loop(0, nck, bias_chunk, 0)

    m_ref[...] = jnp.full(m_ref.shape, NEG_BIG, F32)
    l_ref[...] = jnp.zeros(l_ref.shape, F32)
    acc_ref[...] = jnp.zeros(acc_ref.shape, F32)
    group = N_HEADS // N_KV_HEADS

    def logits(c, s_ref):
        col_max = []
        for pr in range(N_PAIRS):
            g = (2 * pr) // group
            q2 = jnp.concatenate([qn_ref[:, (2 * pr) * HEAD_DIM:(2 * pr + 1) * HEAD_DIM],
                                  qn_ref[:, (2 * pr + 1) * HEAD_DIM:(2 * pr + 2) * HEAD_DIM]],
                                 axis=0)
            mx = jnp.full((SUBLANES, 2 * QB), NEG_BIG, F32)
            for hf in range(KC // SUB):
                start = pl.multiple_of(c * KC + hf * SUB, SUB)
                kc = kn_ref[pl.ds(start, SUB), g * HEAD_DIM:(g + 1) * HEAD_DIM]
                bias = bias_ref[c, hf * SUB:(hf + 1) * SUB, :]
                s = _dot_nt(kc, q2) + jnp.concatenate([bias, bias], axis=1)
                s_ref[pr, hf * SUB:(hf + 1) * SUB, :] = s
                for r in range(SUB // SUBLANES):
                    mx = jnp.maximum(mx, s[r * SUBLANES:(r + 1) * SUBLANES, :])
            col_max.append(jnp.broadcast_to(jnp.max(mx, axis=0, keepdims=True), (SUBLANES, 2 * QB)))
        return tuple(col_max)

    def finish(c, s_ref, col_max):
        alphas = []
        for pr in range(N_PAIRS):
            m_prev = m_ref[pr]
            m_new = jnp.maximum(m_prev, col_max[pr])
            alpha = jnp.exp2(m_prev - m_new)
            lsum = jnp.zeros((SUBLANES, 2 * QB), F32)
            for t in range(KC // SLAB):
                rows = slice(t * SLAB, (t + 1) * SLAB)
                p = jnp.exp2(s_ref[pr, rows, :] - m_new[0:1, :])
                for r in range(SLAB // SUBLANES):
                    lsum = lsum + p[r * SUBLANES:(r + 1) * SUBLANES, :]
                p_ref[pr, rows, :] = p.astype(BF16)
            m_ref[pr] = m_new
            l_ref[pr] = alpha * l_ref[pr] + jnp.sum(lsum, axis=0, keepdims=True)
            alphas.append(alpha)
        for pr in range(N_PAIRS):
            g = (2 * pr) // group
            vt = vt_ref[c, g * HEAD_DIM:(g + 1) * HEAD_DIM, :]
            acc_ref[pr] = acc_ref[pr] * alphas[pr][0:1, :] + _dot(vt, p_ref[pr])

    def attn_step(i, col_max):
        c = 2 * i
        mid = logits(c + 1, sb_ref)
        finish(c, sa_ref, col_max)
        nxt = logits(c + 2, sa_ref)
        finish(c + 1, sb_ref, mid)
        return nxt

    n_steps = (nck - 1) // 2
    tail = 2 * n_steps
    col_max = lax.fori_loop(0, n_steps, attn_step, logits(0, sa_ref))

    @pl.when(tail == nck - 1)
    def _():
        finish(tail, sa_ref, col_max)

    @pl.when(tail != nck - 1)
    def _():
        mid = logits(tail + 1, sb_ref)
        finish(tail, sa_ref, col_max)
        finish(tail + 1, sb_ref, mid)

    for pr in range(N_PAIRS):
        o_t = acc_ref[pr] / l_ref[pr][0:1, :]
        for i in range(2):
            h = 2 * pr + i
            o_ref[:, h * HEAD_DIM:(h + 1) * HEAD_DIM] = o_t[:, i * QB:(i + 1) * QB].T.astype(BF16)


def _dsa(qn, iqr, iwt, kn, vt, ika, ikb, batch, seq):
    t = qn.shape[0]
    nb = seq // QB
    nchunk = seq // KC
    qmap = lambda b, j: (b * nb + j, 0)
    bmap = lambda b, j: (b, 0)
    return pl.pallas_call(
        _dsa_kernel,
        grid=(batch, nb),
        in_specs=[
            pl.BlockSpec((QB, ATTN_DIM), qmap),
            pl.BlockSpec((QB, ATTN_DIM), qmap),
            pl.BlockSpec((IDX_HEADS, QB), lambda b, j: (0, b * nb + j)),
            pl.BlockSpec((seq, KV_DIM), bmap),
            pl.BlockSpec((nchunk, KV_DIM, KC), lambda b, j: (b, 0, 0)),
            pl.BlockSpec((seq, LANES), bmap),
            pl.BlockSpec((seq, LANES), bmap),
        ],
        out_specs=pl.BlockSpec((QB, ATTN_DIM), qmap),
        out_shape=jax.ShapeDtypeStruct((t, ATTN_DIM), BF16),
        scratch_shapes=[
            pltpu.VMEM((nchunk, KC, QB), I32),
            pltpu.VMEM((SUBLANES, QB), I32),
            pltpu.VMEM((SUBLANES, QB), I32),
            pltpu.VMEM((nchunk, KC, QB), F32),
            pltpu.VMEM((N_PAIRS, SUBLANES, 2 * QB), F32),
            pltpu.VMEM((N_PAIRS, SUBLANES, 2 * QB), F32),
            pltpu.VMEM((N_PAIRS, HEAD_DIM, 2 * QB), F32),
            pltpu.VMEM((N_PAIRS, KC, 2 * QB), F32),
            pltpu.VMEM((N_PAIRS, KC, 2 * QB), F32),
            pltpu.VMEM((N_PAIRS, KC, 2 * QB), BF16),
        ],
        compiler_params=_cparams(("parallel", "arbitrary")),
        name="dsa",
    )(qn, iqr, iwt, kn, vt, ika, ikb)


def _mem_kv_kernel(mem_ref, g_ref, w_ref, kg_ref, mk_ref, mv_ref):
    m = _rms(mem_ref[...], g_ref[...]).astype(BF16)
    kv = _dot(m, w_ref[...])
    for h in range(MEM_HEADS):
        sl = slice(h * HEAD_DIM, (h + 1) * HEAD_DIM)
        mk_ref[:, sl] = _rms(kv[:, sl], kg_ref[...]).astype(BF16)
    mv_ref[...] = kv[:, MEM_DIM:].astype(BF16)


def _mem_kv(mem2d, g, w, kg, batch, l):
    n = mem2d.shape[0]
    m = n // batch
    out = jax.ShapeDtypeStruct((n, MEM_DIM), BF16)
    ospec = pl.BlockSpec((m, MEM_DIM), lambda b: (b, 0))
    return pl.pallas_call(
        _mem_kv_kernel,
        grid=(batch,),
        in_specs=[
            pl.BlockSpec((m, D_MODEL), lambda b: (b, 0)),
            _lp(l, (1, D_MODEL), lambda b: (0, 0)),
            _lp(l, (D_MODEL, 2 * MEM_DIM), lambda b: (0, 0)),
            _lp(l, (1, HEAD_DIM), lambda b: (0, 0)),
        ],
        out_specs=[ospec, ospec],
        out_shape=[out, out],
        compiler_params=_cparams(("parallel",)),
        name="mem_kv",
    )(mem2d, g, w, kg)


def _mem_attn_kernel(q_ref, mk_ref, mv_ref, qg_ref, o_ref):
    scale = HEAD_DIM ** -0.5
    for h in range(MEM_HEADS):
        sl = slice(h * HEAD_DIM, (h + 1) * HEAD_DIM)
        qh = (_rms(q_ref[:, sl].astype(F32), qg_ref[...]) * scale).astype(BF16)
        s = _dot_nt(qh, mk_ref[:, sl])
        p = jnp.exp(s - jnp.max(s, axis=1, keepdims=True))
        l = jnp.sum(p, axis=1, keepdims=True)
        o = _dot((p / l).astype(BF16), mv_ref[:, sl])
        o_ref[:, sl] = o.astype(BF16)


def _mem_attn(p, mk, mv, qg, batch, seq, l):
    t = p.shape[0]
    tq = 512
    nq = seq // tq
    m = mk.shape[0] // batch
    return pl.pallas_call(
        _mem_attn_kernel,
        grid=(batch, nq),
        in_specs=[
            pl.BlockSpec((tq, MEM_DIM), lambda b, i: (b * nq + i, P_MQ // MEM_DIM)),
            pl.BlockSpec((m, MEM_DIM), lambda b, i: (b, 0)),
            pl.BlockSpec((m, MEM_DIM), lambda b, i: (b, 0)),
            _lp(l, (1, HEAD_DIM), lambda b, i: (0, 0)),
        ],
        out_specs=pl.BlockSpec((tq, MEM_DIM), lambda b, i: (b * nq + i, 0)),
        out_shape=jax.ShapeDtypeStruct((t, MEM_DIM), BF16),
        compiler_params=_cparams(("parallel", "parallel")),
        name="mem_attn",
    )(p, mk, mv, qg)


CONV_HALO = 32
CONV_ROWS = 64


def _conv_kernel(cur_a_ref, cur_g_ref, prev_a_ref, prev_g_ref, inb_ref, cw_ref, cb_ref, lg_ref,
                 lb_ref, o_ref, u_ref, sh_ref):
    ts = cur_a_ref.shape[0]
    n = CONV_HALO + ts

    def glu(a, g):
        a = a.astype(F32) + inb_ref[:, :CONV_CH]
        g = g.astype(F32) + inb_ref[:, CONV_CH:]
        return a * jax.nn.sigmoid(g)

    u_prev = glu(prev_a_ref[ts - CONV_HALO:, :], prev_g_ref[ts - CONV_HALO:, :])
    u_ref[:CONV_HALO, :] = jnp.where(pl.program_id(1) == 0, 0.0, u_prev)
    u_ref[CONV_HALO:n, :] = glu(cur_a_ref[...], cur_g_ref[...])
    u_ref[n:, :] = jnp.zeros((SUBLANES, CONV_CH), F32)
    for r in range(1, SUBLANES):
        sh_ref[r - 1] = u_ref[r:r + n, :]

    first_tap = CONV_HALO - (CONV_WIDTH - 1)
    for t in range(ts // CONV_ROWS):
        base = t * CONV_ROWS
        y = jnp.broadcast_to(cb_ref[...], (CONV_ROWS, CONV_CH))
        for w in range(CONV_WIDTH):
            r = (first_tap + w) % SUBLANES
            a = base + first_tap + w - r
            rows = u_ref[a:a + CONV_ROWS, :] if r == 0 else sh_ref[r - 1, a:a + CONV_ROWS, :]
            y = y + rows * cw_ref[w:w + 1, :]
        mu = jnp.mean(y, axis=-1, keepdims=True)
        d = y - mu
        var = jnp.mean(d * d, axis=-1, keepdims=True)
        z = d * lax.rsqrt(var + EPS) * lg_ref[...] + lb_ref[...]
        o_ref[base:base + CONV_ROWS, :] = (z * jax.nn.sigmoid(z)).astype(BF16)


def _conv(p, inb, cw, cb, lg, lb, batch, seq, l):
    t = p.shape[0]
    ts = 512
    ns = seq // ts
    a_blk = P_GLU // CONV_CH
    vec = lambda w: _lp(l, (1, w), lambda b, i: (0, 0))
    cur = lambda k: pl.BlockSpec((ts, CONV_CH), lambda b, i: (b * ns + i, a_blk + k))
    prev = lambda k: pl.BlockSpec((ts, CONV_CH), lambda b, i: (b * ns + jnp.maximum(i - 1, 0), a_blk + k))
    return pl.pallas_call(
        _conv_kernel,
        grid=(batch, ns),
        in_specs=[
            cur(0), cur(1), prev(0), prev(1),
            vec(2 * CONV_CH),
            _lp(l, (CONV_WIDTH, CONV_CH), lambda b, i: (0, 0)),
            vec(CONV_CH), vec(CONV_CH), vec(CONV_CH),
        ],
        out_specs=pl.BlockSpec((ts, CONV_CH), lambda b, i: (b * ns + i, 0)),
        out_shape=jax.ShapeDtypeStruct((t, CONV_CH), BF16),
        scratch_shapes=[pltpu.VMEM((CONV_HALO + ts + SUBLANES, CONV_CH), F32),
                        pltpu.VMEM((SUBLANES - 1, CONV_HALO + ts, CONV_CH), F32)],
        compiler_params=_cparams(("parallel", "parallel")),
        name="conv",
    )(p, p, p, p, inb, cw, cb, lg, lb)


MERGE_TN = 512


def _merge_kernel(a_ref, m_ref, c_ref, g0_ref, g1_ref, g2_ref, gb0_ref, gb1_ref, gb2_ref,
                  wa_ref, wm_ref, wc_ref, cob_ref, o_ref):
    a, m, c = a_ref[...], m_ref[...], c_ref[...]
    for n in range(D_MODEL // MERGE_TN):
        sl = slice(n * MERGE_TN, (n + 1) * MERGE_TN)

        def gate(g_ref, gb_ref):
            return jax.nn.sigmoid(g_ref[:, sl].astype(F32) + gb_ref[:, sl])

        y = gate(g0_ref, gb0_ref) * _dot(a, wa_ref[:, sl])
        y = y + gate(g1_ref, gb1_ref) * _dot(m, wm_ref[:, sl])
        y = y + gate(g2_ref, gb2_ref) * (_dot(c, wc_ref[:, sl]) + cob_ref[:, sl])
        o_ref[:, sl] = y.astype(BF16)


def _merge(attn, memo, convo, p, gate_b, wa, wm, wc, cob, l):
    t = attn.shape[0]
    tm = 512
    gblk = P_GATES // D_MODEL
    row = lambda w: pl.BlockSpec((tm, w), lambda i: (i, 0))
    full = lambda r, c: _lp(l, (r, c), lambda i: (0, 0))
    gspec = lambda k: pl.BlockSpec((tm, D_MODEL), lambda i: (i, gblk + k))
    gbspec = lambda k: _lp(l, (1, D_MODEL), lambda i: (0, k))
    return pl.pallas_call(
        _merge_kernel,
        grid=(t // tm,),
        in_specs=[row(ATTN_DIM), row(MEM_DIM), row(CONV_CH),
                  gspec(0), gspec(1), gspec(2), gbspec(0), gbspec(1), gbspec(2),
                  full(ATTN_DIM, D_MODEL), full(MEM_DIM, D_MODEL), full(CONV_CH, D_MODEL),
                  full(1, D_MODEL)],
        out_specs=row(D_MODEL),
        out_shape=jax.ShapeDtypeStruct((t, D_MODEL), BF16),
        compiler_params=_cparams(("parallel",)),
        name="merge",
    )(attn, memo, convo, p, p, p, gate_b, gate_b, gate_b, wa, wm, wc, cob)


def _outproj_kernel(x_ref, mg_ref, w_ref, g_ref, x1_ref, h2_ref):
    x1 = x_ref[...] + _dot(mg_ref[...], w_ref[...])
    x1_ref[...] = x1
    h2_ref[...] = _rms(x1, g_ref[...]).astype(BF16)


def _outproj(x, merged, w, g, l):
    t = x.shape[0]
    tm = 512
    row = pl.BlockSpec((tm, D_MODEL), lambda i: (i, 0))
    return pl.pallas_call(
        _outproj_kernel,
        grid=(t // tm,),
        in_specs=[row, row,
                  _lp(l, (D_MODEL, D_MODEL), lambda i: (0, 0)),
                  _lp(l, (1, D_MODEL), lambda i: (0, 0))],
        out_specs=[row, row],
        out_shape=[jax.ShapeDtypeStruct((t, D_MODEL), F32), jax.ShapeDtypeStruct((t, D_MODEL), BF16)],
        compiler_params=_cparams(("parallel",)),
        name="outproj",
    )(x, merged, w, g)


def _mlp_kernel(h_ref, x_ref, wu_ref, wd_ref, o_ref):
    @pl.when(pl.program_id(1) == 0)
    def _():
        o_ref[...] = x_ref[...]

    u = jnp.maximum(_dot(h_ref[...], wu_ref[...]), 0.0)
    o_ref[...] += _dot((u * u).astype(BF16), wd_ref[...])


def _mlp(h2, x1, wu, wd, l):
    t = x1.shape[0]
    tm, tf = 512, 1024
    return pl.pallas_call(
        _mlp_kernel,
        grid=(t // tm, FFN_DIM // tf),
        in_specs=[
            pl.BlockSpec((tm, D_MODEL), lambda i, f: (i, 0)),
            pl.BlockSpec((tm, D_MODEL), lambda i, f: (i, 0)),
            _lp(l, (D_MODEL, tf), lambda i, f: (0, f)),
            _lp(l, (tf, D_MODEL), lambda i, f: (f, 0)),
        ],
        out_specs=pl.BlockSpec((tm, D_MODEL), lambda i, f: (i, 0)),
        out_shape=jax.ShapeDtypeStruct((t, D_MODEL), F32),
        compiler_params=_cparams(("parallel", "arbitrary")),
        name="mlp",
    )(h2, x1, wu, wd)


def _split_w_in(w):
    narrow = IDX_DIM + IDX_HEADS
    wb = lax.optimization_barrier(w.astype(BF16))
    head = wb[:, :, :P_HEAD]
    tail = wb[:, :, P_HEAD + narrow:]
    small = jnp.pad(wb[:, :, P_HEAD:P_HEAD + narrow], ((0, 0), (0, 0), (0, PS_WIDTH - narrow)))
    return head, tail, small


def kernel(x, mem, positions, norm1_g, w_in, q_norm_g, k_norm_g, mem_norm_g, w_mem_kv, mq_norm_g,
           mk_norm_g, conv_in_b, conv_w, conv_b, conv_ln_g, conv_ln_b, gate_b, w_attn_o, w_mem_o,
           w_conv_o, conv_o_b, w_out, norm2_g, w_up, w_down):
    batch, seq, _ = x.shape
    depth = w_in.shape[0]
    t = batch * seq
    xf = x.reshape(t, D_MODEL)
    mem2d = mem.reshape(batch * mem.shape[1], D_MODEL)
    tabs = _rope_tables(positions)

    vec = lambda v: v.reshape(depth, 1, -1)
    norm1_g, q_norm_g, k_norm_g, mem_norm_g, mq_norm_g, mk_norm_g = map(
        vec, (norm1_g, q_norm_g, k_norm_g, mem_norm_g, mq_norm_g, mk_norm_g))
    conv_in_b, conv_b, conv_ln_g, conv_ln_b, gate_b, conv_o_b, norm2_g = map(
        vec, (conv_in_b, conv_b, conv_ln_g, conv_ln_b, gate_b, conv_o_b, norm2_g))
    w_head, w_tail, w_small = _split_w_in(w_in)
    w_mem_kv, w_attn_o, w_mem_o, w_conv_o, w_out, w_up, w_down = (
        w.astype(BF16) for w in (w_mem_kv, w_attn_o, w_mem_o, w_conv_o, w_out, w_up, w_down))

    for l in range(depth):
        p, ps = _proj(xf, norm1_g, w_head, w_tail, w_small, l)
        qn, iqr, kn, vt, ika, ikb, iwt = _prep(p, ps, tabs, q_norm_g, k_norm_g, l)
        attn = _dsa(qn, iqr, iwt, kn, vt, ika, ikb, batch, seq)
        mk, mv = _mem_kv(mem2d, mem_norm_g, w_mem_kv, mk_norm_g, batch, l)
        memo = _mem_attn(p, mk, mv, mq_norm_g, batch, seq, l)
        convo = _conv(p, conv_in_b, conv_w, conv_b, conv_ln_g, conv_ln_b, batch, seq, l)
        merged = _merge(attn, memo, convo, p, gate_b, w_attn_o, w_mem_o, w_conv_o, conv_o_b, l)
        x1, h2 = _outproj(xf, merged, w_out, norm2_g, l)
        xf = _mlp(h2, x1, w_up, w_down, l)
    return xf.reshape(batch, seq, D_MODEL)
```

```python
import functools
import math

import jax
import jax.numpy as jnp
from jax import lax
from jax.experimental import pallas as pl
from jax.experimental.pallas import tpu as pltpu

F32 = jnp.float32
BF16 = jnp.bfloat16
I32 = jnp.int32

D_MODEL = 2048
HEAD_DIM = 128
N_HEADS = 8
N_KV_HEADS = 2
IDX_HEADS = 16
IDX_DIM = 64
TOPK = 256
CONV_CH = 512
CONV_WIDTH = 31
MEM_HEADS = 4
FFN_DIM = 4 * D_MODEL
ROPE_THETA = 500000.0
N_BRANCH = 3
EPS = 1e-6
ATTN_DIM = N_HEADS * HEAD_DIM
KV_DIM = N_KV_HEADS * HEAD_DIM
MEM_DIM = MEM_HEADS * HEAD_DIM
HEAD_ROT_HALF = HEAD_DIM // 8
IDX_ROT_HALF = IDX_DIM // 8

LANES = 128
VMEM_LIMIT = 56 * 1024 * 1024

P_Q = 0
P_K = P_Q + ATTN_DIM
P_V = P_K + KV_DIM
P_IQ = P_V + KV_DIM
P_HEAD = P_IQ + IDX_HEADS * IDX_DIM
P_GLU = P_HEAD
P_MQ = P_GLU + 2 * CONV_CH
P_GATES = P_MQ + MEM_DIM
P_WIDTH = P_GATES + N_BRANCH * D_MODEL
P_TAIL = P_WIDTH - P_HEAD
PS_WIDTH = LANES

QB = 128
KC = 512
SUB = 256
SLAB = 32
SUBLANES = 8
LOG2E = 1.4426950408889634
NEG_BIG = -1e30
INT_MIN = -2 ** 31
KEY_NEG_INF = -2139095041
KEY_POS_INF = 2139095040


def _cparams(sem, vmem=VMEM_LIMIT):
    return pltpu.CompilerParams(dimension_semantics=sem, vmem_limit_bytes=vmem)


def _lp(l, shape, imap):
    return pl.BlockSpec((pl.Squeezed(),) + shape, lambda *g: (l,) + imap(*g))


def _dot(a, b):
    return jnp.dot(a, b, preferred_element_type=F32)


def _dot_nt(a, b):
    return lax.dot_general(a, b, (((1,), (1,)), ((), ())), preferred_element_type=F32)


def _rms(xf, g):
    return xf * lax.rsqrt(jnp.mean(xf * xf, axis=-1, keepdims=True) + EPS) * g


def _rope_table_kernel(pos_ref, ch_ref, sh_ref, ci_ref, si_ref):
    pos = pos_ref[...].astype(F32)
    lane = lax.broadcasted_iota(I32, (1, LANES), 1)
    hh, ih = HEAD_ROT_HALF, IDX_ROT_HALF
    is_head = lane < hh
    fi = jnp.where(is_head, lane, lane - hh).astype(F32)
    inv = jnp.exp(fi * jnp.where(is_head, -math.log(ROPE_THETA) / hh, -math.log(ROPE_THETA) / ih))
    ang = pos * inv
    c = jnp.cos(ang)
    s = jnp.sin(ang)

    def place(x, shift):
        return pltpu.roll(x, shift % LANES, 1)

    second = (lane >= hh) & (lane < 2 * hh)
    ch_ref[...] = jnp.where(is_head, c, jnp.where(second, place(c, hh), 1.0))
    sh_ref[...] = jnp.where(is_head, -s, jnp.where(second, place(s, hh), 0.0))
    r = lane & (IDX_DIM - 1)
    upper = lane >= IDX_DIM

    def idx_table(x, sign_first, fill):
        first = jnp.where(upper, place(x, IDX_DIM - hh), place(x, -hh))
        second_half = jnp.where(upper, place(x, IDX_DIM - hh + ih), place(x, ih - hh))
        return jnp.where(r < ih, sign_first * first, jnp.where(r < 2 * ih, second_half, fill))

    ci_ref[...] = idx_table(c, 1.0, 1.0)
    si_ref[...] = idx_table(s, -1.0, 0.0)


def _swap_matrix(period, half):
    r = lax.broadcasted_iota(I32, (LANES, LANES), 0)
    c = lax.broadcasted_iota(I32, (LANES, LANES), 1)
    pos = c & (period - 1)
    src = jnp.where(pos < half, c + half, c - half)
    return jnp.where((pos < 2 * half) & (r == src), 1.0, 0.0).astype(BF16)


def _rope_tables(positions):
    t = positions.size
    tp = 512
    tab = jax.ShapeDtypeStruct((t, LANES), F32)
    spec = pl.BlockSpec((tp, LANES), lambda i: (i, 0))
    return pl.pallas_call(
        _rope_table_kernel,
        grid=(t // tp,),
        in_specs=[pl.BlockSpec((tp, 1), lambda i: (i, 0))],
        out_specs=[spec] * 4,
        out_shape=[tab] * 4,
        compiler_params=_cparams(("parallel",)),
        name="rope_tables",
    )(positions.reshape(t, 1))


def _rope(t, c, s, swap):
    hi = t.astype(BF16)
    lo = (t - hi.astype(F32)).astype(BF16)
    return t * c + (_dot(hi, swap) + _dot(lo, swap)) * s


PROJ_TN = 1280
PROJ_ROWS = 256


def _proj_kernel(x_ref, g_ref, wh_ref, wt_ref, ws_ref, o_ref, os_ref, h_ref):
    j = pl.program_id(1)

    @pl.when(j == 0)
    def _():
        for r in range(x_ref.shape[0] // PROJ_ROWS):
            rows = slice(r * PROJ_ROWS, (r + 1) * PROJ_ROWS)
            h = _rms(x_ref[rows, :], g_ref[...]).astype(BF16)
            h_ref[rows, :] = h
            os_ref[rows, :] = _dot(h, ws_ref[...])
            o_ref[rows, :] = _dot(h, wh_ref[...]).astype(o_ref.dtype)

    @pl.when((j > 0) & (j < P_HEAD // PROJ_TN))
    def _():
        o_ref[...] = _dot(h_ref[...], wh_ref[...]).astype(o_ref.dtype)

    @pl.when(j >= P_HEAD // PROJ_TN)
    def _():
        o_ref[...] = _dot(h_ref[...], wt_ref[...]).astype(o_ref.dtype)


def _proj(x, g, wh, wt, ws, l):
    t = x.shape[0]
    tm, tn = 1024, PROJ_TN
    n_head = P_HEAD // tn
    return pl.pallas_call(
        _proj_kernel,
        grid=(t // tm, P_WIDTH // tn),
        in_specs=[
            pl.BlockSpec((tm, D_MODEL), lambda i, j: (i, 0)),
            _lp(l, (1, D_MODEL), lambda i, j: (0, 0)),
            _lp(l, (D_MODEL, tn), lambda i, j: (0, jnp.minimum(j, n_head - 1))),
            _lp(l, (D_MODEL, tn), lambda i, j: (0, jnp.maximum(j - n_head, 0))),
            _lp(l, (D_MODEL, PS_WIDTH), lambda i, j: (0, 0)),
        ],
        out_specs=[
            pl.BlockSpec((tm, tn), lambda i, j: (i, j)),
            pl.BlockSpec((tm, PS_WIDTH), lambda i, j: (i, 0)),
        ],
        out_shape=[
            jax.ShapeDtypeStruct((t, P_WIDTH), BF16),
            jax.ShapeDtypeStruct((t, PS_WIDTH), F32),
        ],
        scratch_shapes=[pltpu.VMEM((tm, D_MODEL), BF16)],
        compiler_params=_cparams(("parallel", "arbitrary")),
        name="proj",
    )(x, g, wh, wt, ws)


def _prep_kernel(q_ref, iql_ref, iqh_ref, kv_ref, ps_ref, ch_ref, sh_ref, ci_ref, si_ref,
                 qg_ref, kg_ref, qn_ref, iqr_ref, kn_ref, vt_ref, ika_ref, ikb_ref, iwt_ref):
    ch, sh = ch_ref[...], sh_ref[...]
    ci, si = ci_ref[...], si_ref[...]
    swap_h = _swap_matrix(HEAD_DIM, HEAD_ROT_HALF)
    swap_i = _swap_matrix(IDX_DIM, IDX_ROT_HALF)
    scale = (HEAD_DIM ** -0.5) * LOG2E
    half_groups = N_HEADS // 2
    for h in range(N_HEADS):
        sl = slice(h * HEAD_DIM, (h + 1) * HEAD_DIM)
        qh = _rms(q_ref[:, sl].astype(F32), qg_ref[...])
        qn_ref[:, sl] = (_rope(qh, ch, sh, swap_h) * scale).astype(BF16)
        src = iql_ref if h < half_groups else iqh_ref
        hs = h % half_groups
        iqh = src[:, hs * LANES:(hs + 1) * LANES].astype(F32)
        iqr_ref[:, sl] = _rope(iqh, ci, si, swap_i).astype(BF16)
    for g in range(N_KV_HEADS):
        sl = slice(g * HEAD_DIM, (g + 1) * HEAD_DIM)
        kh = _rms(kv_ref[:, sl].astype(F32), kg_ref[...])
        kn_ref[:, sl] = _rope(kh, ch, sh, swap_h).astype(BF16)
        vh = kv_ref[:, KV_DIM + g * HEAD_DIM:KV_DIM + (g + 1) * HEAD_DIM].astype(F32)
        vt_ref[0, sl, :] = vh.T.astype(BF16)
    ps = ps_ref[...]
    lane = lax.broadcasted_iota(I32, (1, LANES), 1)
    ikr = jnp.where(lane < IDX_DIM, _rope(ps, ci, si, swap_i), 0.0)
    ika_ref[...] = ikr.astype(BF16)
    ikb_ref[...] = pltpu.roll(ikr, IDX_DIM, 1).astype(BF16)
    idx_scale = (IDX_DIM ** -0.5) * (IDX_HEADS ** -0.5)
    iws = jnp.where(lane < IDX_HEADS, pltpu.roll(ps, LANES - IDX_DIM, 1) * idx_scale, 0.0)
    iwt_ref[...] = iws.T


def _prep(p, ps, tabs, qg, kg, l):
    t = p.shape[0]
    tp = KC
    half_iq = IDX_HEADS * IDX_DIM // 2
    tab_spec = pl.BlockSpec((tp, LANES), lambda i: (i, 0))
    vec_spec = _lp(l, (1, HEAD_DIM), lambda i: (0, 0))

    def out(width, dtype):
        return (pl.BlockSpec((tp, width), lambda i: (i, 0)), jax.ShapeDtypeStruct((t, width), dtype))

    outs = [out(ATTN_DIM, BF16), out(ATTN_DIM, BF16), out(KV_DIM, BF16),
            (pl.BlockSpec((1, KV_DIM, tp), lambda i: (i, 0, 0)),
             jax.ShapeDtypeStruct((t // tp, KV_DIM, tp), BF16)),
            out(LANES, BF16), out(LANES, BF16),
            (pl.BlockSpec((LANES, tp), lambda i: (0, i)), jax.ShapeDtypeStruct((LANES, t), F32))]
    return pl.pallas_call(
        _prep_kernel,
        grid=(t // tp,),
        in_specs=[
            pl.BlockSpec((tp, ATTN_DIM), lambda i: (i, P_Q // ATTN_DIM)),
            pl.BlockSpec((tp, half_iq), lambda i: (i, P_IQ // half_iq)),
            pl.BlockSpec((tp, half_iq), lambda i: (i, P_IQ // half_iq + 1)),
            pl.BlockSpec((tp, 2 * KV_DIM), lambda i: (i, P_K // (2 * KV_DIM))),
            tab_spec,
        ] + [tab_spec] * 4 + [vec_spec, vec_spec],
        out_specs=[o[0] for o in outs],
        out_shape=[o[1] for o in outs],
        compiler_params=_cparams(("parallel",)),
        name="prep",
    )(p, p, p, p, ps, *tabs, qg, kg)


def _sortable(x):
    bits = lax.bitcast_convert_type(x, I32)
    return bits ^ ((bits >> 31) & 0x7FFFFFFF)


N_PAIRS = N_HEADS // 2
N_PART = 4


def _dsa_kernel(qn_ref, iq_ref, iwt_ref, kn_ref, vt_ref, ika_ref, ikb_ref, o_ref,
                key_ref, thr_ref, nge_ref, bias_ref, m_ref, l_ref, acc_ref, sa_ref, sb_ref, p_ref):
    j = pl.program_id(1)
    nck = j // (KC // QB) + 1
    seq = kn_ref.shape[0]

    iwt = iwt_ref[...]
    q_pos = j * QB + lax.broadcasted_iota(I32, (SUB, QB), 1)
    n_grp = IDX_HEADS // 2

    halves = KC // SUB
    nhalf = j // (SUB // QB) + 1

    def score_half(c, hf):
        start = pl.multiple_of(c * KC + hf * SUB, SUB)
        ka = ika_ref[pl.ds(start, SUB), :]
        kb = ikb_ref[pl.ds(start, SUB), :]
        acc = jnp.zeros((SUB, QB), F32)
        for a in range(n_grp // 2):
            b = a + n_grp // 2
            iq2 = jnp.concatenate([iq_ref[:, a * LANES:(a + 1) * LANES],
                                   iq_ref[:, b * LANES:(b + 1) * LANES]], axis=0)
            sa = _dot_nt(ka, iq2)
            sb = _dot_nt(kb, iq2)
            acc = acc + jnp.maximum(sa[:, :QB], 0.0) * iwt[2 * a:2 * a + 1, :]
            acc = acc + jnp.maximum(sb[:, :QB], 0.0) * iwt[2 * a + 1:2 * a + 2, :]
            acc = acc + jnp.maximum(sa[:, QB:], 0.0) * iwt[2 * b:2 * b + 1, :]
            acc = acc + jnp.maximum(sb[:, QB:], 0.0) * iwt[2 * b + 1:2 * b + 2, :]
        k_pos = start + lax.broadcasted_iota(I32, (SUB, QB), 0)
        acc = jnp.where(k_pos <= q_pos, acc, -jnp.inf)
        key_ref[c, hf * SUB:(hf + 1) * SUB, :] = _sortable(acc)

    def score_chunk(c, carry):
        for hf in range(halves):
            score_half(c, hf)
        return carry

    lax.fori_loop(0, nhalf // halves, score_chunk, 0)

    @pl.when(nhalf % halves == 1)
    def _():
        score_half(nck - 1, 0)
        key_ref[nck - 1, SUB:, :] = jnp.full((KC - SUB, QB), INT_MIN, I32)

    n_loaded = jnp.broadcast_to(nhalf * SUB, (SUBLANES, QB)).astype(I32)

    def search(nh):
        def search_pass(p, carry):
            tu, n_at = carry
            cand_u = tu | lax.shift_left(jnp.int32(1), 31 - p)
            cand = cand_u ^ INT_MIN
            parts = [jnp.zeros((SUBLANES, QB), I32)] * N_PART
            for r in range(nh * SUB // SUBLANES):
                row = (r * SUBLANES) % KC
                k = key_ref[(r * SUBLANES) // KC, row:row + SUBLANES, :]
                parts[r % N_PART] = parts[r % N_PART] + jnp.where(k >= cand, 1, 0)
            tot = (parts[0] + parts[1]) + (parts[2] + parts[3])
            n = jnp.broadcast_to(jnp.sum(tot, axis=0, keepdims=True), (SUBLANES, QB))
            ok = n >= TOPK
            return jnp.where(ok, cand_u, tu), jnp.where(ok, n, n_at)

        tu, n_ge = lax.fori_loop(0, 32, search_pass, (jnp.zeros((SUBLANES, QB), I32), n_loaded))
        thr_ref[...] = tu ^ INT_MIN
        nge_ref[...] = n_ge

    for nh in range(1, key_ref.shape[0] * halves + 1):
        pl.when(nhalf == nh)(functools.partial(search, nh))
    thr = thr_ref[...]
    n_ge = nge_ref[...]

    row_iota = lax.broadcasted_iota(I32, (SUBLANES, QB), 0)
    tied = (n_ge > TOPK) & (thr > KEY_NEG_INF)
    any_tied = jnp.max(jnp.where(tied, 1, 0)) > 0

    def count32(pred):
        def chunk(c, parts):
            parts = list(parts)
            for r in range(KC // SUBLANES):
                k = key_ref[c, r * SUBLANES:(r + 1) * SUBLANES, :]
                hit = pred(k, c * KC + r * SUBLANES)
                parts[r % N_PART] = parts[r % N_PART] + jnp.where(hit, 1, 0)
            return tuple(parts)

        zero = jnp.zeros((SUBLANES, QB), I32)
        parts = lax.fori_loop(0, nck, chunk, (zero,) * N_PART)
        tot = (parts[0] + parts[1]) + (parts[2] + parts[3])
        return jnp.broadcast_to(jnp.sum(tot, axis=0, keepdims=True), (SUBLANES, QB))

    thr_fin = jnp.maximum(thr, KEY_NEG_INF + 1)

    @pl.when(jnp.logical_not(any_tied))
    def _():
        def bias_chunk(c, carry):
            for r in range(KC // SUBLANES):
                sl = slice(r * SUBLANES, (r + 1) * SUBLANES)
                k = key_ref[c, sl, :]
                sel = (k >= thr_fin) & (k < KEY_POS_INF)
                bias_ref[c, sl, :] = jnp.where(sel, 0.0, NEG_BIG)
            return carry

        lax.fori_loop(0, nck, bias_chunk, 0)

    @pl.when(any_tied)
    def _():
        need = TOPK - count32(lambda k, _: k > thr)

        def pos_pass(p, x):
            cand = x | lax.shift_left(jnp.int32(1), (seq.bit_length() - 2) - p)
            n = count32(lambda k, r0: (k == thr) & (row_iota + r0 < cand))
            return jnp.where(n < need, cand, x)

        xlim = lax.fori_loop(0, seq.bit_length() - 1, pos_pass, jnp.zeros((SUBLANES, QB), I32))

        def bias_chunk(c, carry):
            for r in range(KC // SUBLANES):
                sl = slice(r * SUBLANES, (r + 1) * SUBLANES)
                k = key_ref[c, sl, :]
                pos = row_iota + (c * KC + r * SUBLANES)
                sel = (k > thr) | ((k == thr) & (pos <= xlim))
                sel = sel & (k >= thr_fin) & (k < KEY_POS_INF)
                bias_ref[c, sl, :] = jnp.where(sel, 0.0, NEG_BIG)
            return carry

        lax.fori_loop(0, nck, bias_chunk, 0)

    m_ref[...] = jnp.full(m_ref.shape, NEG_BIG, F32)
    l_ref[...] = jnp.zeros(l_ref.shape, F32)
    acc_ref[...] = jnp.zeros(acc_ref.shape, F32)
    group = N_HEADS // N_KV_HEADS

    def logits(c, s_ref):
        col_max = []
        for pr in range(N_PAIRS):
            g = (2 * pr) // group
            q2 = jnp.concatenate([qn_ref[:, (2 * pr) * HEAD_DIM:(2 * pr + 1) * HEAD_DIM],
                                  qn_ref[:, (2 * pr + 1) * HEAD_DIM:(2 * pr + 2) * HEAD_DIM]],
                                 axis=0)
            mx = jnp.full((SUBLANES, 2 * QB), NEG_BIG, F32)
            for hf in range(KC // SUB):
                start = pl.multiple_of(c * KC + hf * SUB, SUB)
                kc = kn_ref[pl.ds(start, SUB), g * HEAD_DIM:(g + 1) * HEAD_DIM]
                bias = bias_ref[c, hf * SUB:(hf + 1) * SUB, :]
                s = _dot_nt(kc, q2) + jnp.concatenate([bias, bias], axis=1)
                s_ref[pr, hf * SUB:(hf + 1) * SUB, :] = s
                for r in range(SUB // SUBLANES):
                    mx = jnp.maximum(mx, s[r * SUBLANES:(r + 1) * SUBLANES, :])
            col_max.append(jnp.broadcast_to(jnp.max(mx, axis=0, keepdims=True), (SUBLANES, 2 * QB)))
        return tuple(col_max)

    def finish(c, s_ref, col_max):
        alphas = []
        for pr in range(N_PAIRS):
            m_prev = m_ref[pr]
            m_new = jnp.maximum(m_prev, col_max[pr])
            alpha = jnp.exp2(m_prev - m_new)
            lsum = jnp.zeros((SUBLANES, 2 * QB), F32)
            for t in range(KC // SLAB):
                rows = slice(t * SLAB, (t + 1) * SLAB)
                p = jnp.exp2(s_ref[pr, rows, :] - m_new[0:1, :])
                for r in range(SLAB // SUBLANES):
                    lsum = lsum + p[r * SUBLANES:(r + 1) * SUBLANES, :]
                p_ref[pr, rows, :] = p.astype(BF16)
            m_ref[pr] = m_new
            l_ref[pr] = alpha * l_ref[pr] + jnp.sum(lsum, axis=0, keepdims=True)
            alphas.append(alpha)
        for pr in range(N_PAIRS):
            g = (2 * pr) // group
            vt = vt_ref[c, g * HEAD_DIM:(g + 1) * HEAD_DIM, :]
            acc_ref[pr] = acc_ref[pr] * alphas[pr][0:1, :] + _dot(vt, p_ref[pr])

    def attn_step(i, col_max):
        c = 2 * i
        mid = logits(c + 1, sb_ref)
        finish(c, sa_ref, col_max)
        nxt = logits(c + 2, sa_ref)
        finish(c + 1, sb_ref, mid)
        return nxt

    n_steps = (nck - 1) // 2
    tail = 2 * n_steps
    col_max = lax.fori_loop(0, n_steps, attn_step, logits(0, sa_ref))

    @pl.when(tail == nck - 1)
    def _():
        finish(tail, sa_ref, col_max)

    @pl.when(tail != nck - 1)
    def _():
        mid = logits(tail + 1, sb_ref)
        finish(tail, sa_ref, col_max)
        finish(tail + 1, sb_ref, mid)

    for pr in range(N_PAIRS):
        o_t = acc_ref[pr] / l_ref[pr][0:1, :]
        for i in range(2):
            h = 2 * pr + i
            o_ref[:, h * HEAD_DIM:(h + 1) * HEAD_DIM] = o_t[:, i * QB:(i + 1) * QB].T.astype(BF16)


def _dsa(qn, iqr, iwt, kn, vt, ika, ikb, batch, seq):
    t = qn.shape[0]
    nb = seq // QB
    nchunk = seq // KC
    qmap = lambda b, j: (b * nb + j, 0)
    bmap = lambda b, j: (b, 0)
    return pl.pallas_call(
        _dsa_kernel,
        grid=(batch, nb),
        in_specs=[
            pl.BlockSpec((QB, ATTN_DIM), qmap),
            pl.BlockSpec((QB, ATTN_DIM), qmap),
            pl.BlockSpec((IDX_HEADS, QB), lambda b, j: (0, b * nb + j)),
            pl.BlockSpec((seq, KV_DIM), bmap),
            pl.BlockSpec((nchunk, KV_DIM, KC), lambda b, j: (b, 0, 0)),
            pl.BlockSpec((seq, LANES), bmap),
            pl.BlockSpec((seq, LANES), bmap),
        ],
        out_specs=pl.BlockSpec((QB, ATTN_DIM), qmap),
        out_shape=jax.ShapeDtypeStruct((t, ATTN_DIM), BF16),
        scratch_shapes=[
            pltpu.VMEM((nchunk, KC, QB), I32),
            pltpu.VMEM((SUBLANES, QB), I32),
            pltpu.VMEM((SUBLANES, QB), I32),
            pltpu.VMEM((nchunk, KC, QB), F32),
            pltpu.VMEM((N_PAIRS, SUBLANES, 2 * QB), F32),
            pltpu.VMEM((N_PAIRS, SUBLANES, 2 * QB), F32),
            pltpu.VMEM((N_PAIRS, HEAD_DIM, 2 * QB), F32),
            pltpu.VMEM((N_PAIRS, KC, 2 * QB), F32),
            pltpu.VMEM((N_PAIRS, KC, 2 * QB), F32),
            pltpu.VMEM((N_PAIRS, KC, 2 * QB), BF16),
        ],
        compiler_params=_cparams(("parallel", "arbitrary")),
        name="dsa",
    )(qn, iqr, iwt, kn, vt, ika, ikb)


def _mem_kv_kernel(mem_ref, g_ref, w_ref, kg_ref, mk_ref, mv_ref):
    m = _rms(mem_ref[...], g_ref[...]).astype(BF16)
    kv = _dot(m, w_ref[...])
    for h in range(MEM_HEADS):
        sl = slice(h * HEAD_DIM, (h + 1) * HEAD_DIM)
        mk_ref[:, sl] = _rms(kv[:, sl], kg_ref[...]).astype(BF16)
    mv_ref[...] = kv[:, MEM_DIM:].astype(BF16)


def _mem_kv(mem2d, g, w, kg, batch, l):
    n = mem2d.shape[0]
    m = n // batch
    out = jax.ShapeDtypeStruct((n, MEM_DIM), BF16)
    ospec = pl.BlockSpec((m, MEM_DIM), lambda b: (b, 0))
    return pl.pallas_call(
        _mem_kv_kernel,
        grid=(batch,),
        in_specs=[
            pl.BlockSpec((m, D_MODEL), lambda b: (b, 0)),
            _lp(l, (1, D_MODEL), lambda b: (0, 0)),
            _lp(l, (D_MODEL, 2 * MEM_DIM), lambda b: (0, 0)),
            _lp(l, (1, HEAD_DIM), lambda b: (0, 0)),
        ],
        out_specs=[ospec, ospec],
        out_shape=[out, out],
        compiler_params=_cparams(("parallel",)),
        name="mem_kv",
    )(mem2d, g, w, kg)


def _mem_attn_kernel(q_ref, mk_ref, mv_ref, qg_ref, o_ref):
    scale = HEAD_DIM ** -0.5
    for h in range(MEM_HEADS):
        sl = slice(h * HEAD_DIM, (h + 1) * HEAD_DIM)
        qh = (_rms(q_ref[:, sl].astype(F32), qg_ref[...]) * scale).astype(BF16)
        s = _dot_nt(qh, mk_ref[:, sl])
        p = jnp.exp(s - jnp.max(s, axis=1, keepdims=True))
        l = jnp.sum(p, axis=1, keepdims=True)
        o = _dot((p / l).astype(BF16), mv_ref[:, sl])
        o_ref[:, sl] = o.astype(BF16)


def _mem_attn(p, mk, mv, qg, batch, seq, l):
    t = p.shape[0]
    tq = 512
    nq = seq // tq
    m = mk.shape[0] // batch
    return pl.pallas_call(
        _mem_attn_kernel,
        grid=(batch, nq),
        in_specs=[
            pl.BlockSpec((tq, MEM_DIM), lambda b, i: (b * nq + i, P_MQ // MEM_DIM)),
            pl.BlockSpec((m, MEM_DIM), lambda b, i: (b, 0)),
            pl.BlockSpec((m, MEM_DIM), lambda b, i: (b, 0)),
            _lp(l, (1, HEAD_DIM), lambda b, i: (0, 0)),
        ],
        out_specs=pl.BlockSpec((tq, MEM_DIM), lambda b, i: (b * nq + i, 0)),
        out_shape=jax.ShapeDtypeStruct((t, MEM_DIM), BF16),
        compiler_params=_cparams(("parallel", "parallel")),
        name="mem_attn",
    )(p, mk, mv, qg)


CONV_HALO = 32
CONV_ROWS = 64


def _conv_kernel(cur_a_ref, cur_g_ref, prev_a_ref, prev_g_ref, inb_ref, cw_ref, cb_ref, lg_ref,
                 lb_ref, o_ref, u_ref, sh_ref):
    ts = cur_a_ref.shape[0]
    n = CONV_HALO + ts

    def glu(a, g):
        a = a.astype(F32) + inb_ref[:, :CONV_CH]
        g = g.astype(F32) + inb_ref[:, CONV_CH:]
        return a * jax.nn.sigmoid(g)

    u_prev = glu(prev_a_ref[ts - CONV_HALO:, :], prev_g_ref[ts - CONV_HALO:, :])
    u_ref[:CONV_HALO, :] = jnp.where(pl.program_id(1) == 0, 0.0, u_prev)
    u_ref[CONV_HALO:n, :] = glu(cur_a_ref[...], cur_g_ref[...])
    u_ref[n:, :] = jnp.zeros((SUBLANES, CONV_CH), F32)
    for r in range(1, SUBLANES):
        sh_ref[r - 1] = u_ref[r:r + n, :]

    first_tap = CONV_HALO - (CONV_WIDTH - 1)
    for t in range(ts // CONV_ROWS):
        base = t * CONV_ROWS
        y = jnp.broadcast_to(cb_ref[...], (CONV_ROWS, CONV_CH))
        for w in range(CONV_WIDTH):
            r = (first_tap + w) % SUBLANES
            a = base + first_tap + w - r
            rows = u_ref[a:a + CONV_ROWS, :] if r == 0 else sh_ref[r - 1, a:a + CONV_ROWS, :]
            y = y + rows * cw_ref[w:w + 1, :]
        mu = jnp.mean(y, axis=-1, keepdims=True)
        d = y - mu
        var = jnp.mean(d * d, axis=-1, keepdims=True)
        z = d * lax.rsqrt(var + EPS) * lg_ref[...] + lb_ref[...]
        o_ref[base:base + CONV_ROWS, :] = (z * jax.nn.sigmoid(z)).astype(BF16)


def _conv(p, inb, cw, cb, lg, lb, batch, seq, l):
    t = p.shape[0]
    ts = 512
    ns = seq // ts
    a_blk = P_GLU // CONV_CH
    vec = lambda w: _lp(l, (1, w), lambda b, i: (0, 0))
    cur = lambda k: pl.BlockSpec((ts, CONV_CH), lambda b, i: (b * ns + i, a_blk + k))
    prev = lambda k: pl.BlockSpec((ts, CONV_CH), lambda b, i: (b * ns + jnp.maximum(i - 1, 0), a_blk + k))
    return pl.pallas_call(
        _conv_kernel,
        grid=(batch, ns),
        in_specs=[
            cur(0), cur(1), prev(0), prev(1),
            vec(2 * CONV_CH),
            _lp(l, (CONV_WIDTH, CONV_CH), lambda b, i: (0, 0)),
            vec(CONV_CH), vec(CONV_CH), vec(CONV_CH),
        ],
        out_specs=pl.BlockSpec((ts, CONV_CH), lambda b, i: (b * ns + i, 0)),
        out_shape=jax.ShapeDtypeStruct((t, CONV_CH), BF16),
        scratch_shapes=[pltpu.VMEM((CONV_HALO + ts + SUBLANES, CONV_CH), F32),
                        pltpu.VMEM((SUBLANES - 1, CONV_HALO + ts, CONV_CH), F32)],
        compiler_params=_cparams(("parallel", "parallel")),
        name="conv",
    )(p, p, p, p, inb, cw, cb, lg, lb)


MERGE_TN = 512


def _merge_kernel(a_ref, m_ref, c_ref, g0_ref, g1_ref, g2_ref, gb0_ref, gb1_ref, gb2_ref,
                  wa_ref, wm_ref, wc_ref, cob_ref, o_ref):
    a, m, c = a_ref[...], m_ref[...], c_ref[...]
    for n in range(D_MODEL // MERGE_TN):
        sl = slice(n * MERGE_TN, (n + 1) * MERGE_TN)

        def gate(g_ref, gb_ref):
            return jax.nn.sigmoid(g_ref[:, sl].astype(F32) + gb_ref[:, sl])

        y = gate(g0_ref, gb0_ref) * _dot(a, wa_ref[:, sl])
        y = y + gate(g1_ref, gb1_ref) * _dot(m, wm_ref[:, sl])
        y = y + gate(g2_ref, gb2_ref) * (_dot(c, wc_ref[:, sl]) + cob_ref[:, sl])
        o_ref[:, sl] = y.astype(BF16)


def _merge(attn, memo, convo, p, gate_b, wa, wm, wc, cob, l):
    t = attn.shape[0]
    tm = 512
    gblk = P_GATES // D_MODEL
    row = lambda w: pl.BlockSpec((tm, w), lambda i: (i, 0))
    full = lambda r, c: _lp(l, (r, c), lambda i: (0, 0))
    gspec = lambda k: pl.BlockSpec((tm, D_MODEL), lambda i: (i, gblk + k))
    gbspec = lambda k: _lp(l, (1, D_MODEL), lambda i: (0, k))
    return pl.pallas_call(
        _merge_kernel,
        grid=(t // tm,),
        in_specs=[row(ATTN_DIM), row(MEM_DIM), row(CONV_CH),
                  gspec(0), gspec(1), gspec(2), gbspec(0), gbspec(1), gbspec(2),
                  full(ATTN_DIM, D_MODEL), full(MEM_DIM, D_MODEL), full(CONV_CH, D_MODEL),
                  full(1, D_MODEL)],
        out_specs=row(D_MODEL),
        out_shape=jax.ShapeDtypeStruct((t, D_MODEL), BF16),
        compiler_params=_cparams(("parallel",)),
        name="merge",
    )(attn, memo, convo, p, p, p, gate_b, gate_b, gate_b, wa, wm, wc, cob)


OUTPROJ_ROWS = 128


def _outproj_kernel(x_ref, mg_ref, w_ref, g_ref, x1_ref, h2_ref):
    for r in range(x_ref.shape[0] // OUTPROJ_ROWS):
        rows = slice(r * OUTPROJ_ROWS, (r + 1) * OUTPROJ_ROWS)
        x1 = x_ref[rows, :] + _dot(mg_ref[rows, :], w_ref[...])
        x1_ref[rows, :] = x1
        h2_ref[rows, :] = _rms(x1, g_ref[...]).astype(BF16)


def _outproj(x, merged, w, g, l):
    t = x.shape[0]
    tm = 512
    row = pl.BlockSpec((tm, D_MODEL), lambda i: (i, 0))
    return pl.pallas_call(
        _outproj_kernel,
        grid=(t // tm,),
        in_specs=[row, row,
                  _lp(l, (D_MODEL, D_MODEL), lambda i: (0, 0)),
                  _lp(l, (1, D_MODEL), lambda i: (0, 0))],
        out_specs=[row, row],
        out_shape=[jax.ShapeDtypeStruct((t, D_MODEL), F32), jax.ShapeDtypeStruct((t, D_MODEL), BF16)],
        compiler_params=_cparams(("parallel",)),
        name="outproj",
    )(x, merged, w, g)


def _mlp_kernel(h_ref, x_ref, wu_ref, wd_ref, o_ref):
    @pl.when(pl.program_id(1) == 0)
    def _():
        o_ref[...] = x_ref[...]

    u = jnp.maximum(_dot(h_ref[...], wu_ref[...]), 0.0)
    o_ref[...] += _dot((u * u).astype(BF16), wd_ref[...])


def _mlp(h2, x1, wu, wd, l):
    t = x1.shape[0]
    tm, tf = 512, 1024
    return pl.pallas_call(
        _mlp_kernel,
        grid=(t // tm, FFN_DIM // tf),
        in_specs=[
            pl.BlockSpec((tm, D_MODEL), lambda i, f: (i, 0)),
            pl.BlockSpec((tm, D_MODEL), lambda i, f: (i, 0)),
            _lp(l, (D_MODEL, tf), lambda i, f: (0, f)),
            _lp(l, (tf, D_MODEL), lambda i, f: (f, 0)),
        ],
        out_specs=pl.BlockSpec((tm, D_MODEL), lambda i, f: (i, 0)),
        out_shape=jax.ShapeDtypeStruct((t, D_MODEL), F32),
        compiler_params=_cparams(("parallel", "arbitrary")),
        name="mlp",
    )(h2, x1, wu, wd)


def _split_w_in(w):
    narrow = IDX_DIM + IDX_HEADS
    wb = w.astype(BF16)
    head = wb[:, :, :P_HEAD]
    tail = wb[:, :, P_HEAD + narrow:]
    small = jnp.pad(wb[:, :, P_HEAD:P_HEAD + narrow], ((0, 0), (0, 0), (0, PS_WIDTH - narrow)))
    return head, tail, small


def kernel(x, mem, positions, norm1_g, w_in, q_norm_g, k_norm_g, mem_norm_g, w_mem_kv, mq_norm_g,
           mk_norm_g, conv_in_b, conv_w, conv_b, conv_ln_g, conv_ln_b, gate_b, w_attn_o, w_mem_o,
           w_conv_o, conv_o_b, w_out, norm2_g, w_up, w_down):
    batch, seq, _ = x.shape
    depth = w_in.shape[0]
    t = batch * seq
    xf = x.reshape(t, D_MODEL)
    mem2d = mem.reshape(batch * mem.shape[1], D_MODEL)
    tabs = _rope_tables(positions)

    vec = lambda v: v.reshape(depth, 1, -1)
    norm1_g, q_norm_g, k_norm_g, mem_norm_g, mq_norm_g, mk_norm_g = map(
        vec, (norm1_g, q_norm_g, k_norm_g, mem_norm_g, mq_norm_g, mk_norm_g))
    conv_in_b, conv_b, conv_ln_g, conv_ln_b, gate_b, conv_o_b, norm2_g = map(
        vec, (conv_in_b, conv_b, conv_ln_g, conv_ln_b, gate_b, conv_o_b, norm2_g))
    w_head, w_tail, w_small = _split_w_in(w_in)
    w_mem_kv, w_attn_o, w_mem_o, w_conv_o, w_out, w_up, w_down = (
        w.astype(BF16) for w in (w_mem_kv, w_attn_o, w_mem_o, w_conv_o, w_out, w_up, w_down))

    for l in range(depth):
        p, ps = _proj(xf, norm1_g, w_head, w_tail, w_small, l)
        qn, iqr, kn, vt, ika, ikb, iwt = _prep(p, ps, tabs, q_norm_g, k_norm_g, l)
        attn = _dsa(qn, iqr, iwt, kn, vt, ika, ikb, batch, seq)
        mk, mv = _mem_kv(mem2d, mem_norm_g, w_mem_kv, mk_norm_g, batch, l)
        memo = _mem_attn(p, mk, mv, mq_norm_g, batch, seq, l)
        convo = _conv(p, conv_in_b, conv_w, conv_b, conv_ln_g, conv_ln_b, batch, seq, l)
        merged = _merge(attn, memo, convo, p, gate_b, w_attn_o, w_mem_o, w_conv_o, conv_o_b, l)
        x1, h2 = _outproj(xf, merged, w_out, norm2_g, l)
        xf = _mlp(h2, x1, w_up, w_down, l)
    return xf.reshape(batch, seq, D_MODEL)
```

```python
import functools
import math

import jax
import jax.numpy as jnp
from jax import lax
from jax.experimental import pallas as pl
from jax.experimental.pallas import tpu as pltpu

F32 = jnp.float32
BF16 = jnp.bfloat16
I32 = jnp.int32

D_MODEL = 2048
HEAD_DIM = 128
N_HEADS = 8
N_KV_HEADS = 2
IDX_HEADS = 16
IDX_DIM = 64
TOPK = 256
CONV_CH = 512
CONV_WIDTH = 31
MEM_HEADS = 4
FFN_DIM = 4 * D_MODEL
ROPE_THETA = 500000.0
N_BRANCH = 3
EPS = 1e-6
ATTN_DIM = N_HEADS * HEAD_DIM
KV_DIM = N_KV_HEADS * HEAD_DIM
MEM_DIM = MEM_HEADS * HEAD_DIM
HEAD_ROT_HALF = HEAD_DIM // 8
IDX_ROT_HALF = IDX_DIM // 8

LANES = 128
VMEM_LIMIT = 56 * 1024 * 1024

P_Q = 0
P_K = P_Q + ATTN_DIM
P_V = P_K + KV_DIM
P_IQ = P_V + KV_DIM
P_HEAD = P_IQ + IDX_HEADS * IDX_DIM
P_GLU = P_HEAD
P_MQ = P_GLU + 2 * CONV_CH
P_GATES = P_MQ + MEM_DIM
P_WIDTH = P_GATES + N_BRANCH * D_MODEL
P_TAIL = P_WIDTH - P_HEAD
PS_WIDTH = LANES

QB = 128
KC = 512
SUB = 256
SLAB = 32
SUBLANES = 8
LOG2E = 1.4426950408889634
NEG_BIG = -1e30
INT_MIN = -2 ** 31
KEY_NEG_INF = -2139095041
KEY_POS_INF = 2139095040


def _cparams(sem, vmem=VMEM_LIMIT):
    return pltpu.CompilerParams(dimension_semantics=sem, vmem_limit_bytes=vmem)


def _lp(l, shape, imap):
    return pl.BlockSpec((pl.Squeezed(),) + shape, lambda *g: (l,) + imap(*g))


def _dot(a, b):
    return jnp.dot(a, b, preferred_element_type=F32)


def _dot_nt(a, b):
    return lax.dot_general(a, b, (((1,), (1,)), ((), ())), preferred_element_type=F32)


def _rms(xf, g):
    return xf * lax.rsqrt(jnp.mean(xf * xf, axis=-1, keepdims=True) + EPS) * g


def _rope_table_kernel(pos_ref, ch_ref, sh_ref, ci_ref, si_ref):
    pos = pos_ref[...].astype(F32)
    lane = lax.broadcasted_iota(I32, (1, LANES), 1)
    hh, ih = HEAD_ROT_HALF, IDX_ROT_HALF
    is_head = lane < hh
    fi = jnp.where(is_head, lane, lane - hh).astype(F32)
    inv = jnp.exp(fi * jnp.where(is_head, -math.log(ROPE_THETA) / hh, -math.log(ROPE_THETA) / ih))
    ang = pos * inv
    c = jnp.cos(ang)
    s = jnp.sin(ang)

    def place(x, shift):
        return pltpu.roll(x, shift % LANES, 1)

    second = (lane >= hh) & (lane < 2 * hh)
    ch_ref[...] = jnp.where(is_head, c, jnp.where(second, place(c, hh), 1.0))
    sh_ref[...] = jnp.where(is_head, -s, jnp.where(second, place(s, hh), 0.0))
    r = lane & (IDX_DIM - 1)
    upper = lane >= IDX_DIM

    def idx_table(x, sign_first, fill):
        first = jnp.where(upper, place(x, IDX_DIM - hh), place(x, -hh))
        second_half = jnp.where(upper, place(x, IDX_DIM - hh + ih), place(x, ih - hh))
        return jnp.where(r < ih, sign_first * first, jnp.where(r < 2 * ih, second_half, fill))

    ci_ref[...] = idx_table(c, 1.0, 1.0)
    si_ref[...] = idx_table(s, -1.0, 0.0)


def _swap_matrix(period, half):
    r = lax.broadcasted_iota(I32, (LANES, LANES), 0)
    c = lax.broadcasted_iota(I32, (LANES, LANES), 1)
    pos = c & (period - 1)
    src = jnp.where(pos < half, c + half, c - half)
    return jnp.where((pos < 2 * half) & (r == src), 1.0, 0.0).astype(BF16)


def _rope_tables(positions):
    t = positions.size
    tp = 512
    tab = jax.ShapeDtypeStruct((t, LANES), F32)
    spec = pl.BlockSpec((tp, LANES), lambda i: (i, 0))
    return pl.pallas_call(
        _rope_table_kernel,
        grid=(t // tp,),
        in_specs=[pl.BlockSpec((tp, 1), lambda i: (i, 0))],
        out_specs=[spec] * 4,
        out_shape=[tab] * 4,
        compiler_params=_cparams(("parallel",)),
        name="rope_tables",
    )(positions.reshape(t, 1))


def _rope(t, c, s, swap):
    hi = t.astype(BF16)
    lo = (t - hi.astype(F32)).astype(BF16)
    return t * c + (_dot(hi, swap) + _dot(lo, swap)) * s


PROJ_TN = 1280
PROJ_ROWS = 256


def _proj_kernel(x_ref, g_ref, wh_ref, wt_ref, ws_ref, o_ref, os_ref, h_ref):
    j = pl.program_id(1)

    @pl.when(j == 0)
    def _():
        for r in range(x_ref.shape[0] // PROJ_ROWS):
            rows = slice(r * PROJ_ROWS, (r + 1) * PROJ_ROWS)
            h = _rms(x_ref[rows, :], g_ref[...]).astype(BF16)
            h_ref[rows, :] = h
            os_ref[rows, :] = _dot(h, ws_ref[...])
            o_ref[rows, :] = _dot(h, wh_ref[...]).astype(o_ref.dtype)

    @pl.when((j > 0) & (j < P_HEAD // PROJ_TN))
    def _():
        o_ref[...] = _dot(h_ref[...], wh_ref[...]).astype(o_ref.dtype)

    @pl.when(j >= P_HEAD // PROJ_TN)
    def _():
        o_ref[...] = _dot(h_ref[...], wt_ref[...]).astype(o_ref.dtype)


def _proj(x, g, wh, wt, ws, l):
    t = x.shape[0]
    tm, tn = 1024, PROJ_TN
    n_head = P_HEAD // tn
    return pl.pallas_call(
        _proj_kernel,
        grid=(t // tm, P_WIDTH // tn),
        in_specs=[
            pl.BlockSpec((tm, D_MODEL), lambda i, j: (i, 0)),
            _lp(l, (1, D_MODEL), lambda i, j: (0, 0)),
            _lp(l, (D_MODEL, tn), lambda i, j: (0, jnp.minimum(j, n_head - 1))),
            _lp(l, (D_MODEL, tn), lambda i, j: (0, jnp.maximum(j - n_head, 0))),
            _lp(l, (D_MODEL, PS_WIDTH), lambda i, j: (0, 0)),
        ],
        out_specs=[
            pl.BlockSpec((tm, tn), lambda i, j: (i, j)),
            pl.BlockSpec((tm, PS_WIDTH), lambda i, j: (i, 0)),
        ],
        out_shape=[
            jax.ShapeDtypeStruct((t, P_WIDTH), BF16),
            jax.ShapeDtypeStruct((t, PS_WIDTH), F32),
        ],
        scratch_shapes=[pltpu.VMEM((tm, D_MODEL), BF16)],
        compiler_params=_cparams(("parallel", "arbitrary")),
        name="proj",
    )(x, g, wh, wt, ws)


def _prep_kernel(q_ref, iql_ref, iqh_ref, kv_ref, ps_ref, ch_ref, sh_ref, ci_ref, si_ref,
                 qg_ref, kg_ref, qn_ref, iqr_ref, kn_ref, vt_ref, ika_ref, ikb_ref, iwt_ref):
    ch, sh = ch_ref[...], sh_ref[...]
    ci, si = ci_ref[...], si_ref[...]
    swap_h = _swap_matrix(HEAD_DIM, HEAD_ROT_HALF)
    swap_i = _swap_matrix(IDX_DIM, IDX_ROT_HALF)
    scale = (HEAD_DIM ** -0.5) * LOG2E
    half_groups = N_HEADS // 2
    for h in range(N_HEADS):
        sl = slice(h * HEAD_DIM, (h + 1) * HEAD_DIM)
        qh = _rms(q_ref[:, sl].astype(F32), qg_ref[...])
        qn_ref[:, sl] = (_rope(qh, ch, sh, swap_h) * scale).astype(BF16)
        src = iql_ref if h < half_groups else iqh_ref
        hs = h % half_groups
        iqh = src[:, hs * LANES:(hs + 1) * LANES].astype(F32)
        iqr_ref[:, sl] = _rope(iqh, ci, si, swap_i).astype(BF16)
    for g in range(N_KV_HEADS):
        sl = slice(g * HEAD_DIM, (g + 1) * HEAD_DIM)
        kh = _rms(kv_ref[:, sl].astype(F32), kg_ref[...])
        kn_ref[:, sl] = _rope(kh, ch, sh, swap_h).astype(BF16)
        vh = kv_ref[:, KV_DIM + g * HEAD_DIM:KV_DIM + (g + 1) * HEAD_DIM].astype(F32)
        vt_ref[0, sl, :] = vh.T.astype(BF16)
    ps = ps_ref[...]
    lane = lax.broadcasted_iota(I32, (1, LANES), 1)
    ikr = jnp.where(lane < IDX_DIM, _rope(ps, ci, si, swap_i), 0.0)
    ika_ref[...] = ikr.astype(BF16)
    ikb_ref[...] = pltpu.roll(ikr, IDX_DIM, 1).astype(BF16)
    idx_scale = (IDX_DIM ** -0.5) * (IDX_HEADS ** -0.5)
    iws = jnp.where(lane < IDX_HEADS, pltpu.roll(ps, LANES - IDX_DIM, 1) * idx_scale, 0.0)
    iwt_ref[...] = iws.T


def _prep(p, ps, tabs, qg, kg, l):
    t = p.shape[0]
    tp = KC
    half_iq = IDX_HEADS * IDX_DIM // 2
    tab_spec = pl.BlockSpec((tp, LANES), lambda i: (i, 0))
    vec_spec = _lp(l, (1, HEAD_DIM), lambda i: (0, 0))

    def out(width, dtype):
        return (pl.BlockSpec((tp, width), lambda i: (i, 0)), jax.ShapeDtypeStruct((t, width), dtype))

    outs = [out(ATTN_DIM, BF16), out(ATTN_DIM, BF16), out(KV_DIM, BF16),
            (pl.BlockSpec((1, KV_DIM, tp), lambda i: (i, 0, 0)),
             jax.ShapeDtypeStruct((t // tp, KV_DIM, tp), BF16)),
            out(LANES, BF16), out(LANES, BF16),
            (pl.BlockSpec((LANES, tp), lambda i: (0, i)), jax.ShapeDtypeStruct((LANES, t), F32))]
    return pl.pallas_call(
        _prep_kernel,
        grid=(t // tp,),
        in_specs=[
            pl.BlockSpec((tp, ATTN_DIM), lambda i: (i, P_Q // ATTN_DIM)),
            pl.BlockSpec((tp, half_iq), lambda i: (i, P_IQ // half_iq)),
            pl.BlockSpec((tp, half_iq), lambda i: (i, P_IQ // half_iq + 1)),
            pl.BlockSpec((tp, 2 * KV_DIM), lambda i: (i, P_K // (2 * KV_DIM))),
            tab_spec,
        ] + [tab_spec] * 4 + [vec_spec, vec_spec],
        out_specs=[o[0] for o in outs],
        out_shape=[o[1] for o in outs],
        compiler_params=_cparams(("parallel",)),
        name="prep",
    )(p, p, p, p, ps, *tabs, qg, kg)


def _sortable(x):
    bits = lax.bitcast_convert_type(x, I32)
    return bits ^ ((bits >> 31) & 0x7FFFFFFF)


N_PAIRS = N_HEADS // 2
N_PART = 4


def _dsa_kernel(qn_ref, iq_ref, iwt_ref, kn_ref, vt_ref, ika_ref, ikb_ref, o_ref,
                key_ref, thr_ref, nge_ref, bias_ref, m_ref, l_ref, acc_ref, sa_ref, sb_ref, p_ref):
    j = pl.program_id(1)
    nck = j // (KC // QB) + 1
    seq = kn_ref.shape[0]

    iwt = iwt_ref[...]
    q_pos = j * QB + lax.broadcasted_iota(I32, (SUB, QB), 1)
    n_grp = IDX_HEADS // 2

    halves = KC // SUB
    nhalf = j // (SUB // QB) + 1

    def score_half(c, hf):
        start = pl.multiple_of(c * KC + hf * SUB, SUB)
        ka = ika_ref[pl.ds(start, SUB), :]
        kb = ikb_ref[pl.ds(start, SUB), :]
        acc = jnp.zeros((SUB, QB), F32)
        for a in range(n_grp // 2):
            b = a + n_grp // 2
            iq2 = jnp.concatenate([iq_ref[:, a * LANES:(a + 1) * LANES],
                                   iq_ref[:, b * LANES:(b + 1) * LANES]], axis=0)
            sa = _dot_nt(ka, iq2)
            sb = _dot_nt(kb, iq2)
            acc = acc + jnp.maximum(sa[:, :QB], 0.0) * iwt[2 * a:2 * a + 1, :]
            acc = acc + jnp.maximum(sb[:, :QB], 0.0) * iwt[2 * a + 1:2 * a + 2, :]
            acc = acc + jnp.maximum(sa[:, QB:], 0.0) * iwt[2 * b:2 * b + 1, :]
            acc = acc + jnp.maximum(sb[:, QB:], 0.0) * iwt[2 * b + 1:2 * b + 2, :]
        k_pos = start + lax.broadcasted_iota(I32, (SUB, QB), 0)
        acc = jnp.where(k_pos <= q_pos, acc, -jnp.inf)
        key_ref[c, hf * SUB:(hf + 1) * SUB, :] = _sortable(acc)

    def score_chunk(c, carry):
        for hf in range(halves):
            score_half(c, hf)
        return carry

    lax.fori_loop(0, nhalf // halves, score_chunk, 0)

    @pl.when(nhalf % halves == 1)
    def _():
        score_half(nck - 1, 0)
        key_ref[nck - 1, SUB:, :] = jnp.full((KC - SUB, QB), INT_MIN, I32)

    n_loaded = jnp.broadcast_to(nhalf * SUB, (SUBLANES, QB)).astype(I32)

    def search(nh):
        def search_pass(p, carry):
            tu, n_at = carry
            cand_u = tu | lax.shift_left(jnp.int32(1), 31 - p)
            cand = cand_u ^ INT_MIN
            parts = [jnp.zeros((SUBLANES, QB), I32)] * N_PART
            for r in range(nh * SUB // SUBLANES):
                row = (r * SUBLANES) % KC
                k = key_ref[(r * SUBLANES) // KC, row:row + SUBLANES, :]
                parts[r % N_PART] = parts[r % N_PART] + jnp.where(k >= cand, 1, 0)
            tot = (parts[0] + parts[1]) + (parts[2] + parts[3])
            n = jnp.broadcast_to(jnp.sum(tot, axis=0, keepdims=True), (SUBLANES, QB))
            ok = n >= TOPK
            return jnp.where(ok, cand_u, tu), jnp.where(ok, n, n_at)

        tu, n_ge = lax.fori_loop(0, 32, search_pass, (jnp.zeros((SUBLANES, QB), I32), n_loaded))
        thr_ref[...] = tu ^ INT_MIN
        nge_ref[...] = n_ge

    for nh in range(1, key_ref.shape[0] * halves + 1):
        pl.when(nhalf == nh)(functools.partial(search, nh))
    thr = thr_ref[...]
    n_ge = nge_ref[...]

    row_iota = lax.broadcasted_iota(I32, (SUBLANES, QB), 0)
    tied = (n_ge > TOPK) & (thr > KEY_NEG_INF)
    any_tied = jnp.max(jnp.where(tied, 1, 0)) > 0

    def count32(pred):
        def chunk(c, parts):
            parts = list(parts)
            for r in range(KC // SUBLANES):
                k = key_ref[c, r * SUBLANES:(r + 1) * SUBLANES, :]
                hit = pred(k, c * KC + r * SUBLANES)
                parts[r % N_PART] = parts[r % N_PART] + jnp.where(hit, 1, 0)
            return tuple(parts)

        zero = jnp.zeros((SUBLANES, QB), I32)
        parts = lax.fori_loop(0, nck, chunk, (zero,) * N_PART)
        tot = (parts[0] + parts[1]) + (parts[2] + parts[3])
        return jnp.broadcast_to(jnp.sum(tot, axis=0, keepdims=True), (SUBLANES, QB))

    thr_fin = jnp.maximum(thr, KEY_NEG_INF + 1)

    @pl.when(jnp.logical_not(any_tied))
    def _():
        def bias_chunk(c, carry):
            for r in range(KC // SUBLANES):
                sl = slice(r * SUBLANES, (r + 1) * SUBLANES)
                k = key_ref[c, sl, :]
                sel = (k >= thr_fin) & (k < KEY_POS_INF)
                bias_ref[c, sl, :] = jnp.where(sel, 0.0, NEG_BIG)
            return carry

        lax.fori_loop(0, nck, bias_chunk, 0)

    @pl.when(any_tied)
    def _():
        need = TOPK - count32(lambda k, _: k > thr)

        def pos_pass(p, x):
            cand = x | lax.shift_left(jnp.int32(1), (seq.bit_length() - 2) - p)
            n = count32(lambda k, r0: (k == thr) & (row_iota + r0 < cand))
            return jnp.where(n < need, cand, x)

        xlim = lax.fori_loop(0, seq.bit_length() - 1, pos_pass, jnp.zeros((SUBLANES, QB), I32))

        def bias_chunk(c, carry):
            for r in range(KC // SUBLANES):
                sl = slice(r * SUBLANES, (r + 1) * SUBLANES)
                k = key_ref[c, sl, :]
                pos = row_iota + (c * KC + r * SUBLANES)
                sel = (k > thr) | ((k == thr) & (pos <= xlim))
                sel = sel & (k >= thr_fin) & (k < KEY_POS_INF)
                bias_ref[c, sl, :] = jnp.where(sel, 0.0, NEG_BIG)
            return carry

        lax.fori_loop(0, nck, bias_chunk, 0)

    m_ref[...] = jnp.full(m_ref.shape, NEG_BIG, F32)
    l_ref[...] = jnp.zeros(l_ref.shape, F32)
    acc_ref[...] = jnp.zeros(acc_ref.shape, F32)
    group = N_HEADS // N_KV_HEADS

    def logits(c, s_ref):
        col_max = []
        for pr in range(N_PAIRS):
            g = (2 * pr) // group
            q2 = jnp.concatenate([qn_ref[:, (2 * pr) * HEAD_DIM:(2 * pr + 1) * HEAD_DIM],
                                  qn_ref[:, (2 * pr + 1) * HEAD_DIM:(2 * pr + 2) * HEAD_DIM]],
                                 axis=0)
            mx = jnp.full((SUBLANES, 2 * QB), NEG_BIG, F32)
            for hf in range(KC // SUB):
                start = pl.multiple_of(c * KC + hf * SUB, SUB)
                kc = kn_ref[pl.ds(start, SUB), g * HEAD_DIM:(g + 1) * HEAD_DIM]
                bias = bias_ref[c, hf * SUB:(hf + 1) * SUB, :]
                s = _dot_nt(kc, q2) + jnp.concatenate([bias, bias], axis=1)
                s_ref[pr, hf * SUB:(hf + 1) * SUB, :] = s
                for r in range(SUB // SUBLANES):
                    mx = jnp.maximum(mx, s[r * SUBLANES:(r + 1) * SUBLANES, :])
            col_max.append(jnp.broadcast_to(jnp.max(mx, axis=0, keepdims=True), (SUBLANES, 2 * QB)))
        return tuple(col_max)

    def finish(c, s_ref, col_max):
        alphas = []
        for pr in range(N_PAIRS):
            m_prev = m_ref[pr]
            m_new = jnp.maximum(m_prev, col_max[pr])
            alpha = jnp.exp2(m_prev - m_new)
            lsum = jnp.zeros((SUBLANES, 2 * QB), F32)
            for t in range(KC // SLAB):
                rows = slice(t * SLAB, (t + 1) * SLAB)
                p = jnp.exp2(s_ref[pr, rows, :] - m_new[0:1, :])
                for r in range(SLAB // SUBLANES):
                    lsum = lsum + p[r * SUBLANES:(r + 1) * SUBLANES, :]
                p_ref[pr, rows, :] = p.astype(BF16)
            m_ref[pr] = m_new
            l_ref[pr] = alpha * l_ref[pr] + jnp.sum(lsum, axis=0, keepdims=True)
            alphas.append(alpha)
        for pr in range(N_PAIRS):
            g = (2 * pr) // group
            vt = vt_ref[c, g * HEAD_DIM:(g + 1) * HEAD_DIM, :]
            acc_ref[pr] = acc_ref[pr] * alphas[pr][0:1, :] + _dot(vt, p_ref[pr])

    def attn_step(i, col_max):
        c = 2 * i
        mid = logits(c + 1, sb_ref)
        finish(c, sa_ref, col_max)
        nxt = logits(c + 2, sa_ref)
        finish(c + 1, sb_ref, mid)
        return nxt

    n_steps = (nck - 1) // 2
    tail = 2 * n_steps
    col_max = lax.fori_loop(0, n_steps, attn_step, logits(0, sa_ref))

    @pl.when(tail == nck - 1)
    def _():
        finish(tail, sa_ref, col_max)

    @pl.when(tail != nck - 1)
    def _():
        mid = logits(tail + 1, sb_ref)
        finish(tail, sa_ref, col_max)
        finish(tail + 1, sb_ref, mid)

    for pr in range(N_PAIRS):
        o_t = acc_ref[pr] / l_ref[pr][0:1, :]
        for i in range(2):
            h = 2 * pr + i
            o_ref[:, h * HEAD_DIM:(h + 1) * HEAD_DIM] = o_t[:, i * QB:(i + 1) * QB].T.astype(BF16)


def _dsa(qn, iqr, iwt, kn, vt, ika, ikb, batch, seq):
    t = qn.shape[0]
    nb = seq // QB
    nchunk = seq // KC
    qmap = lambda b, j: (b * nb + j, 0)
    bmap = lambda b, j: (b, 0)
    return pl.pallas_call(
        _dsa_kernel,
        grid=(batch, nb),
        in_specs=[
            pl.BlockSpec((QB, ATTN_DIM), qmap),
            pl.BlockSpec((QB, ATTN_DIM), qmap),
            pl.BlockSpec((IDX_HEADS, QB), lambda b, j: (0, b * nb + j)),
            pl.BlockSpec((seq, KV_DIM), bmap),
            pl.BlockSpec((nchunk, KV_DIM, KC), lambda b, j: (b, 0, 0)),
            pl.BlockSpec((seq, LANES), bmap),
            pl.BlockSpec((seq, LANES), bmap),
        ],
        out_specs=pl.BlockSpec((QB, ATTN_DIM), qmap),
        out_shape=jax.ShapeDtypeStruct((t, ATTN_DIM), BF16),
        scratch_shapes=[
            pltpu.VMEM((nchunk, KC, QB), I32),
            pltpu.VMEM((SUBLANES, QB), I32),
            pltpu.VMEM((SUBLANES, QB), I32),
            pltpu.VMEM((nchunk, KC, QB), F32),
            pltpu.VMEM((N_PAIRS, SUBLANES, 2 * QB), F32),
            pltpu.VMEM((N_PAIRS, SUBLANES, 2 * QB), F32),
            pltpu.VMEM((N_PAIRS, HEAD_DIM, 2 * QB), F32),
            pltpu.VMEM((N_PAIRS, KC, 2 * QB), F32),
            pltpu.VMEM((N_PAIRS, KC, 2 * QB), F32),
            pltpu.VMEM((N_PAIRS, KC, 2 * QB), BF16),
        ],
        compiler_params=_cparams(("parallel", "arbitrary")),
        name="dsa",
    )(qn, iqr, iwt, kn, vt, ika, ikb)


def _mem_kv_kernel(mem_ref, g_ref, w_ref, kg_ref, mk_ref, mv_ref):
    m = _rms(mem_ref[...], g_ref[...]).astype(BF16)
    kv = _dot(m, w_ref[...])
    for h in range(MEM_HEADS):
        sl = slice(h * HEAD_DIM, (h + 1) * HEAD_DIM)
        mk_ref[:, sl] = _rms(kv[:, sl], kg_ref[...]).astype(BF16)
    mv_ref[...] = kv[:, MEM_DIM:].astype(BF16)


def _mem_kv(mem2d, g, w, kg, batch, l):
    n = mem2d.shape[0]
    m = n // batch
    out = jax.ShapeDtypeStruct((n, MEM_DIM), BF16)
    ospec = pl.BlockSpec((m, MEM_DIM), lambda b: (b, 0))
    return pl.pallas_call(
        _mem_kv_kernel,
        grid=(batch,),
        in_specs=[
            pl.BlockSpec((m, D_MODEL), lambda b: (b, 0)),
            _lp(l, (1, D_MODEL), lambda b: (0, 0)),
            _lp(l, (D_MODEL, 2 * MEM_DIM), lambda b: (0, 0)),
            _lp(l, (1, HEAD_DIM), lambda b: (0, 0)),
        ],
        out_specs=[ospec, ospec],
        out_shape=[out, out],
        compiler_params=_cparams(("parallel",)),
        name="mem_kv",
    )(mem2d, g, w, kg)


def _mem_attn_kernel(q_ref, mk_ref, mv_ref, qg_ref, o_ref):
    scale = HEAD_DIM ** -0.5
    for h in range(MEM_HEADS):
        sl = slice(h * HEAD_DIM, (h + 1) * HEAD_DIM)
        qh = (_rms(q_ref[:, sl].astype(F32), qg_ref[...]) * scale).astype(BF16)
        s = _dot_nt(qh, mk_ref[:, sl])
        p = jnp.exp(s - jnp.max(s, axis=1, keepdims=True))
        l = jnp.sum(p, axis=1, keepdims=True)
        o = _dot((p / l).astype(BF16), mv_ref[:, sl])
        o_ref[:, sl] = o.astype(BF16)


def _mem_attn(p, mk, mv, qg, batch, seq, l):
    t = p.shape[0]
    tq = 512
    nq = seq // tq
    m = mk.shape[0] // batch
    return pl.pallas_call(
        _mem_attn_kernel,
        grid=(batch, nq),
        in_specs=[
            pl.BlockSpec((tq, MEM_DIM), lambda b, i: (b * nq + i, P_MQ // MEM_DIM)),
            pl.BlockSpec((m, MEM_DIM), lambda b, i: (b, 0)),
            pl.BlockSpec((m, MEM_DIM), lambda b, i: (b, 0)),
            _lp(l, (1, HEAD_DIM), lambda b, i: (0, 0)),
        ],
        out_specs=pl.BlockSpec((tq, MEM_DIM), lambda b, i: (b * nq + i, 0)),
        out_shape=jax.ShapeDtypeStruct((t, MEM_DIM), BF16),
        compiler_params=_cparams(("parallel", "parallel")),
        name="mem_attn",
    )(p, mk, mv, qg)


CONV_HALO = 32
CONV_ROWS = 64


def _conv_kernel(cur_a_ref, cur_g_ref, prev_a_ref, prev_g_ref, inb_ref, cw_ref, cb_ref, lg_ref,
                 lb_ref, o_ref, u_ref, sh_ref):
    ts = cur_a_ref.shape[0]
    n = CONV_HALO + ts

    def glu(a, g):
        a = a.astype(F32) + inb_ref[:, :CONV_CH]
        g = g.astype(F32) + inb_ref[:, CONV_CH:]
        return a * jax.nn.sigmoid(g)

    u_prev = glu(prev_a_ref[ts - CONV_HALO:, :], prev_g_ref[ts - CONV_HALO:, :])
    u_ref[:CONV_HALO, :] = jnp.where(pl.program_id(1) == 0, 0.0, u_prev)
    u_ref[CONV_HALO:n, :] = glu(cur_a_ref[...], cur_g_ref[...])
    u_ref[n:, :] = jnp.zeros((SUBLANES, CONV_CH), F32)
    for r in range(1, SUBLANES):
        sh_ref[r - 1] = u_ref[r:r + n, :]

    first_tap = CONV_HALO - (CONV_WIDTH - 1)
    for t in range(ts // CONV_ROWS):
        base = t * CONV_ROWS
        y = jnp.broadcast_to(cb_ref[...], (CONV_ROWS, CONV_CH))
        for w in range(CONV_WIDTH):
            r = (first_tap + w) % SUBLANES
            a = base + first_tap + w - r
            rows = u_ref[a:a + CONV_ROWS, :] if r == 0 else sh_ref[r - 1, a:a + CONV_ROWS, :]
            y = y + rows * cw_ref[w:w + 1, :]
        mu = jnp.mean(y, axis=-1, keepdims=True)
        d = y - mu
        var = jnp.mean(d * d, axis=-1, keepdims=True)
        z = d * lax.rsqrt(var + EPS) * lg_ref[...] + lb_ref[...]
        o_ref[base:base + CONV_ROWS, :] = (z * jax.nn.sigmoid(z)).astype(BF16)


def _conv(p, inb, cw, cb, lg, lb, batch, seq, l):
    t = p.shape[0]
    ts = 512
    ns = seq // ts
    a_blk = P_GLU // CONV_CH
    vec = lambda w: _lp(l, (1, w), lambda b, i: (0, 0))
    cur = lambda k: pl.BlockSpec((ts, CONV_CH), lambda b, i: (b * ns + i, a_blk + k))
    prev = lambda k: pl.BlockSpec((ts, CONV_CH), lambda b, i: (b * ns + jnp.maximum(i - 1, 0), a_blk + k))
    return pl.pallas_call(
        _conv_kernel,
        grid=(batch, ns),
        in_specs=[
            cur(0), cur(1), prev(0), prev(1),
            vec(2 * CONV_CH),
            _lp(l, (CONV_WIDTH, CONV_CH), lambda b, i: (0, 0)),
            vec(CONV_CH), vec(CONV_CH), vec(CONV_CH),
        ],
        out_specs=pl.BlockSpec((ts, CONV_CH), lambda b, i: (b * ns + i, 0)),
        out_shape=jax.ShapeDtypeStruct((t, CONV_CH), BF16),
        scratch_shapes=[pltpu.VMEM((CONV_HALO + ts + SUBLANES, CONV_CH), F32),
                        pltpu.VMEM((SUBLANES - 1, CONV_HALO + ts, CONV_CH), F32)],
        compiler_params=_cparams(("parallel", "parallel")),
        name="conv",
    )(p, p, p, p, inb, cw, cb, lg, lb)


MERGE_TN = 512


def _merge_kernel(a_ref, m_ref, c_ref, g0_ref, g1_ref, g2_ref, gb0_ref, gb1_ref, gb2_ref,
                  wa_ref, wm_ref, wc_ref, cob_ref, o_ref):
    a, m, c = a_ref[...], m_ref[...], c_ref[...]
    for n in range(D_MODEL // MERGE_TN):
        sl = slice(n * MERGE_TN, (n + 1) * MERGE_TN)

        def gate(g_ref, gb_ref):
            return jax.nn.sigmoid(g_ref[:, sl].astype(F32) + gb_ref[:, sl])

        y = gate(g0_ref, gb0_ref) * _dot(a, wa_ref[:, sl])
        y = y + gate(g1_ref, gb1_ref) * _dot(m, wm_ref[:, sl])
        y = y + gate(g2_ref, gb2_ref) * (_dot(c, wc_ref[:, sl]) + cob_ref[:, sl])
        o_ref[:, sl] = y.astype(BF16)


def _merge(attn, memo, convo, p, gate_b, wa, wm, wc, cob, l):
    t = attn.shape[0]
    tm = 512
    gblk = P_GATES // D_MODEL
    row = lambda w: pl.BlockSpec((tm, w), lambda i: (i, 0))
    full = lambda r, c: _lp(l, (r, c), lambda i: (0, 0))
    gspec = lambda k: pl.BlockSpec((tm, D_MODEL), lambda i: (i, gblk + k))
    gbspec = lambda k: _lp(l, (1, D_MODEL), lambda i: (0, k))
    return pl.pallas_call(
        _merge_kernel,
        grid=(t // tm,),
        in_specs=[row(ATTN_DIM), row(MEM_DIM), row(CONV_CH),
                  gspec(0), gspec(1), gspec(2), gbspec(0), gbspec(1), gbspec(2),
                  full(ATTN_DIM, D_MODEL), full(MEM_DIM, D_MODEL), full(CONV_CH, D_MODEL),
                  full(1, D_MODEL)],
        out_specs=row(D_MODEL),
        out_shape=jax.ShapeDtypeStruct((t, D_MODEL), BF16),
        compiler_params=_cparams(("parallel",)),
        name="merge",
    )(attn, memo, convo, p, p, p, gate_b, gate_b, gate_b, wa, wm, wc, cob)


def _outproj_kernel(x_ref, mg_ref, w_ref, g_ref, x1_ref, h2_ref):
    x1 = x_ref[...] + _dot(mg_ref[...], w_ref[...])
    x1_ref[...] = x1
    h2_ref[...] = _rms(x1, g_ref[...]).astype(BF16)


def _outproj(x, merged, w, g, l):
    t = x.shape[0]
    tm = 512
    row = pl.BlockSpec((tm, D_MODEL), lambda i: (i, 0))
    return pl.pallas_call(
        _outproj_kernel,
        grid=(t // tm,),
        in_specs=[row, row,
                  _lp(l, (D_MODEL, D_MODEL), lambda i: (0, 0)),
                  _lp(l, (1, D_MODEL), lambda i: (0, 0))],
        out_specs=[row, row],
        out_shape=[jax.ShapeDtypeStruct((t, D_MODEL), F32), jax.ShapeDtypeStruct((t, D_MODEL), BF16)],
        compiler_params=_cparams(("parallel",)),
        name="outproj",
    )(x, merged, w, g)


def _mlp_kernel(h_ref, x_ref, wu_ref, wd_ref, o_ref):
    @pl.when(pl.program_id(1) == 0)
    def _():
        o_ref[...] = x_ref[...]

    u = jnp.maximum(_dot(h_ref[...], wu_ref[...]), 0.0)
    o_ref[...] += _dot((u * u).astype(BF16), wd_ref[...])


def _mlp(h2, x1, wu, wd, l):
    t = x1.shape[0]
    tm, tf = 512, 1024
    return pl.pallas_call(
        _mlp_kernel,
        grid=(t // tm, FFN_DIM // tf),
        in_specs=[
            pl.BlockSpec((tm, D_MODEL), lambda i, f: (i, 0)),
            pl.BlockSpec((tm, D_MODEL), lambda i, f: (i, 0)),
            _lp(l, (D_MODEL, tf), lambda i, f: (0, f)),
            _lp(l, (tf, D_MODEL), lambda i, f: (f, 0)),
        ],
        out_specs=pl.BlockSpec((tm, D_MODEL), lambda i, f: (i, 0)),
        out_shape=jax.ShapeDtypeStruct((t, D_MODEL), F32),
        compiler_params=_cparams(("parallel", "arbitrary")),
        name="mlp",
    )(h2, x1, wu, wd)


def _split_w_in(w):
    narrow = IDX_DIM + IDX_HEADS
    wb = w.astype(BF16)
    head = wb[:, :, :P_HEAD]
    tail = wb[:, :, P_HEAD + narrow:]
    small = jnp.pad(wb[:, :, P_HEAD:P_HEAD + narrow], ((0, 0), (0, 0), (0, PS_WIDTH - narrow)))
    return head, tail, small


def kernel(x, mem, positions, norm1_g, w_in, q_norm_g, k_norm_g, mem_norm_g, w_mem_kv, mq_norm_g,
           mk_norm_g, conv_in_b, conv_w, conv_b, conv_ln_g, conv_ln_b, gate_b, w_attn_o, w_mem_o,
           w_conv_o, conv_o_b, w_out, norm2_g, w_up, w_down):
    batch, seq, _ = x.shape
    depth = w_in.shape[0]
    t = batch * seq
    xf = x.reshape(t, D_MODEL)
    mem2d = mem.reshape(batch * mem.shape[1], D_MODEL)
    tabs = _rope_tables(positions)

    vec = lambda v: v.reshape(depth, 1, -1)
    norm1_g, q_norm_g, k_norm_g, mem_norm_g, mq_norm_g, mk_norm_g = map(
        vec, (norm1_g, q_norm_g, k_norm_g, mem_norm_g, mq_norm_g, mk_norm_g))
    conv_in_b, conv_b, conv_ln_g, conv_ln_b, gate_b, conv_o_b, norm2_g = map(
        vec, (conv_in_b, conv_b, conv_ln_g, conv_ln_b, gate_b, conv_o_b, norm2_g))
    w_head, w_tail, w_small = _split_w_in(w_in)
    w_mem_kv, w_attn_o, w_mem_o, w_conv_o, w_out, w_up, w_down = (
        w.astype(BF16) for w in (w_mem_kv, w_attn_o, w_mem_o, w_conv_o, w_out, w_up, w_down))

    for l in range(depth):
        p, ps = _proj(xf, norm1_g, w_head, w_tail, w_small, l)
        qn, iqr, kn, vt, ika, ikb, iwt = _prep(p, ps, tabs, q_norm_g, k_norm_g, l)
        attn = _dsa(qn, iqr, iwt, kn, vt, ika, ikb, batch, seq)
        mk, mv = _mem_kv(mem2d, mem_norm_g, w_mem_kv, mk_norm_g, batch, l)
        memo = _mem_attn(p, mk, mv, mq_norm_g, batch, seq, l)
        convo = _conv(p, conv_in_b, conv_w, conv_b, conv_ln_g, conv_ln_b, batch, seq, l)
        merged = _merge(attn, memo, convo, p, gate_b, w_attn_o, w_mem_o, w_conv_o, conv_o_b, l)
        x1, h2 = _outproj(xf, merged, w_out, norm2_g, l)
        xf = _mlp(h2, x1, w_up, w_down, l)
    return xf.reshape(batch, seq, D_MODEL)
```

```python
import functools
import math

import jax
import jax.numpy as jnp
from jax import lax
from jax.experimental import pallas as pl
from jax.experimental.pallas import tpu as pltpu

F32 = jnp.float32
BF16 = jnp.bfloat16
I32 = jnp.int32

D_MODEL = 2048
HEAD_DIM = 128
N_HEADS = 8
N_KV_HEADS = 2
IDX_HEADS = 16
IDX_DIM = 64
TOPK = 256
CONV_CH = 512
CONV_WIDTH = 31
MEM_HEADS = 4
FFN_DIM = 4 * D_MODEL
ROPE_THETA = 500000.0
N_BRANCH = 3
EPS = 1e-6
ATTN_DIM = N_HEADS * HEAD_DIM
KV_DIM = N_KV_HEADS * HEAD_DIM
MEM_DIM = MEM_HEADS * HEAD_DIM
HEAD_ROT_HALF = HEAD_DIM // 8
IDX_ROT_HALF = IDX_DIM // 8

LANES = 128
VMEM_LIMIT = 56 * 1024 * 1024

P_Q = 0
P_K = P_Q + ATTN_DIM
P_V = P_K + KV_DIM
P_IQ = P_V + KV_DIM
P_HEAD = P_IQ + IDX_HEADS * IDX_DIM
P_GLU = P_HEAD
P_MQ = P_GLU + 2 * CONV_CH
P_GATES = P_MQ + MEM_DIM
P_WIDTH = P_GATES + N_BRANCH * D_MODEL
P_TAIL = P_WIDTH - P_HEAD
PS_WIDTH = LANES

QB = 128
KC = 512
SUB = 256
SLAB = 32
SUBLANES = 8
LOG2E = 1.4426950408889634
NEG_BIG = -1e30
INT_MIN = -2 ** 31
KEY_NEG_INF = -2139095041
KEY_POS_INF = 2139095040


def _cparams(sem, vmem=VMEM_LIMIT):
    return pltpu.CompilerParams(dimension_semantics=sem, vmem_limit_bytes=vmem)


def _lp(l, shape, imap):
    return pl.BlockSpec((pl.Squeezed(),) + shape, lambda *g: (l,) + imap(*g))


def _dot(a, b):
    return jnp.dot(a, b, preferred_element_type=F32)


def _dot_nt(a, b):
    return lax.dot_general(a, b, (((1,), (1,)), ((), ())), preferred_element_type=F32)


def _rms(xf, g):
    return xf * lax.rsqrt(jnp.mean(xf * xf, axis=-1, keepdims=True) + EPS) * g


def _rope_table_kernel(pos_ref, ch_ref, sh_ref, ci_ref, si_ref):
    pos = pos_ref[...].astype(F32)
    lane = lax.broadcasted_iota(I32, (1, LANES), 1)
    hh, ih = HEAD_ROT_HALF, IDX_ROT_HALF
    is_head = lane < hh
    fi = jnp.where(is_head, lane, lane - hh).astype(F32)
    inv = jnp.exp(fi * jnp.where(is_head, -math.log(ROPE_THETA) / hh, -math.log(ROPE_THETA) / ih))
    ang = pos * inv
    c = jnp.cos(ang)
    s = jnp.sin(ang)

    def place(x, shift):
        return pltpu.roll(x, shift % LANES, 1)

    second = (lane >= hh) & (lane < 2 * hh)
    ch_ref[...] = jnp.where(is_head, c, jnp.where(second, place(c, hh), 1.0))
    sh_ref[...] = jnp.where(is_head, -s, jnp.where(second, place(s, hh), 0.0))
    r = lane & (IDX_DIM - 1)
    upper = lane >= IDX_DIM

    def idx_table(x, sign_first, fill):
        first = jnp.where(upper, place(x, IDX_DIM - hh), place(x, -hh))
        second_half = jnp.where(upper, place(x, IDX_DIM - hh + ih), place(x, ih - hh))
        return jnp.where(r < ih, sign_first * first, jnp.where(r < 2 * ih, second_half, fill))

    ci_ref[...] = idx_table(c, 1.0, 1.0)
    si_ref[...] = idx_table(s, -1.0, 0.0)


def _swap_matrix(period, half):
    r = lax.broadcasted_iota(I32, (LANES, LANES), 0)
    c = lax.broadcasted_iota(I32, (LANES, LANES), 1)
    pos = c & (period - 1)
    src = jnp.where(pos < half, c + half, c - half)
    return jnp.where((pos < 2 * half) & (r == src), 1.0, 0.0).astype(BF16)


def _rope_tables(positions):
    t = positions.size
    tp = 512
    tab = jax.ShapeDtypeStruct((t, LANES), F32)
    spec = pl.BlockSpec((tp, LANES), lambda i: (i, 0))
    return pl.pallas_call(
        _rope_table_kernel,
        grid=(t // tp,),
        in_specs=[pl.BlockSpec((tp, 1), lambda i: (i, 0))],
        out_specs=[spec] * 4,
        out_shape=[tab] * 4,
        compiler_params=_cparams(("parallel",)),
        name="rope_tables",
    )(positions.reshape(t, 1))


def _rope(t, c, s, swap):
    hi = t.astype(BF16)
    lo = (t - hi.astype(F32)).astype(BF16)
    return t * c + (_dot(hi, swap) + _dot(lo, swap)) * s


PROJ_TN = 1280
PROJ_ROWS = 256


def _proj_kernel(x_ref, g_ref, wh_ref, wt_ref, ws_ref, o_ref, os_ref, h_ref):
    j = pl.program_id(1)

    @pl.when(j == 0)
    def _():
        for r in range(x_ref.shape[0] // PROJ_ROWS):
            rows = slice(r * PROJ_ROWS, (r + 1) * PROJ_ROWS)
            h = _rms(x_ref[rows, :], g_ref[...]).astype(BF16)
            h_ref[rows, :] = h
            os_ref[rows, :] = _dot(h, ws_ref[...])
            o_ref[rows, :] = _dot(h, wh_ref[...]).astype(o_ref.dtype)

    @pl.when((j > 0) & (j < P_HEAD // PROJ_TN))
    def _():
        o_ref[...] = _dot(h_ref[...], wh_ref[...]).astype(o_ref.dtype)

    @pl.when(j >= P_HEAD // PROJ_TN)
    def _():
        o_ref[...] = _dot(h_ref[...], wt_ref[...]).astype(o_ref.dtype)


def _proj(x, g, wh, wt, ws, l):
    t = x.shape[0]
    tm, tn = 1024, PROJ_TN
    n_head = P_HEAD // tn
    return pl.pallas_call(
        _proj_kernel,
        grid=(t // tm, P_WIDTH // tn),
        in_specs=[
            pl.BlockSpec((tm, D_MODEL), lambda i, j: (i, 0)),
            _lp(l, (1, D_MODEL), lambda i, j: (0, 0)),
            _lp(l, (D_MODEL, tn), lambda i, j: (0, jnp.minimum(j, n_head - 1))),
            _lp(l, (D_MODEL, tn), lambda i, j: (0, jnp.maximum(j - n_head, 0))),
            _lp(l, (D_MODEL, PS_WIDTH), lambda i, j: (0, 0)),
        ],
        out_specs=[
            pl.BlockSpec((tm, tn), lambda i, j: (i, j)),
            pl.BlockSpec((tm, PS_WIDTH), lambda i, j: (i, 0)),
        ],
        out_shape=[
            jax.ShapeDtypeStruct((t, P_WIDTH), BF16),
            jax.ShapeDtypeStruct((t, PS_WIDTH), F32),
        ],
        scratch_shapes=[pltpu.VMEM((tm, D_MODEL), BF16)],
        compiler_params=_cparams(("parallel", "arbitrary")),
        name="proj",
    )(x, g, wh, wt, ws)


def _prep_kernel(q_ref, iql_ref, iqh_ref, kv_ref, ps_ref, ch_ref, sh_ref, ci_ref, si_ref,
                 qg_ref, kg_ref, qn_ref, iqr_ref, kn_ref, vt_ref, ika_ref, ikb_ref, iwt_ref):
    ch, sh = ch_ref[...], sh_ref[...]
    ci, si = ci_ref[...], si_ref[...]
    swap_h = _swap_matrix(HEAD_DIM, HEAD_ROT_HALF)
    swap_i = _swap_matrix(IDX_DIM, IDX_ROT_HALF)
    scale = (HEAD_DIM ** -0.5) * LOG2E
    half_groups = N_HEADS // 2
    for h in range(N_HEADS):
        sl = slice(h * HEAD_DIM, (h + 1) * HEAD_DIM)
        qh = _rms(q_ref[:, sl].astype(F32), qg_ref[...])
        qn_ref[:, sl] = (_rope(qh, ch, sh, swap_h) * scale).astype(BF16)
        src = iql_ref if h < half_groups else iqh_ref
        hs = h % half_groups
        iqh = src[:, hs * LANES:(hs + 1) * LANES].astype(F32)
        iqr_ref[:, sl] = _rope(iqh, ci, si, swap_i).astype(BF16)
    for g in range(N_KV_HEADS):
        sl = slice(g * HEAD_DIM, (g + 1) * HEAD_DIM)
        kh = _rms(kv_ref[:, sl].astype(F32), kg_ref[...])
        kn_ref[:, sl] = _rope(kh, ch, sh, swap_h).astype(BF16)
        vh = kv_ref[:, KV_DIM + g * HEAD_DIM:KV_DIM + (g + 1) * HEAD_DIM].astype(F32)
        vt_ref[0, sl, :] = vh.T.astype(BF16)
    ps = ps_ref[...]
    lane = lax.broadcasted_iota(I32, (1, LANES), 1)
    ikr = jnp.where(lane < IDX_DIM, _rope(ps, ci, si, swap_i), 0.0)
    ika_ref[...] = ikr.astype(BF16)
    ikb_ref[...] = pltpu.roll(ikr, IDX_DIM, 1).astype(BF16)
    idx_scale = (IDX_DIM ** -0.5) * (IDX_HEADS ** -0.5)
    iws = jnp.where(lane < IDX_HEADS, pltpu.roll(ps, LANES - IDX_DIM, 1) * idx_scale, 0.0)
    iwt_ref[...] = iws.T


def _prep(p, ps, tabs, qg, kg, l):
    t = p.shape[0]
    tp = KC
    half_iq = IDX_HEADS * IDX_DIM // 2
    tab_spec = pl.BlockSpec((tp, LANES), lambda i: (i, 0))
    vec_spec = _lp(l, (1, HEAD_DIM), lambda i: (0, 0))

    def out(width, dtype):
        return (pl.BlockSpec((tp, width), lambda i: (i, 0)), jax.ShapeDtypeStruct((t, width), dtype))

    outs = [out(ATTN_DIM, BF16), out(ATTN_DIM, BF16), out(KV_DIM, BF16),
            (pl.BlockSpec((1, KV_DIM, tp), lambda i: (i, 0, 0)),
             jax.ShapeDtypeStruct((t // tp, KV_DIM, tp), BF16)),
            out(LANES, BF16), out(LANES, BF16),
            (pl.BlockSpec((LANES, tp), lambda i: (0, i)), jax.ShapeDtypeStruct((LANES, t), F32))]
    return pl.pallas_call(
        _prep_kernel,
        grid=(t // tp,),
        in_specs=[
            pl.BlockSpec((tp, ATTN_DIM), lambda i: (i, P_Q // ATTN_DIM)),
            pl.BlockSpec((tp, half_iq), lambda i: (i, P_IQ // half_iq)),
            pl.BlockSpec((tp, half_iq), lambda i: (i, P_IQ // half_iq + 1)),
            pl.BlockSpec((tp, 2 * KV_DIM), lambda i: (i, P_K // (2 * KV_DIM))),
            tab_spec,
        ] + [tab_spec] * 4 + [vec_spec, vec_spec],
        out_specs=[o[0] for o in outs],
        out_shape=[o[1] for o in outs],
        compiler_params=_cparams(("parallel",)),
        name="prep",
    )(p, p, p, p, ps, *tabs, qg, kg)


def _sortable(x):
    bits = lax.bitcast_convert_type(x, I32)
    return bits ^ ((bits >> 31) & 0x7FFFFFFF)


N_PAIRS = N_HEADS // 2
N_PART = 4


def _dsa_kernel(qn_ref, iq_ref, iwt_ref, kn_ref, vt_ref, ika_ref, ikb_ref, o_ref,
                key_ref, thr_ref, nge_ref, bias_ref, m_ref, l_ref, acc_ref, sa_ref, sb_ref, p_ref):
    j = pl.program_id(1)
    nck = j // (KC // QB) + 1
    seq = kn_ref.shape[0]

    iwt = iwt_ref[...]
    q_pos = j * QB + lax.broadcasted_iota(I32, (SUB, QB), 1)
    n_grp = IDX_HEADS // 2

    halves = KC // SUB
    nhalf = j // (SUB // QB) + 1

    def score_half(c, hf):
        start = pl.multiple_of(c * KC + hf * SUB, SUB)
        ka = ika_ref[pl.ds(start, SUB), :]
        kb = ikb_ref[pl.ds(start, SUB), :]
        acc = jnp.zeros((SUB, QB), F32)
        for a in range(n_grp // 2):
            b = a + n_grp // 2
            iq2 = jnp.concatenate([iq_ref[:, a * LANES:(a + 1) * LANES],
                                   iq_ref[:, b * LANES:(b + 1) * LANES]], axis=0)
            sa = _dot_nt(ka, iq2)
            sb = _dot_nt(kb, iq2)
            acc = acc + jnp.maximum(sa[:, :QB], 0.0) * iwt[2 * a:2 * a + 1, :]
            acc = acc + jnp.maximum(sb[:, :QB], 0.0) * iwt[2 * a + 1:2 * a + 2, :]
            acc = acc + jnp.maximum(sa[:, QB:], 0.0) * iwt[2 * b:2 * b + 1, :]
            acc = acc + jnp.maximum(sb[:, QB:], 0.0) * iwt[2 * b + 1:2 * b + 2, :]
        k_pos = start + lax.broadcasted_iota(I32, (SUB, QB), 0)
        acc = jnp.where(k_pos <= q_pos, acc, -jnp.inf)
        key_ref[c, hf * SUB:(hf + 1) * SUB, :] = _sortable(acc)

    def score_chunk(c, carry):
        for hf in range(halves):
            score_half(c, hf)
        return carry

    lax.fori_loop(0, nhalf // halves, score_chunk, 0)

    @pl.when(nhalf % halves == 1)
    def _():
        score_half(nck - 1, 0)
        key_ref[nck - 1, SUB:, :] = jnp.full((KC - SUB, QB), INT_MIN, I32)

    n_loaded = jnp.broadcast_to(nhalf * SUB, (SUBLANES, QB)).astype(I32)

    def search(nh):
        def search_pass(p, carry):
            tu, n_at = carry
            cand_u = tu | lax.shift_left(jnp.int32(1), 31 - p)
            cand = cand_u ^ INT_MIN
            parts = [jnp.zeros((SUBLANES, QB), I32)] * N_PART
            for r in range(nh * SUB // SUBLANES):
                row = (r * SUBLANES) % KC
                k = key_ref[(r * SUBLANES) // KC, row:row + SUBLANES, :]
                parts[r % N_PART] = parts[r % N_PART] + jnp.where(k >= cand, 1, 0)
            tot = (parts[0] + parts[1]) + (parts[2] + parts[3])
            n = jnp.broadcast_to(jnp.sum(tot, axis=0, keepdims=True), (SUBLANES, QB))
            ok = n >= TOPK
            return jnp.where(ok, cand_u, tu), jnp.where(ok, n, n_at)

        tu, n_ge = lax.fori_loop(0, 32, search_pass, (jnp.zeros((SUBLANES, QB), I32), n_loaded))
        thr_ref[...] = tu ^ INT_MIN
        nge_ref[...] = n_ge

    for nh in range(1, key_ref.shape[0] * halves + 1):
        pl.when(nhalf == nh)(functools.partial(search, nh))
    thr = thr_ref[...]
    n_ge = nge_ref[...]

    row_iota = lax.broadcasted_iota(I32, (SUBLANES, QB), 0)
    tied = (n_ge > TOPK) & (thr > KEY_NEG_INF)
    any_tied = jnp.max(jnp.where(tied, 1, 0)) > 0

    def count32(pred):
        def chunk(c, parts):
            parts = list(parts)
            for r in range(KC // SUBLANES):
                k = key_ref[c, r * SUBLANES:(r + 1) * SUBLANES, :]
                hit = pred(k, c * KC + r * SUBLANES)
                parts[r % N_PART] = parts[r % N_PART] + jnp.where(hit, 1, 0)
            return tuple(parts)

        zero = jnp.zeros((SUBLANES, QB), I32)
        parts = lax.fori_loop(0, nck, chunk, (zero,) * N_PART)
        tot = (parts[0] + parts[1]) + (parts[2] + parts[3])
        return jnp.broadcast_to(jnp.sum(tot, axis=0, keepdims=True), (SUBLANES, QB))

    thr_fin = jnp.maximum(thr, KEY_NEG_INF + 1)

    @pl.when(jnp.logical_not(any_tied))
    def _():
        def bias_chunk(c, carry):
            for r in range(KC // SUBLANES):
                sl = slice(r * SUBLANES, (r + 1) * SUBLANES)
                k = key_ref[c, sl, :]
                sel = (k >= thr_fin) & (k < KEY_POS_INF)
                bias_ref[c, sl, :] = jnp.where(sel, 0.0, NEG_BIG)
            return carry

        lax.fori_loop(0, nck, bias_chunk, 0)

    @pl.when(any_tied)
    def _():
        need = TOPK - count32(lambda k, _: k > thr)

        def pos_pass(p, x):
            cand = x | lax.shift_left(jnp.int32(1), (seq.bit_length() - 2) - p)
            n = count32(lambda k, r0: (k == thr) & (row_iota + r0 < cand))
            return jnp.where(n < need, cand, x)

        xlim = lax.fori_loop(0, seq.bit_length() - 1, pos_pass, jnp.zeros((SUBLANES, QB), I32))

        def bias_chunk(c, carry):
            for r in range(KC // SUBLANES):
                sl = slice(r * SUBLANES, (r + 1) * SUBLANES)
                k = key_ref[c, sl, :]
                pos = row_iota + (c * KC + r * SUBLANES)
                sel = (k > thr) | ((k == thr) & (pos <= xlim))
                sel = sel & (k >= thr_fin) & (k < KEY_POS_INF)
                bias_ref[c, sl, :] = jnp.where(sel, 0.0, NEG_BIG)
            return carry

        lax.fori_loop(0, nck, bias_chunk, 0)

    m_ref[...] = jnp.full(m_ref.shape, NEG_BIG, F32)
    l_ref[...] = jnp.zeros(l_ref.shape, F32)
    acc_ref[...] = jnp.zeros(acc_ref.shape, F32)
    group = N_HEADS // N_KV_HEADS

    def logits(c, s_ref):
        col_max = []
        for pr in range(N_PAIRS):
            g = (2 * pr) // group
            q2 = jnp.concatenate([qn_ref[:, (2 * pr) * HEAD_DIM:(2 * pr + 1) * HEAD_DIM],
                                  qn_ref[:, (2 * pr + 1) * HEAD_DIM:(2 * pr + 2) * HEAD_DIM]],
                                 axis=0)
            mx = jnp.full((SUBLANES, 2 * QB), NEG_BIG, F32)
            for hf in range(KC // SUB):
                start = pl.multiple_of(c * KC + hf * SUB, SUB)
                kc = kn_ref[pl.ds(start, SUB), g * HEAD_DIM:(g + 1) * HEAD_DIM]
                bias = bias_ref[c, hf * SUB:(hf + 1) * SUB, :]
                s = _dot_nt(kc, q2) + jnp.concatenate([bias, bias], axis=1)
                s_ref[pr, hf * SUB:(hf + 1) * SUB, :] = s
                for r in range(SUB // SUBLANES):
                    mx = jnp.maximum(mx, s[r * SUBLANES:(r + 1) * SUBLANES, :])
            col_max.append(jnp.broadcast_to(jnp.max(mx, axis=0, keepdims=True), (SUBLANES, 2 * QB)))
        return tuple(col_max)

    def finish(c, s_ref, col_max):
        alphas = []
        for pr in range(N_PAIRS):
            m_prev = m_ref[pr]
            m_new = jnp.maximum(m_prev, col_max[pr])
            alpha = jnp.exp2(m_prev - m_new)
            lsum = jnp.zeros((SUBLANES, 2 * QB), F32)
            for t in range(KC // SLAB):
                rows = slice(t * SLAB, (t + 1) * SLAB)
                p = jnp.exp2(s_ref[pr, rows, :] - m_new[0:1, :])
                for r in range(SLAB // SUBLANES):
                    lsum = lsum + p[r * SUBLANES:(r + 1) * SUBLANES, :]
                p_ref[pr, rows, :] = p.astype(BF16)
            m_ref[pr] = m_new
            l_ref[pr] = alpha * l_ref[pr] + jnp.sum(lsum, axis=0, keepdims=True)
            alphas.append(alpha)
        for pr in range(N_PAIRS):
            g = (2 * pr) // group
            vt = vt_ref[c, g * HEAD_DIM:(g + 1) * HEAD_DIM, :]
            acc_ref[pr] = acc_ref[pr] * alphas[pr][0:1, :] + _dot(vt, p_ref[pr])

    def attn_step(i, col_max):
        c = 2 * i
        mid = logits(c + 1, sb_ref)
        finish(c, sa_ref, col_max)
        nxt = logits(c + 2, sa_ref)
        finish(c + 1, sb_ref, mid)
        return nxt

    n_steps = (nck - 1) // 2
    tail = 2 * n_steps
    col_max = lax.fori_loop(0, n_steps, attn_step, logits(0, sa_ref))

    @pl.when(tail == nck - 1)
    def _():
        finish(tail, sa_ref, col_max)

    @pl.when(tail != nck - 1)
    def _():
        mid = logits(tail + 1, sb_ref)
        finish(tail, sa_ref, col_max)
        finish(tail + 1, sb_ref, mid)

    for pr in range(N_PAIRS):
        o_t = acc_ref[pr] / l_ref[pr][0:1, :]
        for i in range(2):
            h = 2 * pr + i
            o_ref[:, h * HEAD_DIM:(h + 1) * HEAD_DIM] = o_t[:, i * QB:(i + 1) * QB].T.astype(BF16)


def _dsa(qn, iqr, iwt, kn, vt, ika, ikb, batch, seq):
    t = qn.shape[0]
    nb = seq // QB
    nchunk = seq // KC
    qmap = lambda b, j: (b * nb + j, 0)
    bmap = lambda b, j: (b, 0)
    return pl.pallas_call(
        _dsa_kernel,
        grid=(batch, nb),
        in_specs=[
            pl.BlockSpec((QB, ATTN_DIM), qmap),
            pl.BlockSpec((QB, ATTN_DIM), qmap),
            pl.BlockSpec((IDX_HEADS, QB), lambda b, j: (0, b * nb + j)),
            pl.BlockSpec((seq, KV_DIM), bmap),
            pl.BlockSpec((nchunk, KV_DIM, KC), lambda b, j: (b, 0, 0)),
            pl.BlockSpec((seq, LANES), bmap),
            pl.BlockSpec((seq, LANES), bmap),
        ],
        out_specs=pl.BlockSpec((QB, ATTN_DIM), qmap),
        out_shape=jax.ShapeDtypeStruct((t, ATTN_DIM), BF16),
        scratch_shapes=[
            pltpu.VMEM((nchunk, KC, QB), I32),
            pltpu.VMEM((SUBLANES, QB), I32),
            pltpu.VMEM((SUBLANES, QB), I32),
            pltpu.VMEM((nchunk, KC, QB), F32),
            pltpu.VMEM((N_PAIRS, SUBLANES, 2 * QB), F32),
            pltpu.VMEM((N_PAIRS, SUBLANES, 2 * QB), F32),
            pltpu.VMEM((N_PAIRS, HEAD_DIM, 2 * QB), F32),
            pltpu.VMEM((N_PAIRS, KC, 2 * QB), F32),
            pltpu.VMEM((N_PAIRS, KC, 2 * QB), F32),
            pltpu.VMEM((N_PAIRS, KC, 2 * QB), BF16),
        ],
        compiler_params=_cparams(("parallel", "arbitrary")),
        name="dsa",
    )(qn, iqr, iwt, kn, vt, ika, ikb)


def _mem_kv_kernel(mem_ref, g_ref, w_ref, kg_ref, mk_ref, mv_ref):
    m = _rms(mem_ref[...], g_ref[...]).astype(BF16)
    kv = _dot(m, w_ref[...])
    for h in range(MEM_HEADS):
        sl = slice(h * HEAD_DIM, (h + 1) * HEAD_DIM)
        mk_ref[:, sl] = _rms(kv[:, sl], kg_ref[...]).astype(BF16)
    mv_ref[...] = kv[:, MEM_DIM:].astype(BF16)


def _mem_kv(mem2d, g, w, kg, batch, l):
    n = mem2d.shape[0]
    m = n // batch
    out = jax.ShapeDtypeStruct((n, MEM_DIM), BF16)
    ospec = pl.BlockSpec((m, MEM_DIM), lambda b: (b, 0))
    return pl.pallas_call(
        _mem_kv_kernel,
        grid=(batch,),
        in_specs=[
            pl.BlockSpec((m, D_MODEL), lambda b: (b, 0)),
            _lp(l, (1, D_MODEL), lambda b: (0, 0)),
            _lp(l, (D_MODEL, 2 * MEM_DIM), lambda b: (0, 0)),
            _lp(l, (1, HEAD_DIM), lambda b: (0, 0)),
        ],
        out_specs=[ospec, ospec],
        out_shape=[out, out],
        compiler_params=_cparams(("parallel",)),
        name="mem_kv",
    )(mem2d, g, w, kg)


def _mem_attn_kernel(q_ref, mk_ref, mv_ref, qg_ref, o_ref):
    scale = HEAD_DIM ** -0.5
    for h in range(MEM_HEADS):
        sl = slice(h * HEAD_DIM, (h + 1) * HEAD_DIM)
        qh = (_rms(q_ref[:, sl].astype(F32), qg_ref[...]) * scale).astype(BF16)
        s = _dot_nt(qh, mk_ref[:, sl])
        p = jnp.exp(s - jnp.max(s, axis=1, keepdims=True))
        l = jnp.sum(p, axis=1, keepdims=True)
        o = _dot((p / l).astype(BF16), mv_ref[:, sl])
        o_ref[:, sl] = o.astype(BF16)


def _mem_attn(p, mk, mv, qg, batch, seq, l):
    t = p.shape[0]
    tq = 512
    nq = seq // tq
    m = mk.shape[0] // batch
    return pl.pallas_call(
        _mem_attn_kernel,
        grid=(batch, nq),
        in_specs=[
            pl.BlockSpec((tq, MEM_DIM), lambda b, i: (b * nq + i, P_MQ // MEM_DIM)),
            pl.BlockSpec((m, MEM_DIM), lambda b, i: (b, 0)),
            pl.BlockSpec((m, MEM_DIM), lambda b, i: (b, 0)),
            _lp(l, (1, HEAD_DIM), lambda b, i: (0, 0)),
        ],
        out_specs=pl.BlockSpec((tq, MEM_DIM), lambda b, i: (b * nq + i, 0)),
        out_shape=jax.ShapeDtypeStruct((t, MEM_DIM), BF16),
        compiler_params=_cparams(("parallel", "parallel")),
        name="mem_attn",
    )(p, mk, mv, qg)


CONV_HALO = 32
CONV_ROWS = 64


def _conv_kernel(cur_a_ref, cur_g_ref, prev_a_ref, prev_g_ref, inb_ref, cw_ref, cb_ref, lg_ref,
                 lb_ref, o_ref, u_ref, sh_ref):
    ts = cur_a_ref.shape[0]
    n = CONV_HALO + ts

    def glu(a, g):
        a = a.astype(F32) + inb_ref[:, :CONV_CH]
        g = g.astype(F32) + inb_ref[:, CONV_CH:]
        return a * jax.nn.sigmoid(g)

    u_prev = glu(prev_a_ref[ts - CONV_HALO:, :], prev_g_ref[ts - CONV_HALO:, :])
    u_ref[:CONV_HALO, :] = jnp.where(pl.program_id(1) == 0, 0.0, u_prev)
    u_ref[CONV_HALO:n, :] = glu(cur_a_ref[...], cur_g_ref[...])
    u_ref[n:, :] = jnp.zeros((SUBLANES, CONV_CH), F32)
    for r in range(1, SUBLANES):
        sh_ref[r - 1] = u_ref[r:r + n, :]

    first_tap = CONV_HALO - (CONV_WIDTH - 1)
    for t in range(ts // CONV_ROWS):
        base = t * CONV_ROWS
        y = jnp.broadcast_to(cb_ref[...], (CONV_ROWS, CONV_CH))
        for w in range(CONV_WIDTH):
            r = (first_tap + w) % SUBLANES
            a = base + first_tap + w - r
            rows = u_ref[a:a + CONV_ROWS, :] if r == 0 else sh_ref[r - 1, a:a + CONV_ROWS, :]
            y = y + rows * cw_ref[w:w + 1, :]
        mu = jnp.mean(y, axis=-1, keepdims=True)
        d = y - mu
        var = jnp.mean(d * d, axis=-1, keepdims=True)
        z = d * lax.rsqrt(var + EPS) * lg_ref[...] + lb_ref[...]
        o_ref[base:base + CONV_ROWS, :] = (z * jax.nn.sigmoid(z)).astype(BF16)


def _conv(p, inb, cw, cb, lg, lb, batch, seq, l):
    t = p.shape[0]
    ts = 512
    ns = seq // ts
    a_blk = P_GLU // CONV_CH
    vec = lambda w: _lp(l, (1, w), lambda b, i: (0, 0))
    cur = lambda k: pl.BlockSpec((ts, CONV_CH), lambda b, i: (b * ns + i, a_blk + k))
    prev = lambda k: pl.BlockSpec((ts, CONV_CH), lambda b, i: (b * ns + jnp.maximum(i - 1, 0), a_blk + k))
    return pl.pallas_call(
        _conv_kernel,
        grid=(batch, ns),
        in_specs=[
            cur(0), cur(1), prev(0), prev(1),
            vec(2 * CONV_CH),
            _lp(l, (CONV_WIDTH, CONV_CH), lambda b, i: (0, 0)),
            vec(CONV_CH), vec(CONV_CH), vec(CONV_CH),
        ],
        out_specs=pl.BlockSpec((ts, CONV_CH), lambda b, i: (b * ns + i, 0)),
        out_shape=jax.ShapeDtypeStruct((t, CONV_CH), BF16),
        scratch_shapes=[pltpu.VMEM((CONV_HALO + ts + SUBLANES, CONV_CH), F32),
                        pltpu.VMEM((SUBLANES - 1, CONV_HALO + ts, CONV_CH), F32)],
        compiler_params=_cparams(("parallel", "parallel")),
        name="conv",
    )(p, p, p, p, inb, cw, cb, lg, lb)


MERGE_TN = 512


def _merge_kernel(a_ref, m_ref, c_ref, g0_ref, g1_ref, g2_ref, gb0_ref, gb1_ref, gb2_ref,
                  wa_ref, wm_ref, wc_ref, cob_ref, o_ref):
    a, m, c = a_ref[...], m_ref[...], c_ref[...]
    for n in range(D_MODEL // MERGE_TN):
        sl = slice(n * MERGE_TN, (n + 1) * MERGE_TN)

        def gate(g_ref, gb_ref):
            return jax.nn.sigmoid(g_ref[:, sl].astype(F32) + gb_ref[:, sl])

        y = gate(g0_ref, gb0_ref) * _dot(a, wa_ref[:, sl])
        y = y + gate(g1_ref, gb1_ref) * _dot(m, wm_ref[:, sl])
        y = y + gate(g2_ref, gb2_ref) * (_dot(c, wc_ref[:, sl]) + cob_ref[:, sl])
        o_ref[:, sl] = y.astype(BF16)


def _outproj_kernel(x_ref, mg_ref, w_ref, g_ref, x1_ref, h2_ref):
    x1 = x_ref[...] + _dot(mg_ref[...], w_ref[...])
    x1_ref[...] = x1
    h2_ref[...] = _rms(x1, g_ref[...]).astype(BF16)


def _merge_out_kernel(a_ref, m_ref, c_ref, g0_ref, g1_ref, g2_ref, gb0_ref, gb1_ref, gb2_ref,
                      wa_ref, wm_ref, wc_ref, cob_ref, x_ref, wo_ref, ng_ref, x1_ref, h2_ref, mg_ref):
    _merge_kernel(a_ref, m_ref, c_ref, g0_ref, g1_ref, g2_ref, gb0_ref, gb1_ref, gb2_ref,
                  wa_ref, wm_ref, wc_ref, cob_ref, mg_ref)
    _outproj_kernel(x_ref, mg_ref, wo_ref, ng_ref, x1_ref, h2_ref)


def _merge_out(attn, memo, convo, p, gate_b, wa, wm, wc, cob, x, wo, ng, l):
    t = attn.shape[0]
    tm = 256
    gblk = P_GATES // D_MODEL
    row = lambda w: pl.BlockSpec((tm, w), lambda i: (i, 0))
    full = lambda r, c: _lp(l, (r, c), lambda i: (0, 0))
    gspec = lambda k: pl.BlockSpec((tm, D_MODEL), lambda i: (i, gblk + k))
    gbspec = lambda k: _lp(l, (1, D_MODEL), lambda i: (0, k))
    return pl.pallas_call(
        _merge_out_kernel,
        grid=(t // tm,),
        in_specs=[row(ATTN_DIM), row(MEM_DIM), row(CONV_CH),
                  gspec(0), gspec(1), gspec(2), gbspec(0), gbspec(1), gbspec(2),
                  full(ATTN_DIM, D_MODEL), full(MEM_DIM, D_MODEL), full(CONV_CH, D_MODEL),
                  full(1, D_MODEL), row(D_MODEL), full(D_MODEL, D_MODEL), full(1, D_MODEL)],
        out_specs=[row(D_MODEL), row(D_MODEL)],
        out_shape=[jax.ShapeDtypeStruct((t, D_MODEL), F32), jax.ShapeDtypeStruct((t, D_MODEL), BF16)],
        scratch_shapes=[pltpu.VMEM((tm, D_MODEL), BF16)],
        compiler_params=_cparams(("parallel",)),
        name="merge_out",
    )(attn, memo, convo, p, p, p, gate_b, gate_b, gate_b, wa, wm, wc, cob, x, wo, ng)


def _mlp_kernel(h_ref, x_ref, wu_ref, wd_ref, o_ref):
    @pl.when(pl.program_id(1) == 0)
    def _():
        o_ref[...] = x_ref[...]

    u = jnp.maximum(_dot(h_ref[...], wu_ref[...]), 0.0)
    o_ref[...] += _dot((u * u).astype(BF16), wd_ref[...])


def _mlp(h2, x1, wu, wd, l):
    t = x1.shape[0]
    tm, tf = 512, 1024
    return pl.pallas_call(
        _mlp_kernel,
        grid=(t // tm, FFN_DIM // tf),
        in_specs=[
            pl.BlockSpec((tm, D_MODEL), lambda i, f: (i, 0)),
            pl.BlockSpec((tm, D_MODEL), lambda i, f: (i, 0)),
            _lp(l, (D_MODEL, tf), lambda i, f: (0, f)),
            _lp(l, (tf, D_MODEL), lambda i, f: (f, 0)),
        ],
        out_specs=pl.BlockSpec((tm, D_MODEL), lambda i, f: (i, 0)),
        out_shape=jax.ShapeDtypeStruct((t, D_MODEL), F32),
        compiler_params=_cparams(("parallel", "arbitrary")),
        name="mlp",
    )(h2, x1, wu, wd)


def _split_w_in(w):
    narrow = IDX_DIM + IDX_HEADS
    wb = w.astype(BF16)
    head = wb[:, :, :P_HEAD]
    tail = wb[:, :, P_HEAD + narrow:]
    small = jnp.pad(wb[:, :, P_HEAD:P_HEAD + narrow], ((0, 0), (0, 0), (0, PS_WIDTH - narrow)))
    return head, tail, small


def kernel(x, mem, positions, norm1_g, w_in, q_norm_g, k_norm_g, mem_norm_g, w_mem_kv, mq_norm_g,
           mk_norm_g, conv_in_b, conv_w, conv_b, conv_ln_g, conv_ln_b, gate_b, w_attn_o, w_mem_o,
           w_conv_o, conv_o_b, w_out, norm2_g, w_up, w_down):
    batch, seq, _ = x.shape
    depth = w_in.shape[0]
    t = batch * seq
    xf = x.reshape(t, D_MODEL)
    mem2d = mem.reshape(batch * mem.shape[1], D_MODEL)
    tabs = _rope_tables(positions)

    vec = lambda v: v.reshape(depth, 1, -1)
    norm1_g, q_norm_g, k_norm_g, mem_norm_g, mq_norm_g, mk_norm_g = map(
        vec, (norm1_g, q_norm_g, k_norm_g, mem_norm_g, mq_norm_g, mk_norm_g))
    conv_in_b, conv_b, conv_ln_g, conv_ln_b, gate_b, conv_o_b, norm2_g = map(
        vec, (conv_in_b, conv_b, conv_ln_g, conv_ln_b, gate_b, conv_o_b, norm2_g))
    w_head, w_tail, w_small = _split_w_in(w_in)
    w_mem_kv, w_attn_o, w_mem_o, w_conv_o, w_out, w_up, w_down = (
        w.astype(BF16) for w in (w_mem_kv, w_attn_o, w_mem_o, w_conv_o, w_out, w_up, w_down))

    for l in range(depth):
        p, ps = _proj(xf, norm1_g, w_head, w_tail, w_small, l)
        qn, iqr, kn, vt, ika, ikb, iwt = _prep(p, ps, tabs, q_norm_g, k_norm_g, l)
        attn = _dsa(qn, iqr, iwt, kn, vt, ika, ikb, batch, seq)
        mk, mv = _mem_kv(mem2d, mem_norm_g, w_mem_kv, mk_norm_g, batch, l)
        memo = _mem_attn(p, mk, mv, mq_norm_g, batch, seq, l)
        convo = _conv(p, conv_in_b, conv_w, conv_b, conv_ln_g, conv_ln_b, batch, seq, l)
        x1, h2 = _merge_out(attn, memo, convo, p, gate_b, w_attn_o, w_mem_o, w_conv_o, conv_o_b,
                            xf, w_out, norm2_g, l)
        xf = _mlp(h2, x1, w_up, w_down, l)
    return xf.reshape(batch, seq, D_MODEL)
```

```python
import functools
import math

import jax
import jax.numpy as jnp
from jax import lax
from jax.experimental import pallas as pl
from jax.experimental.pallas import tpu as pltpu

F32 = jnp.float32
BF16 = jnp.bfloat16
I32 = jnp.int32

D_MODEL = 2048
HEAD_DIM = 128
N_HEADS = 8
N_KV_HEADS = 2
IDX_HEADS = 16
IDX_DIM = 64
TOPK = 256
CONV_CH = 512
CONV_WIDTH = 31
MEM_HEADS = 4
FFN_DIM = 4 * D_MODEL
ROPE_THETA = 500000.0
N_BRANCH = 3
EPS = 1e-6
ATTN_DIM = N_HEADS * HEAD_DIM
KV_DIM = N_KV_HEADS * HEAD_DIM
MEM_DIM = MEM_HEADS * HEAD_DIM
HEAD_ROT_HALF = HEAD_DIM // 8
IDX_ROT_HALF = IDX_DIM // 8

LANES = 128
VMEM_LIMIT = 56 * 1024 * 1024

P_Q = 0
P_K = P_Q + ATTN_DIM
P_V = P_K + KV_DIM
P_IQ = P_V + KV_DIM
P_HEAD = P_IQ + IDX_HEADS * IDX_DIM
P_GLU = P_HEAD
P_MQ = P_GLU + 2 * CONV_CH
P_GATES = P_MQ + MEM_DIM
P_WIDTH = P_GATES + N_BRANCH * D_MODEL
P_TAIL = P_WIDTH - P_HEAD
PS_WIDTH = LANES

QB = 128
KC = 512
SUB = 256
SLAB = 32
SUBLANES = 8
LOG2E = 1.4426950408889634
NEG_BIG = -1e30
INT_MIN = -2 ** 31
KEY_NEG_INF = -2139095041
KEY_POS_INF = 2139095040


def _cparams(sem, vmem=VMEM_LIMIT):
    return pltpu.CompilerParams(dimension_semantics=sem, vmem_limit_bytes=vmem)


def _lp(l, shape, imap):
    return pl.BlockSpec((pl.Squeezed(),) + shape, lambda *g: (l,) + imap(*g))


def _dot(a, b):
    return jnp.dot(a, b, preferred_element_type=F32)


def _dot_nt(a, b):
    return lax.dot_general(a, b, (((1,), (1,)), ((), ())), preferred_element_type=F32)


def _rms(xf, g):
    return xf * lax.rsqrt(jnp.mean(xf * xf, axis=-1, keepdims=True) + EPS) * g


def _rope_table_kernel(pos_ref, ch_ref, sh_ref, ci_ref, si_ref):
    pos = pos_ref[...].astype(F32)
    lane = lax.broadcasted_iota(I32, (1, LANES), 1)
    hh, ih = HEAD_ROT_HALF, IDX_ROT_HALF
    is_head = lane < hh
    fi = jnp.where(is_head, lane, lane - hh).astype(F32)
    inv = jnp.exp(fi * jnp.where(is_head, -math.log(ROPE_THETA) / hh, -math.log(ROPE_THETA) / ih))
    ang = pos * inv
    c = jnp.cos(ang)
    s = jnp.sin(ang)

    def place(x, shift):
        return pltpu.roll(x, shift % LANES, 1)

    second = (lane >= hh) & (lane < 2 * hh)
    ch_ref[...] = jnp.where(is_head, c, jnp.where(second, place(c, hh), 1.0))
    sh_ref[...] = jnp.where(is_head, -s, jnp.where(second, place(s, hh), 0.0))
    r = lane & (IDX_DIM - 1)
    upper = lane >= IDX_DIM

    def idx_table(x, sign_first, fill):
        first = jnp.where(upper, place(x, IDX_DIM - hh), place(x, -hh))
        second_half = jnp.where(upper, place(x, IDX_DIM - hh + ih), place(x, ih - hh))
        return jnp.where(r < ih, sign_first * first, jnp.where(r < 2 * ih, second_half, fill))

    ci_ref[...] = idx_table(c, 1.0, 1.0)
    si_ref[...] = idx_table(s, -1.0, 0.0)


def _swap_matrix(period, half):
    r = lax.broadcasted_iota(I32, (LANES, LANES), 0)
    c = lax.broadcasted_iota(I32, (LANES, LANES), 1)
    pos = c & (period - 1)
    src = jnp.where(pos < half, c + half, c - half)
    return jnp.where((pos < 2 * half) & (r == src), 1.0, 0.0).astype(BF16)


def _rope_tables(positions):
    t = positions.size
    tp = 512
    tab = jax.ShapeDtypeStruct((t, LANES), F32)
    spec = pl.BlockSpec((tp, LANES), lambda i: (i, 0))
    return pl.pallas_call(
        _rope_table_kernel,
        grid=(t // tp,),
        in_specs=[pl.BlockSpec((tp, 1), lambda i: (i, 0))],
        out_specs=[spec] * 4,
        out_shape=[tab] * 4,
        compiler_params=_cparams(("parallel",)),
        name="rope_tables",
    )(positions.reshape(t, 1))


def _rope(t, c, s, swap):
    hi = t.astype(BF16)
    lo = (t - hi.astype(F32)).astype(BF16)
    return t * c + (_dot(hi, swap) + _dot(lo, swap)) * s


PROJ_TN = 1280
PROJ_ROWS = 256


def _proj_kernel(x_ref, g_ref, wh_ref, wt_ref, ws_ref, o_ref, os_ref, h_ref):
    j = pl.program_id(1)

    @pl.when(j == 0)
    def _():
        for r in range(x_ref.shape[0] // PROJ_ROWS):
            rows = slice(r * PROJ_ROWS, (r + 1) * PROJ_ROWS)
            h = _rms(x_ref[rows, :], g_ref[...]).astype(BF16)
            h_ref[rows, :] = h
            os_ref[rows, :] = _dot(h, ws_ref[...])
            o_ref[rows, :] = _dot(h, wh_ref[...]).astype(o_ref.dtype)

    @pl.when((j > 0) & (j < P_HEAD // PROJ_TN))
    def _():
        o_ref[...] = _dot(h_ref[...], wh_ref[...]).astype(o_ref.dtype)

    @pl.when(j >= P_HEAD // PROJ_TN)
    def _():
        o_ref[...] = _dot(h_ref[...], wt_ref[...]).astype(o_ref.dtype)


def _proj(x, g, wh, wt, ws, l):
    t = x.shape[0]
    tm, tn = 1024, PROJ_TN
    n_head = P_HEAD // tn
    return pl.pallas_call(
        _proj_kernel,
        grid=(t // tm, P_WIDTH // tn),
        in_specs=[
            pl.BlockSpec((tm, D_MODEL), lambda i, j: (i, 0)),
            _lp(l, (1, D_MODEL), lambda i, j: (0, 0)),
            _lp(l, (D_MODEL, tn), lambda i, j: (0, jnp.minimum(j, n_head - 1))),
            _lp(l, (D_MODEL, tn), lambda i, j: (0, jnp.maximum(j - n_head, 0))),
            _lp(l, (D_MODEL, PS_WIDTH), lambda i, j: (0, 0)),
        ],
        out_specs=[
            pl.BlockSpec((tm, tn), lambda i, j: (i, j)),
            pl.BlockSpec((tm, PS_WIDTH), lambda i, j: (i, 0)),
        ],
        out_shape=[
            jax.ShapeDtypeStruct((t, P_WIDTH), BF16),
            jax.ShapeDtypeStruct((t, PS_WIDTH), F32),
        ],
        scratch_shapes=[pltpu.VMEM((tm, D_MODEL), BF16)],
        compiler_params=_cparams(("parallel", "arbitrary")),
        name="proj",
    )(x, g, wh, wt, ws)


def _prep_kernel(q_ref, iql_ref, iqh_ref, kv_ref, ps_ref, ch_ref, sh_ref, ci_ref, si_ref,
                 qg_ref, kg_ref, qn_ref, iqr_ref, kn_ref, vt_ref, ika_ref, ikb_ref, iwt_ref):
    ch, sh = ch_ref[...], sh_ref[...]
    ci, si = ci_ref[...], si_ref[...]
    swap_h = _swap_matrix(HEAD_DIM, HEAD_ROT_HALF)
    swap_i = _swap_matrix(IDX_DIM, IDX_ROT_HALF)
    scale = (HEAD_DIM ** -0.5) * LOG2E
    half_groups = N_HEADS // 2
    for h in range(N_HEADS):
        sl = slice(h * HEAD_DIM, (h + 1) * HEAD_DIM)
        qh = _rms(q_ref[:, sl].astype(F32), qg_ref[...])
        qn_ref[:, sl] = (_rope(qh, ch, sh, swap_h) * scale).astype(BF16)
        src = iql_ref if h < half_groups else iqh_ref
        hs = h % half_groups
        iqh = src[:, hs * LANES:(hs + 1) * LANES].astype(F32)
        iqr_ref[:, sl] = _rope(iqh, ci, si, swap_i).astype(BF16)
    for g in range(N_KV_HEADS):
        sl = slice(g * HEAD_DIM, (g + 1) * HEAD_DIM)
        kh = _rms(kv_ref[:, sl].astype(F32), kg_ref[...])
        kn_ref[:, sl] = _rope(kh, ch, sh, swap_h).astype(BF16)
        vh = kv_ref[:, KV_DIM + g * HEAD_DIM:KV_DIM + (g + 1) * HEAD_DIM].astype(F32)
        vt_ref[0, sl, :] = vh.T.astype(BF16)
    ps = ps_ref[...]
    lane = lax.broadcasted_iota(I32, (1, LANES), 1)
    ikr = jnp.where(lane < IDX_DIM, _rope(ps, ci, si, swap_i), 0.0)
    ika_ref[...] = ikr.astype(BF16)
    ikb_ref[...] = pltpu.roll(ikr, IDX_DIM, 1).astype(BF16)
    idx_scale = (IDX_DIM ** -0.5) * (IDX_HEADS ** -0.5)
    iws = jnp.where(lane < IDX_HEADS, pltpu.roll(ps, LANES - IDX_DIM, 1) * idx_scale, 0.0)
    iwt_ref[...] = iws.T


def _prep(p, ps, tabs, qg, kg, l):
    t = p.shape[0]
    tp = KC
    half_iq = IDX_HEADS * IDX_DIM // 2
    tab_spec = pl.BlockSpec((tp, LANES), lambda i: (i, 0))
    vec_spec = _lp(l, (1, HEAD_DIM), lambda i: (0, 0))

    def out(width, dtype):
        return (pl.BlockSpec((tp, width), lambda i: (i, 0)), jax.ShapeDtypeStruct((t, width), dtype))

    outs = [out(ATTN_DIM, BF16), out(ATTN_DIM, BF16), out(KV_DIM, BF16),
            (pl.BlockSpec((1, KV_DIM, tp), lambda i: (i, 0, 0)),
             jax.ShapeDtypeStruct((t // tp, KV_DIM, tp), BF16)),
            out(LANES, BF16), out(LANES, BF16),
            (pl.BlockSpec((LANES, tp), lambda i: (0, i)), jax.ShapeDtypeStruct((LANES, t), F32))]
    return pl.pallas_call(
        _prep_kernel,
        grid=(t // tp,),
        in_specs=[
            pl.BlockSpec((tp, ATTN_DIM), lambda i: (i, P_Q // ATTN_DIM)),
            pl.BlockSpec((tp, half_iq), lambda i: (i, P_IQ // half_iq)),
            pl.BlockSpec((tp, half_iq), lambda i: (i, P_IQ // half_iq + 1)),
            pl.BlockSpec((tp, 2 * KV_DIM), lambda i: (i, P_K // (2 * KV_DIM))),
            tab_spec,
        ] + [tab_spec] * 4 + [vec_spec, vec_spec],
        out_specs=[o[0] for o in outs],
        out_shape=[o[1] for o in outs],
        compiler_params=_cparams(("parallel",)),
        name="prep",
    )(p, p, p, p, ps, *tabs, qg, kg)


def _sortable(x):
    bits = lax.bitcast_convert_type(x, I32)
    return bits ^ ((bits >> 31) & 0x7FFFFFFF)


N_PAIRS = N_HEADS // 2
N_PART = 4


def _dsa_kernel(qn_ref, iq_ref, iwt_ref, kn_ref, vt_ref, ika_ref, ikb_ref, o_ref,
                key_ref, thr_ref, nge_ref, bias_ref, m_ref, l_ref, acc_ref, sa_ref, sb_ref, p_ref):
    j = pl.program_id(1)
    nck = j // (KC // QB) + 1
    seq = kn_ref.shape[0]

    iwt = iwt_ref[...]
    q_pos = j * QB + lax.broadcasted_iota(I32, (SUB, QB), 1)
    n_grp = IDX_HEADS // 2

    halves = KC // SUB
    nhalf = j // (SUB // QB) + 1

    def score_half(c, hf):
        start = pl.multiple_of(c * KC + hf * SUB, SUB)
        ka = ika_ref[pl.ds(start, SUB), :]
        kb = ikb_ref[pl.ds(start, SUB), :]
        acc = jnp.zeros((SUB, QB), F32)
        for a in range(n_grp // 2):
            b = a + n_grp // 2
            iq2 = jnp.concatenate([iq_ref[:, a * LANES:(a + 1) * LANES],
                                   iq_ref[:, b * LANES:(b + 1) * LANES]], axis=0)
            sa = _dot_nt(ka, iq2)
            sb = _dot_nt(kb, iq2)
            acc = acc + jnp.maximum(sa[:, :QB], 0.0) * iwt[2 * a:2 * a + 1, :]
            acc = acc + jnp.maximum(sb[:, :QB], 0.0) * iwt[2 * a + 1:2 * a + 2, :]
            acc = acc + jnp.maximum(sa[:, QB:], 0.0) * iwt[2 * b:2 * b + 1, :]
            acc = acc + jnp.maximum(sb[:, QB:], 0.0) * iwt[2 * b + 1:2 * b + 2, :]
        k_pos = start + lax.broadcasted_iota(I32, (SUB, QB), 0)
        acc = jnp.where(k_pos <= q_pos, acc, -jnp.inf)
        key_ref[c, hf * SUB:(hf + 1) * SUB, :] = _sortable(acc)

    def score_chunk(c, carry):
        for hf in range(halves):
            score_half(c, hf)
        return carry

    lax.fori_loop(0, nhalf // halves, score_chunk, 0)

    @pl.when(nhalf % halves == 1)
    def _():
        score_half(nck - 1, 0)
        key_ref[nck - 1, SUB:, :] = jnp.full((KC - SUB, QB), INT_MIN, I32)

    n_loaded = jnp.broadcast_to(nhalf * SUB, (SUBLANES, QB)).astype(I32)

    def search(nh):
        def search_pass(p, carry):
            tu, n_at = carry
            cand_u = tu | lax.shift_left(jnp.int32(1), 31 - p)
            cand = cand_u ^ INT_MIN
            parts = [jnp.zeros((SUBLANES, QB), I32)] * N_PART
            for r in range(nh * SUB // SUBLANES):
                row = (r * SUBLANES) % KC
                k = key_ref[(r * SUBLANES) // KC, row:row + SUBLANES, :]
                parts[r % N_PART] = parts[r % N_PART] + jnp.where(k >= cand, 1, 0)
            tot = (parts[0] + parts[1]) + (parts[2] + parts[3])
            n = jnp.broadcast_to(jnp.sum(tot, axis=0, keepdims=True), (SUBLANES, QB))
            ok = n >= TOPK
            return jnp.where(ok, cand_u, tu), jnp.where(ok, n, n_at)

        tu, n_ge = lax.fori_loop(0, 32, search_pass, (jnp.zeros((SUBLANES, QB), I32), n_loaded))
        thr_ref[...] = tu ^ INT_MIN
        nge_ref[...] = n_ge

    for nh in range(1, key_ref.shape[0] * halves + 1):
        pl.when(nhalf == nh)(functools.partial(search, nh))
    thr = thr_ref[...]
    n_ge = nge_ref[...]

    row_iota = lax.broadcasted_iota(I32, (SUBLANES, QB), 0)
    tied = (n_ge > TOPK) & (thr > KEY_NEG_INF)
    any_tied = jnp.max(jnp.where(tied, 1, 0)) > 0

    def count32(pred):
        def chunk(c, parts):
            parts = list(parts)
            for r in range(KC // SUBLANES):
                k = key_ref[c, r * SUBLANES:(r + 1) * SUBLANES, :]
                hit = pred(k, c * KC + r * SUBLANES)
                parts[r % N_PART] = parts[r % N_PART] + jnp.where(hit, 1, 0)
            return tuple(parts)

        zero = jnp.zeros((SUBLANES, QB), I32)
        parts = lax.fori_loop(0, nck, chunk, (zero,) * N_PART)
        tot = (parts[0] + parts[1]) + (parts[2] + parts[3])
        return jnp.broadcast_to(jnp.sum(tot, axis=0, keepdims=True), (SUBLANES, QB))

    thr_fin = jnp.maximum(thr, KEY_NEG_INF + 1)

    @pl.when(jnp.logical_not(any_tied))
    def _():
        def bias_chunk(c, carry):
            for r in range(KC // SUBLANES):
                sl = slice(r * SUBLANES, (r + 1) * SUBLANES)
                k = key_ref[c, sl, :]
                sel = (k >= thr_fin) & (k < KEY_POS_INF)
                bias_ref[c, sl, :] = jnp.where(sel, 0.0, NEG_BIG)
            return carry

        lax.fori_loop(0, nck, bias_chunk, 0)

    @pl.when(any_tied)
    def _():
        need = TOPK - count32(lambda k, _: k > thr)

        def pos_pass(p, x):
            cand = x | lax.shift_left(jnp.int32(1), (seq.bit_length() - 2) - p)
            n = count32(lambda k, r0: (k == thr) & (row_iota + r0 < cand))
            return jnp.where(n < need, cand, x)

        xlim = lax.fori_loop(0, seq.bit_length() - 1, pos_pass, jnp.zeros((SUBLANES, QB), I32))

        def bias_chunk(c, carry):
            for r in range(KC // SUBLANES):
                sl = slice(r * SUBLANES, (r + 1) * SUBLANES)
                k = key_ref[c, sl, :]
                pos = row_iota + (c * KC + r * SUBLANES)
                sel = (k > thr) | ((k == thr) & (pos <= xlim))
                sel = sel & (k >= thr_fin) & (k < KEY_POS_INF)
                bias_ref[c, sl, :] = jnp.where(sel, 0.0, NEG_BIG)
            return carry

        lax.fori_loop(0, nck, bias_chunk, 0)

    m_ref[...] = jnp.full(m_ref.shape, NEG_BIG, F32)
    l_ref[...] = jnp.zeros(l_ref.shape, F32)
    acc_ref[...] = jnp.zeros(acc_ref.shape, F32)
    group = N_HEADS // N_KV_HEADS

    def logits(c, s_ref):
        col_max = []
        for pr in range(N_PAIRS):
            g = (2 * pr) // group
            q2 = jnp.concatenate([qn_ref[:, (2 * pr) * HEAD_DIM:(2 * pr + 1) * HEAD_DIM],
                                  qn_ref[:, (2 * pr + 1) * HEAD_DIM:(2 * pr + 2) * HEAD_DIM]],
                                 axis=0)
            mx = jnp.full((SUBLANES, 2 * QB), NEG_BIG, F32)
            for hf in range(KC // SUB):
                start = pl.multiple_of(c * KC + hf * SUB, SUB)
                kc = kn_ref[pl.ds(start, SUB), g * HEAD_DIM:(g + 1) * HEAD_DIM]
                bias = bias_ref[c, hf * SUB:(hf + 1) * SUB, :]
                s = _dot_nt(kc, q2) + jnp.concatenate([bias, bias], axis=1)
                s_ref[pr, hf * SUB:(hf + 1) * SUB, :] = s
                for r in range(SUB // SUBLANES):
                    mx = jnp.maximum(mx, s[r * SUBLANES:(r + 1) * SUBLANES, :])
            col_max.append(jnp.broadcast_to(jnp.max(mx, axis=0, keepdims=True), (SUBLANES, 2 * QB)))
        return tuple(col_max)

    def finish(c, s_ref, col_max):
        alphas = []
        for pr in range(N_PAIRS):
            m_prev = m_ref[pr]
            m_new = jnp.maximum(m_prev, col_max[pr])
            alpha = jnp.exp2(m_prev - m_new)
            lsum = jnp.zeros((SUBLANES, 2 * QB), F32)
            for t in range(KC // SLAB):
                rows = slice(t * SLAB, (t + 1) * SLAB)
                p = jnp.exp2(s_ref[pr, rows, :] - m_new[0:1, :])
                for r in range(SLAB // SUBLANES):
                    lsum = lsum + p[r * SUBLANES:(r + 1) * SUBLANES, :]
                p_ref[pr, rows, :] = p.astype(BF16)
            m_ref[pr] = m_new
            l_ref[pr] = alpha * l_ref[pr] + jnp.sum(lsum, axis=0, keepdims=True)
            alphas.append(alpha)
        for pr in range(N_PAIRS):
            g = (2 * pr) // group
            vt = vt_ref[c, g * HEAD_DIM:(g + 1) * HEAD_DIM, :]
            acc_ref[pr] = acc_ref[pr] * alphas[pr][0:1, :] + _dot(vt, p_ref[pr])

    def attn_step(i, col_max):
        c = 2 * i
        mid = logits(c + 1, sb_ref)
        finish(c, sa_ref, col_max)
        nxt = logits(c + 2, sa_ref)
        finish(c + 1, sb_ref, mid)
        return nxt

    n_steps = (nck - 1) // 2
    tail = 2 * n_steps
    col_max = lax.fori_loop(0, n_steps, attn_step, logits(0, sa_ref))

    @pl.when(tail == nck - 1)
    def _():
        finish(tail, sa_ref, col_max)

    @pl.when(tail != nck - 1)
    def _():
        mid = logits(tail + 1, sb_ref)
        finish(tail, sa_ref, col_max)
        finish(tail + 1, sb_ref, mid)

    for pr in range(N_PAIRS):
        o_t = acc_ref[pr] / l_ref[pr][0:1, :]
        for i in range(2):
            h = 2 * pr + i
            o_ref[:, h * HEAD_DIM:(h + 1) * HEAD_DIM] = o_t[:, i * QB:(i + 1) * QB].T.astype(BF16)


def _dsa(qn, iqr, iwt, kn, vt, ika, ikb, batch, seq):
    t = qn.shape[0]
    nb = seq // QB
    nchunk = seq // KC
    qmap = lambda b, j: (b * nb + j, 0)
    bmap = lambda b, j: (b, 0)
    return pl.pallas_call(
        _dsa_kernel,
        grid=(batch, nb),
        in_specs=[
            pl.BlockSpec((QB, ATTN_DIM), qmap),
            pl.BlockSpec((QB, ATTN_DIM), qmap),
            pl.BlockSpec((IDX_HEADS, QB), lambda b, j: (0, b * nb + j)),
            pl.BlockSpec((seq, KV_DIM), bmap),
            pl.BlockSpec((nchunk, KV_DIM, KC), lambda b, j: (b, 0, 0)),
            pl.BlockSpec((seq, LANES), bmap),
            pl.BlockSpec((seq, LANES), bmap),
        ],
        out_specs=pl.BlockSpec((QB, ATTN_DIM), qmap),
        out_shape=jax.ShapeDtypeStruct((t, ATTN_DIM), BF16),
        scratch_shapes=[
            pltpu.VMEM((nchunk, KC, QB), I32),
            pltpu.VMEM((SUBLANES, QB), I32),
            pltpu.VMEM((SUBLANES, QB), I32),
            pltpu.VMEM((nchunk, KC, QB), F32),
            pltpu.VMEM((N_PAIRS, SUBLANES, 2 * QB), F32),
            pltpu.VMEM((N_PAIRS, SUBLANES, 2 * QB), F32),
            pltpu.VMEM((N_PAIRS, HEAD_DIM, 2 * QB), F32),
            pltpu.VMEM((N_PAIRS, KC, 2 * QB), F32),
            pltpu.VMEM((N_PAIRS, KC, 2 * QB), F32),
            pltpu.VMEM((N_PAIRS, KC, 2 * QB), BF16),
        ],
        compiler_params=_cparams(("parallel", "arbitrary")),
        name="dsa",
    )(qn, iqr, iwt, kn, vt, ika, ikb)


def _mem_kv_kernel(mem_ref, g_ref, w_ref, kg_ref, mk_ref, mv_ref):
    m = _rms(mem_ref[...], g_ref[...]).astype(BF16)
    kv = _dot(m, w_ref[...])
    for h in range(MEM_HEADS):
        sl = slice(h * HEAD_DIM, (h + 1) * HEAD_DIM)
        mk_ref[:, sl] = _rms(kv[:, sl], kg_ref[...]).astype(BF16)
    mv_ref[...] = kv[:, MEM_DIM:].astype(BF16)


def _mem_kv(mem2d, g, w, kg, batch, l):
    n = mem2d.shape[0]
    m = n // batch
    out = jax.ShapeDtypeStruct((n, MEM_DIM), BF16)
    ospec = pl.BlockSpec((m, MEM_DIM), lambda b: (b, 0))
    return pl.pallas_call(
        _mem_kv_kernel,
        grid=(batch,),
        in_specs=[
            pl.BlockSpec((m, D_MODEL), lambda b: (b, 0)),
            _lp(l, (1, D_MODEL), lambda b: (0, 0)),
            _lp(l, (D_MODEL, 2 * MEM_DIM), lambda b: (0, 0)),
            _lp(l, (1, HEAD_DIM), lambda b: (0, 0)),
        ],
        out_specs=[ospec, ospec],
        out_shape=[out, out],
        compiler_params=_cparams(("parallel",)),
        name="mem_kv",
    )(mem2d, g, w, kg)


def _mem_attn_kernel(q_ref, mk_ref, mv_ref, qg_ref, o_ref):
    scale = HEAD_DIM ** -0.5
    for h in range(MEM_HEADS):
        sl = slice(h * HEAD_DIM, (h + 1) * HEAD_DIM)
        qh = (_rms(q_ref[:, sl].astype(F32), qg_ref[...]) * scale).astype(BF16)
        s = _dot_nt(qh, mk_ref[:, sl])
        p = jnp.exp(s - jnp.max(s, axis=1, keepdims=True))
        l = jnp.sum(p, axis=1, keepdims=True)
        o = _dot((p / l).astype(BF16), mv_ref[:, sl])
        o_ref[:, sl] = o.astype(BF16)


CONV_HALO = 32
CONV_ROWS = 64


def _conv_kernel(cur_a_ref, cur_g_ref, prev_a_ref, prev_g_ref, inb_ref, cw_ref, cb_ref, lg_ref,
                 lb_ref, o_ref, u_ref, sh_ref):
    ts = cur_a_ref.shape[0]
    n = CONV_HALO + ts

    def glu(a, g):
        a = a.astype(F32) + inb_ref[:, :CONV_CH]
        g = g.astype(F32) + inb_ref[:, CONV_CH:]
        return a * jax.nn.sigmoid(g)

    u_prev = glu(prev_a_ref[ts - CONV_HALO:, :], prev_g_ref[ts - CONV_HALO:, :])
    u_ref[:CONV_HALO, :] = jnp.where(pl.program_id(1) == 0, 0.0, u_prev)
    u_ref[CONV_HALO:n, :] = glu(cur_a_ref[...], cur_g_ref[...])
    u_ref[n:, :] = jnp.zeros((SUBLANES, CONV_CH), F32)
    for r in range(1, SUBLANES):
        sh_ref[r - 1] = u_ref[r:r + n, :]

    first_tap = CONV_HALO - (CONV_WIDTH - 1)
    for t in range(ts // CONV_ROWS):
        base = t * CONV_ROWS
        y = jnp.broadcast_to(cb_ref[...], (CONV_ROWS, CONV_CH))
        for w in range(CONV_WIDTH):
            r = (first_tap + w) % SUBLANES
            a = base + first_tap + w - r
            rows = u_ref[a:a + CONV_ROWS, :] if r == 0 else sh_ref[r - 1, a:a + CONV_ROWS, :]
            y = y + rows * cw_ref[w:w + 1, :]
        mu = jnp.mean(y, axis=-1, keepdims=True)
        d = y - mu
        var = jnp.mean(d * d, axis=-1, keepdims=True)
        z = d * lax.rsqrt(var + EPS) * lg_ref[...] + lb_ref[...]
        o_ref[base:base + CONV_ROWS, :] = (z * jax.nn.sigmoid(z)).astype(BF16)


def _side_kernel(q_ref, mk_ref, mv_ref, qg_ref, cur_a_ref, cur_g_ref, prev_a_ref, prev_g_ref,
                 inb_ref, cw_ref, cb_ref, lg_ref, lb_ref, mo_ref, co_ref, u_ref, sh_ref):
    _mem_attn_kernel(q_ref, mk_ref, mv_ref, qg_ref, mo_ref)
    _conv_kernel(cur_a_ref, cur_g_ref, prev_a_ref, prev_g_ref, inb_ref, cw_ref, cb_ref, lg_ref,
                 lb_ref, co_ref, u_ref, sh_ref)


def _side_branches(p, mk, mv, qg, inb, cw, cb, lg, lb, batch, seq, l):
    t = p.shape[0]
    ts = 512
    ns = seq // ts
    m = mk.shape[0] // batch
    a_blk = P_GLU // CONV_CH
    vec = lambda w: _lp(l, (1, w), lambda b, i: (0, 0))
    cur = lambda k: pl.BlockSpec((ts, CONV_CH), lambda b, i: (b * ns + i, a_blk + k))
    prev = lambda k: pl.BlockSpec((ts, CONV_CH), lambda b, i: (b * ns + jnp.maximum(i - 1, 0), a_blk + k))
    row = lambda w: pl.BlockSpec((ts, w), lambda b, i: (b * ns + i, 0))
    return pl.pallas_call(
        _side_kernel,
        grid=(batch, ns),
        in_specs=[
            pl.BlockSpec((ts, MEM_DIM), lambda b, i: (b * ns + i, P_MQ // MEM_DIM)),
            pl.BlockSpec((m, MEM_DIM), lambda b, i: (b, 0)),
            pl.BlockSpec((m, MEM_DIM), lambda b, i: (b, 0)),
            vec(HEAD_DIM),
            cur(0), cur(1), prev(0), prev(1),
            vec(2 * CONV_CH),
            _lp(l, (CONV_WIDTH, CONV_CH), lambda b, i: (0, 0)),
            vec(CONV_CH), vec(CONV_CH), vec(CONV_CH),
        ],
        out_specs=[row(MEM_DIM), row(CONV_CH)],
        out_shape=[jax.ShapeDtypeStruct((t, MEM_DIM), BF16), jax.ShapeDtypeStruct((t, CONV_CH), BF16)],
        scratch_shapes=[pltpu.VMEM((CONV_HALO + ts + SUBLANES, CONV_CH), F32),
                        pltpu.VMEM((SUBLANES - 1, CONV_HALO + ts, CONV_CH), F32)],
        compiler_params=_cparams(("parallel", "parallel")),
        name="side_branches",
    )(p, mk, mv, qg, p, p, p, p, inb, cw, cb, lg, lb)


MERGE_TN = 512


def _merge_kernel(a_ref, m_ref, c_ref, g0_ref, g1_ref, g2_ref, gb0_ref, gb1_ref, gb2_ref,
                  wa_ref, wm_ref, wc_ref, cob_ref, o_ref):
    a, m, c = a_ref[...], m_ref[...], c_ref[...]
    for n in range(D_MODEL // MERGE_TN):
        sl = slice(n * MERGE_TN, (n + 1) * MERGE_TN)

        def gate(g_ref, gb_ref):
            return jax.nn.sigmoid(g_ref[:, sl].astype(F32) + gb_ref[:, sl])

        y = gate(g0_ref, gb0_ref) * _dot(a, wa_ref[:, sl])
        y = y + gate(g1_ref, gb1_ref) * _dot(m, wm_ref[:, sl])
        y = y + gate(g2_ref, gb2_ref) * (_dot(c, wc_ref[:, sl]) + cob_ref[:, sl])
        o_ref[:, sl] = y.astype(BF16)


def _outproj_kernel(x_ref, mg_ref, w_ref, g_ref, x1_ref, h2_ref):
    x1 = x_ref[...] + _dot(mg_ref[...], w_ref[...])
    x1_ref[...] = x1
    h2_ref[...] = _rms(x1, g_ref[...]).astype(BF16)


def _merge_out_kernel(a_ref, m_ref, c_ref, g0_ref, g1_ref, g2_ref, gb0_ref, gb1_ref, gb2_ref,
                      wa_ref, wm_ref, wc_ref, cob_ref, x_ref, wo_ref, ng_ref, x1_ref, h2_ref, mg_ref):
    _merge_kernel(a_ref, m_ref, c_ref, g0_ref, g1_ref, g2_ref, gb0_ref, gb1_ref, gb2_ref,
                  wa_ref, wm_ref, wc_ref, cob_ref, mg_ref)
    _outproj_kernel(x_ref, mg_ref, wo_ref, ng_ref, x1_ref, h2_ref)


def _merge_out(attn, memo, convo, p, gate_b, wa, wm, wc, cob, x, wo, ng, l):
    t = attn.shape[0]
    tm = 256
    gblk = P_GATES // D_MODEL
    row = lambda w: pl.BlockSpec((tm, w), lambda i: (i, 0))
    full = lambda r, c: _lp(l, (r, c), lambda i: (0, 0))
    gspec = lambda k: pl.BlockSpec((tm, D_MODEL), lambda i: (i, gblk + k))
    gbspec = lambda k: _lp(l, (1, D_MODEL), lambda i: (0, k))
    return pl.pallas_call(
        _merge_out_kernel,
        grid=(t // tm,),
        in_specs=[row(ATTN_DIM), row(MEM_DIM), row(CONV_CH),
                  gspec(0), gspec(1), gspec(2), gbspec(0), gbspec(1), gbspec(2),
                  full(ATTN_DIM, D_MODEL), full(MEM_DIM, D_MODEL), full(CONV_CH, D_MODEL),
                  full(1, D_MODEL), row(D_MODEL), full(D_MODEL, D_MODEL), full(1, D_MODEL)],
        out_specs=[row(D_MODEL), row(D_MODEL)],
        out_shape=[jax.ShapeDtypeStruct((t, D_MODEL), F32), jax.ShapeDtypeStruct((t, D_MODEL), BF16)],
        scratch_shapes=[pltpu.VMEM((tm, D_MODEL), BF16)],
        compiler_params=_cparams(("parallel",)),
        name="merge_out",
    )(attn, memo, convo, p, p, p, gate_b, gate_b, gate_b, wa, wm, wc, cob, x, wo, ng)


def _mlp_kernel(h_ref, x_ref, wu_ref, wd_ref, o_ref):
    @pl.when(pl.program_id(1) == 0)
    def _():
        o_ref[...] = x_ref[...]

    u = jnp.maximum(_dot(h_ref[...], wu_ref[...]), 0.0)
    o_ref[...] += _dot((u * u).astype(BF16), wd_ref[...])


def _mlp(h2, x1, wu, wd, l):
    t = x1.shape[0]
    tm, tf = 512, 1024
    return pl.pallas_call(
        _mlp_kernel,
        grid=(t // tm, FFN_DIM // tf),
        in_specs=[
            pl.BlockSpec((tm, D_MODEL), lambda i, f: (i, 0)),
            pl.BlockSpec((tm, D_MODEL), lambda i, f: (i, 0)),
            _lp(l, (D_MODEL, tf), lambda i, f: (0, f)),
            _lp(l, (tf, D_MODEL), lambda i, f: (f, 0)),
        ],
        out_specs=pl.BlockSpec((tm, D_MODEL), lambda i, f: (i, 0)),
        out_shape=jax.ShapeDtypeStruct((t, D_MODEL), F32),
        compiler_params=_cparams(("parallel", "arbitrary")),
        name="mlp",
    )(h2, x1, wu, wd)


def _split_w_in(w):
    narrow = IDX_DIM + IDX_HEADS
    wb = w.astype(BF16)
    head = wb[:, :, :P_HEAD]
    tail = wb[:, :, P_HEAD + narrow:]
    small = jnp.pad(wb[:, :, P_HEAD:P_HEAD + narrow], ((0, 0), (0, 0), (0, PS_WIDTH - narrow)))
    return head, tail, small


def kernel(x, mem, positions, norm1_g, w_in, q_norm_g, k_norm_g, mem_norm_g, w_mem_kv, mq_norm_g,
           mk_norm_g, conv_in_b, conv_w, conv_b, conv_ln_g, conv_ln_b, gate_b, w_attn_o, w_mem_o,
           w_conv_o, conv_o_b, w_out, norm2_g, w_up, w_down):
    batch, seq, _ = x.shape
    depth = w_in.shape[0]
    t = batch * seq
    xf = x.reshape(t, D_MODEL)
    mem2d = mem.reshape(batch * mem.shape[1], D_MODEL)
    tabs = _rope_tables(positions)

    vec = lambda v: v.reshape(depth, 1, -1)
    norm1_g, q_norm_g, k_norm_g, mem_norm_g, mq_norm_g, mk_norm_g = map(
        vec, (norm1_g, q_norm_g, k_norm_g, mem_norm_g, mq_norm_g, mk_norm_g))
    conv_in_b, conv_b, conv_ln_g, conv_ln_b, gate_b, conv_o_b, norm2_g = map(
        vec, (conv_in_b, conv_b, conv_ln_g, conv_ln_b, gate_b, conv_o_b, norm2_g))
    w_head, w_tail, w_small = _split_w_in(w_in)
    w_mem_kv, w_attn_o, w_mem_o, w_conv_o, w_out, w_up, w_down = (
        w.astype(BF16) for w in (w_mem_kv, w_attn_o, w_mem_o, w_conv_o, w_out, w_up, w_down))

    for l in range(depth):
        p, ps = _proj(xf, norm1_g, w_head, w_tail, w_small, l)
        qn, iqr, kn, vt, ika, ikb, iwt = _prep(p, ps, tabs, q_norm_g, k_norm_g, l)
        attn = _dsa(qn, iqr, iwt, kn, vt, ika, ikb, batch, seq)
        mk, mv = _mem_kv(mem2d, mem_norm_g, w_mem_kv, mk_norm_g, batch, l)
        memo, convo = _side_branches(p, mk, mv, mq_norm_g, conv_in_b, conv_w, conv_b, conv_ln_g,
                                     conv_ln_b, batch, seq, l)
        x1, h2 = _merge_out(attn, memo, convo, p, gate_b, w_attn_o, w_mem_o, w_conv_o, conv_o_b,
                            xf, w_out, norm2_g, l)
        xf = _mlp(h2, x1, w_up, w_down, l)
    return xf.reshape(batch, seq, D_MODEL)
```
